```python
import jax
import jax.numpy as jnp
from jax import lax
import numpy as np

D_MODEL = 2048
BATCH = 4
SEQ = 2048
DEPTH = 1
DEC_BATCH = 32
DEC_SEQ = 4
PAST_LEN = 16384
PAGE_SIZE = 128

HEAD_DIM = 128
DSA_HEADS = 8
DSA_KV_HEADS = 4
FOX_HEADS = 8
FOX_KV_HEADS = 4
IDX_HEADS = 16
IDX_DIM = 64
TOPK_MAX = 256
ROPE_THETA = 500000.0
ROT_DIM = HEAD_DIM // 4
IDX_ROT_DIM = IDX_DIM // 4
D_FF = 5632
CONV_W = 3
Q_BLOCK = 128
RMS_EPS = 1e-6
COL_SIZES = (
    DSA_HEADS * HEAD_DIM,
    DSA_KV_HEADS * HEAD_DIM,
    DSA_KV_HEADS * HEAD_DIM,
    IDX_HEADS * IDX_DIM,
    IDX_DIM,
    IDX_HEADS,
    FOX_HEADS * HEAD_DIM,
    FOX_KV_HEADS * HEAD_DIM,
    FOX_KV_HEADS * HEAD_DIM,
    FOX_HEADS,
    D_MODEL,
    D_MODEL,
)
IN_COLS = sum(COL_SIZES)

kernel_name = 'gated_dsa_fox_convffn_step'


def rmsnorm(x, g):
    xf = x.astype(jnp.float32)
    xf = xf * lax.rsqrt(jnp.mean(xf * xf, axis=-1, keepdims=True) + RMS_EPS)
    return xf.astype(x.dtype) * g


def partial_rope(x, pos, rot_dim):
    half = rot_dim // 2
    inv_freq = jnp.power(ROPE_THETA, -jnp.arange(half, dtype=jnp.float32) * (2.0 / rot_dim))
    ang = pos.astype(jnp.float32)[:, None] * inv_freq[None, :]
    cos = jnp.cos(ang)[None, :, None, :].astype(x.dtype)
    sin = jnp.sin(ang)[None, :, None, :].astype(x.dtype)
    x1 = x[..., :half]
    x2 = x[..., half:rot_dim]
    return jnp.concatenate([x1 * cos - x2 * sin, x1 * sin + x2 * cos, x[..., rot_dim:]], axis=-1)


def project(h, pos, w_in, b_forget):
    B, S, _ = h.shape
    offs = [int(o) for o in np.cumsum(COL_SIZES)[:-1]]
    qa, ka, va, iq, ik, iw, qb, kb, vb, fl, ga, gb = jnp.split(h @ w_in, offs, axis=-1)
    qa = partial_rope(qa.reshape(B, S, DSA_HEADS, HEAD_DIM), pos, ROT_DIM)
    ka = partial_rope(ka.reshape(B, S, DSA_KV_HEADS, HEAD_DIM), pos, ROT_DIM)
    va = va.reshape(B, S, DSA_KV_HEADS, HEAD_DIM)
    iq = partial_rope(iq.reshape(B, S, IDX_HEADS, IDX_DIM), pos, IDX_ROT_DIM)
    ik = partial_rope(ik.reshape(B, S, 1, IDX_DIM), pos, IDX_ROT_DIM)[:, :, 0]
    qb = qb.reshape(B, S, FOX_HEADS, HEAD_DIM)
    kb = kb.reshape(B, S, FOX_KV_HEADS, HEAD_DIM)
    vb = vb.reshape(B, S, FOX_KV_HEADS, HEAD_DIM)
    logf = jax.nn.log_sigmoid((fl + b_forget).astype(jnp.float32))
    return qa, ka, va, iq, ik, iw, qb, kb, vb, logf, ga, gb


def indexer_scores(iq, iw, ik):
    s = jnp.einsum('bthd,bld->bthl', iq, ik, preferred_element_type=jnp.float32)
    return jnp.einsum('bthl,bth->btl', jax.nn.relu(s), iw.astype(jnp.float32)) * (IDX_HEADS * IDX_DIM) ** -0.5


def gather_rows(a, idx):
    return jax.vmap(lambda ab, ib: ab[ib])(a, idx)


def sparse_attend(q, k_sel, v_sel, valid):
    B, T, H, D = q.shape
    hkv = k_sel.shape[3]
    qg = q.reshape(B, T, hkv, H // hkv, D)
    logits = jnp.einsum('btkgd,btnkd->btkgn', qg, k_sel, preferred_element_type=jnp.float32) * D ** -0.5
    logits = jnp.where(valid[:, :, None, None, :], logits, -jnp.inf)
    p = jax.nn.softmax(logits, axis=-1).astype(v_sel.dtype)
    return jnp.einsum('btkgn,btnkd->btkgd', p, v_sel).reshape(B, T, H * D)


def dsa_prompt(qa, ka, va, iq, ik, iw):
    B, S = qa.shape[:2]
    topk = min(TOPK_MAX, S // 4)
    keys = jnp.arange(S)

    def block(i):
        start = i * Q_BLOCK
        sl = lambda a: lax.dynamic_slice_in_dim(a, start, Q_BLOCK, axis=1)
        t = start + jnp.arange(Q_BLOCK)
        scores = indexer_scores(sl(iq), sl(iw), ik)
        scores = jnp.where((keys[None, :] <= t[:, None])[None], scores, -jnp.inf)
        _, idx = lax.top_k(scores, topk)
        valid = idx <= t[None, :, None]
        return sparse_attend(sl(qa), gather_rows(ka, idx), gather_rows(va, idx), valid)

    o = lax.map(block, jnp.arange(S // Q_BLOCK))
    return o.transpose(1, 0, 2, 3).reshape(B, S, -1)


def dsa_sample(qa, ka, va, iq, ik, iw, cache_k, cache_v, cache_ik, page_table):
    Bd, T = qa.shape[:2]
    n_pages = page_table.shape[1]
    past = n_pages * PAGE_SIZE
    L = past + T
    topk = min(TOPK_MAX, L // 4)
    ik_all = jnp.concatenate([cache_ik[page_table].reshape(Bd, past, IDX_DIM), ik], axis=1)
    t = past + jnp.arange(T)
    scores = indexer_scores(iq, iw, ik_all)
    scores = jnp.where((jnp.arange(L)[None, :] <= t[:, None])[None], scores, -jnp.inf)
    _, idx = lax.top_k(scores, topk)
    valid = idx <= t[None, :, None]
    in_past = idx < past
    pidx = jnp.minimum(idx, past - 1)
    phys = jax.vmap(lambda pt, p: pt[p])(page_table, pidx // PAGE_SIZE)
    row = pidx % PAGE_SIZE
    nidx = jnp.clip(idx - past, 0, T - 1)

    def select(cache, new):
        return jnp.where(in_past[..., None, None], cache[phys, row], gather_rows(new, nidx))

    return sparse_attend(qa, select(cache_k, ka), select(cache_v, va), valid)


def fox_prompt(qb, kb, vb, logf):
    B, S, H, D = qb.shape
    hkv = kb.shape[2]
    g = H // hkv
    c = lax.cumsum(logf, axis=1)
    c_k = c.reshape(B, S, hkv, g).transpose(0, 2, 3, 1)
    keys = jnp.arange(S)

    def block(i):
        start = i * Q_BLOCK
        sl = lambda a: lax.dynamic_slice_in_dim(a, start, Q_BLOCK, axis=1)
        q = sl(qb).reshape(B, Q_BLOCK, hkv, g, D)
        cq = sl(c).reshape(B, Q_BLOCK, hkv, g).transpose(0, 2, 3, 1)
        t = start + jnp.arange(Q_BLOCK)
        logits = jnp.einsum('bqkgd,bskd->bkgqs', q, kb, preferred_element_type=jnp.float32) * D ** -0.5
        logits = logits + cq[..., None] - c_k[..., None, :]
        logits = jnp.where(keys[None, :] <= t[:, None], logits, -jnp.inf)
        p = jax.nn.softmax(logits, axis=-1).astype(vb.dtype)
        return jnp.einsum('bkgqs,bskd->bqkgd', p, vb).reshape(B, Q_BLOCK, H * D)

    o = lax.map(block, jnp.arange(S // Q_BLOCK))
    return o.transpose(1, 0, 2, 3).reshape(B, S, -1)


def fox_sample(qb, kb, vb, logf, cache_k, cache_v, cache_logf, page_table):
    Bd, T, H, D = qb.shape
    hkv = kb.shape[2]
    g = H // hkv
    scale = D ** -0.5
    n_pages = page_table.shape[1]
    past = n_pages * PAGE_SIZE
    logf_past = cache_logf[page_table].reshape(Bd, past, H).astype(jnp.float32)
    r_past = lax.cumsum(logf_past, axis=1, reverse=True) - logf_past
    c_new = lax.cumsum(logf, axis=1)
    cq = c_new.reshape(Bd, T, hkv, g).transpose(0, 2, 3, 1)
    q = qb.reshape(Bd, T, hkv, g, D)
    logits = jnp.einsum('btkgd,bskd->bkgts', q, kb, preferred_element_type=jnp.float32) * scale
    logits = logits + cq[..., None] - cq[..., None, :]
    logits = jnp.where(jnp.arange(T)[None, :] <= jnp.arange(T)[:, None], logits, -jnp.inf)
    m = jnp.max(logits, axis=-1)
    e = jnp.exp(logits - m[..., None])
    l = jnp.sum(e, axis=-1)
    acc = jnp.einsum('bkgts,bskd->bkgtd', e, vb.astype(jnp.float32))
    r_pages = r_past.reshape(Bd, n_pages, PAGE_SIZE, hkv, g).transpose(1, 0, 3, 4, 2)

    def page_step(carry, xs):
        m, l, acc = carry
        phys, r_pg = xs
        k_pg = cache_k[phys]
        v_pg = cache_v[phys]
        lg = jnp.einsum('btkgd,bskd->bkgts', q, k_pg, preferred_element_type=jnp.float32) * scale
        lg = lg + cq[..., None] + r_pg[..., None, :]
        m_new = jnp.maximum(m, jnp.max(lg, axis=-1))
        corr = jnp.exp(m - m_new)
        e = jnp.exp(lg - m_new[..., None])
        l = l * corr + jnp.sum(e, axis=-1)
        acc = acc * corr[..., None] + jnp.einsum('bkgts,bskd->bkgtd', e, v_pg.astype(jnp.float32))
        return (m_new, l, acc), None

    (m, l, acc), _ = lax.scan(page_step, (m, l, acc), (page_table.T, r_pages))
    o = (acc / l[..., None]).astype(qb.dtype)
    return o.transpose(0, 3, 1, 2, 4).reshape(Bd, T, H * D)


def merge_branches(x, o_a, o_b, ga, gb, w_branch_a, w_branch_b, w_out):
    merged = jax.nn.sigmoid(ga) * (o_a @ w_branch_a) + jax.nn.sigmoid(gb) * (o_b @ w_branch_b)
    return x + merged @ w_out


def conv_ffn(x, prev, ffn_norm, w_gate, w_up, w_down, conv_w, conv_b):
    h = rmsnorm(x, ffn_norm)
    S = h.shape[1]
    gp = jnp.concatenate([prev, h @ w_gate], axis=1)
    conv = conv_b
    for j in range(CONV_W):
        conv = conv + conv_w[j] * gp[:, j:j + S]
    out = (jax.nn.silu(conv) * (h @ w_up)) @ w_down
    return x + out, gp[:, S:]


def setup_inputs(seed: int = 0) -> dict:
    key = jax.random.key(seed)
    ks = jax.random.split(key, 24)
    n_pages = PAST_LEN // PAGE_SIZE
    n_used = DEC_BATCH * n_pages
    n_pool = n_used + n_used // 4
    f32 = jnp.float32
    nrm = lambda k, shape, s: jax.random.normal(k, shape, f32) * s
    return {
        'x_prompt': nrm(ks[0], (BATCH, SEQ, D_MODEL), 1.0),
        'x_sample': nrm(ks[1], (DEC_BATCH, DEC_SEQ, D_MODEL), 1.0),
        'cache_dsa_k': nrm(ks[2], (DEPTH, n_pool, PAGE_SIZE, DSA_KV_HEADS, HEAD_DIM), 1.0),
        'cache_dsa_v': nrm(ks[3], (DEPTH, n_pool, PAGE_SIZE, DSA_KV_HEADS, HEAD_DIM), 1.0),
        'cache_idx_k': nrm(ks[4], (DEPTH, n_pool, PAGE_SIZE, IDX_DIM), 1.0),
        'cache_fox_k': nrm(ks[5], (DEPTH, n_pool, PAGE_SIZE, FOX_KV_HEADS, HEAD_DIM), 1.0),
        'cache_fox_v': nrm(ks[6], (DEPTH, n_pool, PAGE_SIZE, FOX_KV_HEADS, HEAD_DIM), 1.0),
        'cache_fox_logf': jax.nn.log_sigmoid(
            jax.random.uniform(ks[7], (DEPTH, n_pool, PAGE_SIZE, FOX_HEADS), f32, 3.0, 6.0)
            + nrm(ks[8], (DEPTH, n_pool, PAGE_SIZE, FOX_HEADS), 1.0)),
        'state_ffn_conv': nrm(ks[9], (DEPTH, DEC_BATCH, CONV_W - 1, D_FF), 1.0),
        'page_table': jax.random.permutation(ks[10], n_pool)[:n_used].reshape(DEC_BATCH, n_pages).astype(jnp.int32),
        'attn_norm': 1.0 + nrm(ks[11], (DEPTH, D_MODEL), 0.02),
        'w_in': nrm(ks[12], (DEPTH, D_MODEL, IN_COLS), D_MODEL ** -0.5),
        'b_forget': jax.random.uniform(ks[13], (DEPTH, FOX_HEADS), f32, 3.0, 6.0),
        'w_branch_a': nrm(ks[14], (DEPTH, DSA_HEADS * HEAD_DIM, D_MODEL), (DSA_HEADS * HEAD_DIM) ** -0.5),
        'w_branch_b': nrm(ks[15], (DEPTH, FOX_HEADS * HEAD_DIM, D_MODEL), (FOX_HEADS * HEAD_DIM) ** -0.5),
        'w_out': nrm(ks[16], (DEPTH, D_MODEL, D_MODEL), D_MODEL ** -0.5),
        'ffn_norm': 1.0 + nrm(ks[17], (DEPTH, D_MODEL), 0.02),
        'w_gate': nrm(ks[18], (DEPTH, D_MODEL, D_FF), D_MODEL ** -0.5),
        'w_up': nrm(ks[19], (DEPTH, D_MODEL, D_FF), D_MODEL ** -0.5),
        'w_down': nrm(ks[20], (DEPTH, D_FF, D_MODEL), D_FF ** -0.5),
        'conv_w': nrm(ks[21], (DEPTH, CONV_W, D_FF), CONV_W ** -0.5),
        'conv_b': nrm(ks[22], (DEPTH, D_FF), 0.01),
        'final_norm': 1.0 + nrm(ks[23], (D_MODEL,), 0.02),
    }


def reference(x_prompt, x_sample, cache_dsa_k, cache_dsa_v, cache_idx_k, cache_fox_k, cache_fox_v,
              cache_fox_logf, state_ffn_conv, page_table, attn_norm, w_in, b_forget, w_branch_a,
              w_branch_b, w_out, ffn_norm, w_gate, w_up, w_down, conv_w, conv_b, final_norm):
    B, S, _ = x_prompt.shape
    past = page_table.shape[1] * PAGE_SIZE
    pos_p = jnp.arange(S, dtype=jnp.int32)
    pos_s = past + jnp.arange(x_sample.shape[1], dtype=jnp.int32)
    xp, xs = x_prompt, x_sample
    p_new, s_new = [], []
    for l in range(DEPTH):
        qa, ka, va, iq, ik, iw, qb, kb, vb, logf, ga, gb = project(rmsnorm(xp, attn_norm[l]), pos_p, w_in[l], b_forget[l])
        o_a = dsa_prompt(qa, ka, va, iq, ik, iw)
        o_b = fox_prompt(qb, kb, vb, logf)
        xp = merge_branches(xp, o_a, o_b, ga, gb, w_branch_a[l], w_branch_b[l], w_out[l])
        conv0 = jnp.zeros((B, CONV_W - 1, D_FF), xp.dtype)
        xp, conv_p = conv_ffn(xp, conv0, ffn_norm[l], w_gate[l], w_up[l], w_down[l], conv_w[l], conv_b[l])
        p_new.append((ka, va, ik, kb, vb, logf, conv_p))
        qa, ka, va, iq, ik, iw, qb, kb, vb, logf, ga, gb = project(rmsnorm(xs, attn_norm[l]), pos_s, w_in[l], b_forget[l])
        o_a = dsa_sample(qa, ka, va, iq, ik, iw, cache_dsa_k[l], cache_dsa_v[l], cache_idx_k[l], page_table)
        o_b = fox_sample(qb, kb, vb, logf, cache_fox_k[l], cache_fox_v[l], cache_fox_logf[l], page_table)
        xs = merge_branches(xs, o_a, o_b, ga, gb, w_branch_a[l], w_branch_b[l], w_out[l])
        xs, conv_s = conv_ffn(xs, state_ffn_conv[l], ffn_norm[l], w_gate[l], w_up[l], w_down[l], conv_w[l], conv_b[l])
        s_new.append((ka, va, ik, kb, vb, logf, conv_s))
    y_prompt = rmsnorm(xp, final_norm)
    y_sample = rmsnorm(xs, final_norm)
    P = [jnp.stack(z) for z in zip(*p_new)]
    Q = [jnp.stack(z) for z in zip(*s_new)]
    return (y_prompt, y_sample, P[0], P[1], P[2], P[3], P[4], P[5], P[6], Q[0], Q[1], Q[2], Q[3], Q[4], Q[5], Q[6])
```

```python
import functools

import numpy as np
import jax
import jax.numpy as jnp
from jax import lax
from jax.experimental import pallas as pl
from jax.experimental.pallas import tpu as pltpu

F32 = jnp.float32
BF16 = jnp.bfloat16
I32 = jnp.int32

HEAD_DIM = 128
N_HEADS = 8
N_KV = 4
GROUP = N_HEADS // N_KV
IDX_HEADS = 16
IDX_DIM = 64
TOPK_MAX = 256
ROPE_THETA = 500000.0
ROT_DIM = HEAD_DIM // 4
IDX_ROT_DIM = IDX_DIM // 4
PAGE = 128
Q_BLOCK = 128
CONV_W = 3
RMS_EPS = 1e-6
ATT_SCALE = HEAD_DIM ** -0.5
IDX_SCALE = (IDX_HEADS * IDX_DIM) ** -0.5

LANES = 128
SUBLANES = 8
NEG = -1e30
INT_MIN = -2 ** 31
VMEM_LIMIT = 56 * 1024 * 1024

IW_LANE = 0
LOGF_LANE = IDX_HEADS

NT_DIMS = (((1,), (1,)), ((), ()))


def _params(*sem):
    return pltpu.CompilerParams(dimension_semantics=sem, vmem_limit_bytes=VMEM_LIMIT)


def _nt(a, b):
    return lax.dot_general(a, b, NT_DIMS, preferred_element_type=F32)


def _rms(x, g):
    ms = jnp.mean(x * x, axis=-1, keepdims=True)
    return (x * lax.rsqrt(ms + RMS_EPS)) * g


class _Layout:
    def __init__(self, d_model):
        self.d = d_model
        off = 0
        for name, size in (("ga", d_model), ("gb", d_model), ("qa", 1024), ("iq", 1024), ("qb", 1024),
                           ("ka", 512), ("va", 512), ("kb", 512), ("vb", 512), ("ik", LANES), ("iwf", LANES)):
            assert off % size == 0, (name, off, size)
            setattr(self, name, off)
            off += size
        self.tn = 512
        self.nc = -(-off // self.tn) * self.tn

    def chunk_kinds(self):
        kinds = ["plain"] * (self.nc // LANES)
        for name, size, kind in (("qa", 1024, "rope128"), ("ka", 512, "rope128"), ("iq", 1024, "rope64"),
                                 ("ik", LANES, "rope64"), ("iwf", LANES, "iwf")):
            start = getattr(self, name) // LANES
            for c in range(size // LANES):
                kinds[start + c] = kind
        return kinds


def _permute_w_in(w_in, lay):
    d = lay.d
    sizes = (1024, 512, 512, 1024, IDX_DIM, IDX_HEADS, 1024, 512, 512, N_HEADS, d, d)
    names = ("qa", "ka", "va", "iq", "ik", "iw", "qb", "kb", "vb", "fl", "ga", "gb")
    offs = np.concatenate([[0], np.cumsum(sizes)])
    src = {n: w_in[:, int(offs[k]):int(offs[k + 1])] for k, n in enumerate(names)}
    z = lambda n: jnp.zeros((d, n), w_in.dtype)
    used = lay.iwf + LANES
    parts = [src["ga"], src["gb"], src["qa"], src["iq"], src["qb"], src["ka"], src["va"], src["kb"], src["vb"],
             src["ik"], z(LANES - IDX_DIM), src["iw"], src["fl"], z(LANES - IDX_HEADS - N_HEADS), z(lay.nc - used)]
    return jnp.concatenate(parts, axis=1).astype(BF16)


def _rope_tables(pos):
    def one(rot_dim, period):
        half = rot_dim // 2
        inv_freq = jnp.power(ROPE_THETA, -jnp.arange(half, dtype=F32) * (2.0 / rot_dim))
        ang = pos.astype(F32)[:, None] * inv_freq[None, :]
        cos, sin = jnp.cos(ang), jnp.sin(ang)
        n = pos.shape[0]
        c = jnp.concatenate([cos, cos, jnp.ones((n, period - rot_dim), F32)], axis=1)
        sa = jnp.concatenate([-sin, jnp.zeros((n, period - half), F32)], axis=1)
        sb = jnp.concatenate([jnp.zeros((n, half), F32), sin, jnp.zeros((n, period - rot_dim), F32)], axis=1)
        rep = LANES // period
        return [jnp.tile(t, (1, rep)) for t in (c, sa, sb)]
    return jnp.concatenate(one(ROT_DIM, HEAD_DIM) + one(IDX_ROT_DIM, IDX_DIM), axis=1)


def _proj_body(x_ref, g_ref, w_ref, tab_ref, bf_ref, o_ref, h_ref, *, tile_kinds):
    j = pl.program_id(1)

    @pl.when(j == 0)
    def _():
        h_ref[...] = _rms(x_ref[...], g_ref[...]).astype(BF16)

    acc = jnp.dot(h_ref[...], w_ref[...], preferred_element_type=F32)

    def rope(a, base, half):
        c = tab_ref[:, base:base + LANES]
        sa = tab_ref[:, base + LANES:base + 2 * LANES]
        sb = tab_ref[:, base + 2 * LANES:base + 3 * LANES]
        return a * c + pltpu.roll(a, LANES - half, 1) * sa + pltpu.roll(a, half, 1) * sb

    def log_forget(a):
        z = a + bf_ref[...]
        ls = jnp.minimum(z, 0.0) - jnp.log1p(jnp.exp(-jnp.abs(z)))
        lane = lax.broadcasted_iota(I32, a.shape, 1)
        return jnp.where((lane >= LOGF_LANE) & (lane < LOGF_LANE + N_HEADS), ls, a)

    def emit(kinds):
        for c, kind in enumerate(kinds):
            a = acc[:, c * LANES:(c + 1) * LANES]
            if kind == "rope128":
                a = rope(a, 0, ROT_DIM // 2)
            elif kind == "rope64":
                a = rope(a, 3 * LANES, IDX_ROT_DIM // 2)
            elif kind == "iwf":
                a = log_forget(a)
            o_ref[:, c * LANES:(c + 1) * LANES] = a

    groups = {}
    for t, kinds in enumerate(tile_kinds):
        groups.setdefault(kinds, []).append(t)
    for kinds, tiles in groups.items():
        cond = functools.reduce(jnp.logical_or, [j == t for t in tiles])
        if all(k == "plain" for k in kinds):
            @pl.when(cond)
            def _():
                o_ref[...] = acc
        else:
            pl.when(cond)(functools.partial(emit, kinds))


def _norm_proj(x2d, gamma, w_perm, tab, bf_row, lay, tm):
    n, d = x2d.shape
    tn = lay.tn
    kinds = lay.chunk_kinds()
    per = tn // LANES
    tile_kinds = tuple(tuple(kinds[t * per:(t + 1) * per]) for t in range(lay.nc // tn))
    tab_blocks = tab.shape[0] // tm
    return pl.pallas_call(
        functools.partial(_proj_body, tile_kinds=tile_kinds),
        grid=(n // tm, lay.nc // tn),
        in_specs=[
            pl.BlockSpec((tm, d), lambda i, j: (i, 0)),
            pl.BlockSpec((1, d), lambda i, j: (0, 0)),
            pl.BlockSpec((d, tn), lambda i, j: (0, j)),
            pl.BlockSpec((tm, 6 * LANES), lambda i, j: (i % tab_blocks, 0)),
            pl.BlockSpec((1, LANES), lambda i, j: (0, 0)),
        ],
        out_specs=pl.BlockSpec((tm, tn), lambda i, j: (i, j)),
        out_shape=jax.ShapeDtypeStruct((n, lay.nc), F32),
        scratch_shapes=[pltpu.VMEM((tm, d), BF16)],
        compiler_params=_params("arbitrary", "arbitrary"),
        name="norm_proj",
    )(x2d, gamma, w_perm, tab, bf_row)


def _to_key(x):
    bits = pltpu.bitcast(x, I32)
    return jnp.where(bits < 0, bits ^ jnp.int32(0x7FFFFFFF), bits)


def _kth_largest_key(key_ref, k):
    rows = key_ref.shape[0]

    def body(it, res):
        cand = res + jnp.left_shift(jnp.int32(1), 31 - it)
        cnt = jnp.sum(jnp.where(key_ref[...] >= cand, 1.0, 0.0), axis=1, keepdims=True)
        return jnp.where(cnt >= k, cand, res)

    return lax.fori_loop(0, 32, body, jnp.full((rows, 1), INT_MIN, I32))


def _dsa_prompt_body(iq_ref, iwf_ref, ik_ref, q_ref, k_ref, v_ref, o_ref, sc_ref, key_ref, bias_ref, *, topk):
    i = pl.program_id(1)
    s_len = k_ref.shape[0]
    ikb = ik_ref[:, :IDX_DIM].astype(BF16)
    for h in range(IDX_HEADS):
        qh = iq_ref[:, h * IDX_DIM:(h + 1) * IDX_DIM].astype(BF16)
        s = jnp.maximum(_nt(qh, ikb), 0.0) * iwf_ref[:, IW_LANE + h:IW_LANE + h + 1]
        if h == 0:
            sc_ref[...] = s
        else:
            sc_ref[...] += s
    row = i * Q_BLOCK + lax.broadcasted_iota(I32, (Q_BLOCK, s_len), 0)
    col = lax.broadcasted_iota(I32, (Q_BLOCK, s_len), 1)
    causal = col <= row
    key_ref[...] = jnp.where(causal, _to_key(sc_ref[...] * IDX_SCALE), INT_MIN)
    thr = _kth_largest_key(key_ref, topk)
    bias_ref[...] = jnp.where((key_ref[...] >= thr) & causal, 0.0, NEG)
    for kh in range(N_KV):
        kk = k_ref[:, kh * HEAD_DIM:(kh + 1) * HEAD_DIM].astype(BF16)
        vv = v_ref[:, kh * HEAD_DIM:(kh + 1) * HEAD_DIM].astype(BF16)
        for g in range(GROUP):
            h = kh * GROUP + g
            q = q_ref[:, h * HEAD_DIM:(h + 1) * HEAD_DIM].astype(BF16)
            lg = _nt(q, kk) * ATT_SCALE + bias_ref[...]
            m = jnp.max(lg, axis=1, keepdims=True)
            e = jnp.exp(lg - m)
            l = jnp.sum(e, axis=1, keepdims=True)
            o = jnp.dot(e.astype(BF16), vv, preferred_element_type=F32)
            o_ref[:, h * HEAD_DIM:(h + 1) * HEAD_DIM] = o / l


def _dsa_prompt(proj, b, s, lay):
    nb = s // Q_BLOCK
    topk = min(TOPK_MAX, s // 4)
    row = lambda bb, i: bb * nb + i
    return pl.pallas_call(
        functools.partial(_dsa_prompt_body, topk=topk),
        grid=(b, nb),
        in_specs=[
            pl.BlockSpec((Q_BLOCK, 1024), lambda bb, i: (row(bb, i), lay.iq // 1024)),
            pl.BlockSpec((Q_BLOCK, LANES), lambda bb, i: (row(bb, i), lay.iwf // LANES)),
            pl.BlockSpec((s, LANES), lambda bb, i: (bb, lay.ik // LANES)),
            pl.BlockSpec((Q_BLOCK, 1024), lambda bb, i: (row(bb, i), lay.qa // 1024)),
            pl.BlockSpec((s, 512), lambda bb, i: (bb, lay.ka // 512)),
            pl.BlockSpec((s, 512), lambda bb, i: (bb, lay.va // 512)),
        ],
        out_specs=pl.BlockSpec((Q_BLOCK, 1024), lambda bb, i: (row(bb, i), 0)),
        out_shape=jax.ShapeDtypeStruct((b * s, 1024), F32),
        scratch_shapes=[pltpu.VMEM((Q_BLOCK, s), F32), pltpu.VMEM((Q_BLOCK, s), I32), pltpu.VMEM((Q_BLOCK, s), F32)],
        compiler_params=_params("arbitrary", "arbitrary"),
        name="dsa_prompt",
    )(proj, proj, proj, proj, proj, proj)


CUM_BLOCK = 256


def _fox_prompt_body(q_ref, k_ref, v_ref, lf_ref, o_ref, c_ref, ct_ref, bias_ref):
    i = pl.program_id(1)
    s_len = k_ref.shape[0]

    @pl.when(i == 0)
    def _():
        r = lax.broadcasted_iota(I32, (CUM_BLOCK, CUM_BLOCK), 0)
        c = lax.broadcasted_iota(I32, (CUM_BLOCK, CUM_BLOCK), 1)
        tri = jnp.where(c <= r, 1.0, 0.0).astype(F32)
        carry = jnp.zeros((1, LANES), F32)
        for blk in range(s_len // CUM_BLOCK):
            xb = lf_ref[blk * CUM_BLOCK:(blk + 1) * CUM_BLOCK, :]
            cb = jnp.dot(tri, xb, precision=lax.Precision.HIGHEST, preferred_element_type=F32) + carry
            c_ref[blk * CUM_BLOCK:(blk + 1) * CUM_BLOCK, :] = cb
            carry = cb[CUM_BLOCK - 1:CUM_BLOCK, :]
        ct_ref[...] = c_ref[...].T

    row = i * Q_BLOCK + lax.broadcasted_iota(I32, (Q_BLOCK, s_len), 0)
    col = lax.broadcasted_iota(I32, (Q_BLOCK, s_len), 1)
    bias_ref[...] = jnp.where(col <= row, 0.0, NEG)
    start = pl.multiple_of(i * Q_BLOCK, Q_BLOCK)
    for kh in range(N_KV):
        kk = k_ref[:, kh * HEAD_DIM:(kh + 1) * HEAD_DIM].astype(BF16)
        vv = v_ref[:, kh * HEAD_DIM:(kh + 1) * HEAD_DIM].astype(BF16)
        for g in range(GROUP):
            h = kh * GROUP + g
            q = q_ref[:, h * HEAD_DIM:(h + 1) * HEAD_DIM].astype(BF16)
            cq = c_ref[pl.ds(start, Q_BLOCK), LOGF_LANE + h:LOGF_LANE + h + 1]
            ck = ct_ref[LOGF_LANE + h:LOGF_LANE + h + 1, :]
            lg = _nt(q, kk) * ATT_SCALE + (cq - ck) + bias_ref[...]
            m = jnp.max(lg, axis=1, keepdims=True)
            e = jnp.exp(lg - m)
            l = jnp.sum(e, axis=1, keepdims=True)
            o = jnp.dot(e.astype(BF16), vv, preferred_element_type=F32)
            o_ref[:, h * HEAD_DIM:(h + 1) * HEAD_DIM] = o / l


def _fox_prompt(proj, b, s, lay):
    nb = s // Q_BLOCK
    assert s % CUM_BLOCK == 0
    row = lambda bb, i: bb * nb + i
    return pl.pallas_call(
        _fox_prompt_body,
        grid=(b, nb),
        in_specs=[
            pl.BlockSpec((Q_BLOCK, 1024), lambda bb, i: (row(bb, i), lay.qb // 1024)),
            pl.BlockSpec((s, 512), lambda bb, i: (bb, lay.kb // 512)),
            pl.BlockSpec((s, 512), lambda bb, i: (bb, lay.vb // 512)),
            pl.BlockSpec((s, LANES), lambda bb, i: (bb, lay.iwf // LANES)),
        ],
        out_specs=pl.BlockSpec((Q_BLOCK, 1024), lambda bb, i: (row(bb, i), 0)),
        out_shape=jax.ShapeDtypeStruct((b * s, 1024), F32),
        scratch_shapes=[pltpu.VMEM((s, LANES), F32), pltpu.VMEM((LANES, s), F32), pltpu.VMEM((Q_BLOCK, s), F32)],
        compiler_params=_params("arbitrary", "arbitrary"),
        name="fox_prompt",
    )(proj, proj, proj, proj)


def _pages_per_step(n_pages):
    for pp in (8, 4, 2, 1):
        if n_pages % pp == 0:
            return pp


def _page_specs(block, pp, page_of):
    def spec(r):
        return pl.BlockSpec((None,) + block, lambda bb, c, pt: (pt[bb, page_of(c, r)],) + (0,) * len(block))
    return [spec(r) for r in range(pp)]


def _softmax_update(kh, lg, sel, vv, m_ref, l_ref, acc_ref):
    if sel is not None:
        lg = jnp.where(sel, lg, NEG)
    m_old = m_ref[kh]
    m_new = jnp.maximum(m_old, jnp.max(lg, axis=1, keepdims=True))
    corr = jnp.exp(m_old - m_new)
    e = jnp.exp(lg - m_new)
    if sel is not None:
        e = jnp.where(sel, e, 0.0)
    l_ref[kh] = l_ref[kh] * corr + jnp.sum(e, axis=1, keepdims=True)
    acc_ref[kh] = acc_ref[kh] * corr + jnp.dot(e.astype(BF16), vv, preferred_element_type=F32)
    m_ref[kh] = m_new


def _softmax_init(m_ref, l_ref, acc_ref):
    m_ref[...] = jnp.full(m_ref.shape, NEG, F32)
    l_ref[...] = jnp.zeros(l_ref.shape, F32)
    acc_ref[...] = jnp.zeros(acc_ref.shape, F32)


def _kv_head_rows(page_ref, kh):
    return page_ref[pl.ds(kh, PAGE, stride=N_KV), :].astype(BF16)


def _group_rows(x, kh):
    t = SUBLANES // GROUP
    return jnp.concatenate([jnp.broadcast_to(x[kh * GROUP + g:kh * GROUP + g + 1], (t, x.shape[1]))
                            for g in range(GROUP)], axis=0)


def _dsa_sample_keys_body(pt_ref, iq_ref, iw_ref, ikn_ref, *rest, pp, topk, n_new):
    pages = rest[:pp]
    kp_ref, kn_ref, thr_ref, key_ref = rest[pp:]
    c = pl.program_id(1)
    past = key_ref.shape[1] - LANES
    q = iq_ref[...].astype(BF16)
    w = iw_ref[...]

    def keys_of(ikt):
        s = jnp.maximum(jnp.dot(q, ikt.astype(BF16), preferred_element_type=F32), 0.0) * w
        acc = s[0:SUBLANES]
        for h in range(1, IDX_HEADS):
            acc = acc + s[h * SUBLANES:(h + 1) * SUBLANES]
        return _to_key(acc * IDX_SCALE)

    for r in range(pp):
        kk = keys_of(pages[r][...])
        kp_ref[:, r * PAGE:(r + 1) * PAGE] = kk
        key_ref[:, pl.ds(pl.multiple_of((c * pp + r) * PAGE, PAGE), PAGE)] = kk

    @pl.when(c == 0)
    def _():
        row = lax.broadcasted_iota(I32, (SUBLANES, LANES), 0)
        lane = lax.broadcasted_iota(I32, (SUBLANES, LANES), 1)
        kk = jnp.where(lane <= row % n_new, keys_of(ikn_ref[...]), INT_MIN)
        kn_ref[...] = kk
        key_ref[:, past:past + LANES] = kk

    @pl.when(c == pl.num_programs(1) - 1)
    def _():
        thr_ref[...] = jnp.broadcast_to(_kth_largest_key(key_ref, topk), thr_ref.shape)


def _dsa_sample_keys(page_table, iq2, iw2, ikt_new, ik_cache_t, n_new):
    bd, n_pages = page_table.shape
    pp = _pages_per_step(n_pages)
    past = n_pages * PAGE
    topk = min(TOPK_MAX, (past + n_new) // 4)
    per_b = lambda shape: pl.BlockSpec((None,) + shape, lambda bb, c, pt: (bb,) + (0,) * len(shape))
    return pl.pallas_call(
        functools.partial(_dsa_sample_keys_body, pp=pp, topk=topk, n_new=n_new),
        grid_spec=pltpu.PrefetchScalarGridSpec(
            num_scalar_prefetch=1,
            grid=(bd, n_pages // pp),
            in_specs=[per_b((IDX_HEADS * SUBLANES, IDX_DIM)), per_b((IDX_HEADS * SUBLANES, 1)), per_b((IDX_DIM, LANES))]
            + _page_specs((IDX_DIM, PAGE), pp, lambda c, r: c * pp + r),
            out_specs=[
                pl.BlockSpec((None, SUBLANES, pp * PAGE), lambda bb, c, pt: (bb, 0, c)),
                per_b((SUBLANES, LANES)),
                per_b((SUBLANES, LANES)),
            ],
            scratch_shapes=[pltpu.VMEM((SUBLANES, past + LANES), I32)],
        ),
        out_shape=[jax.ShapeDtypeStruct((bd, SUBLANES, past), I32),
                   jax.ShapeDtypeStruct((bd, SUBLANES, LANES), I32),
                   jax.ShapeDtypeStruct((bd, SUBLANES, LANES), I32)],
        compiler_params=_params("arbitrary", "arbitrary"),
        name="dsa_sample_keys",
    )(page_table, iq2, iw2, ikt_new, *([ik_cache_t] * pp))


def _dsa_sample_attend_body(pt_ref, q_ref, kp_ref, kn_ref, thr_ref, knew_ref, vnew_ref, *rest, pp):
    k_pages, v_pages = rest[:pp], rest[pp:2 * pp]
    o_ref, m_ref, l_ref, acc_ref = rest[2 * pp:]
    c = pl.program_id(1)
    thr = thr_ref[:, 0:1]

    def selected(keys):
        return (keys >= thr) & (keys > INT_MIN)

    @pl.when(c == 0)
    def _():
        _softmax_init(m_ref, l_ref, acc_ref)
        sel = selected(kn_ref[...])
        for kh in range(N_KV):
            lg = _nt(q_ref[kh].astype(BF16), knew_ref[kh].astype(BF16)) * ATT_SCALE
            _softmax_update(kh, lg, sel, vnew_ref[kh].astype(BF16), m_ref, l_ref, acc_ref)

    for r in range(pp):
        sel = selected(kp_ref[:, r * PAGE:(r + 1) * PAGE])
        for kh in range(N_KV):
            lg = _nt(q_ref[kh].astype(BF16), _kv_head_rows(k_pages[r], kh)) * ATT_SCALE
            _softmax_update(kh, lg, sel, _kv_head_rows(v_pages[r], kh), m_ref, l_ref, acc_ref)

    @pl.when(c == pl.num_programs(1) - 1)
    def _():
        o_ref[...] = acc_ref[...] / l_ref[...]


def _softmax_scratch():
    return [pltpu.VMEM((N_KV, SUBLANES, 1), F32), pltpu.VMEM((N_KV, SUBLANES, 1), F32),
            pltpu.VMEM((N_KV, SUBLANES, HEAD_DIM), F32)]


def _dsa_sample_attend(page_table, q_s, keys_past, keys_new, thr, k_new, v_new, k_cache, v_cache):
    bd, n_pages = page_table.shape
    pp = _pages_per_step(n_pages)
    per_b = lambda shape: pl.BlockSpec((None,) + shape, lambda bb, c, pt: (bb,) + (0,) * len(shape))
    page_of = lambda c, r: c * pp + r
    return pl.pallas_call(
        functools.partial(_dsa_sample_attend_body, pp=pp),
        grid_spec=pltpu.PrefetchScalarGridSpec(
            num_scalar_prefetch=1,
            grid=(bd, n_pages // pp),
            in_specs=[per_b((N_KV, SUBLANES, HEAD_DIM)),
                      pl.BlockSpec((None, SUBLANES, pp * PAGE), lambda bb, c, pt: (bb, 0, c)),
                      per_b((SUBLANES, LANES)), per_b((SUBLANES, LANES)),
                      per_b((N_KV, LANES, HEAD_DIM)), per_b((N_KV, LANES, HEAD_DIM))]
            + _page_specs((PAGE * N_KV, HEAD_DIM), pp, page_of) + _page_specs((PAGE * N_KV, HEAD_DIM), pp, page_of),
            out_specs=per_b((N_KV, SUBLANES, HEAD_DIM)),
            scratch_shapes=_softmax_scratch(),
        ),
        out_shape=jax.ShapeDtypeStruct((bd, N_KV, SUBLANES, HEAD_DIM), F32),
        compiler_params=_params("arbitrary", "arbitrary"),
        name="dsa_sample_attend",
    )(page_table, q_s, keys_past, keys_new, thr, k_new, v_new, *([k_cache] * pp), *([v_cache] * pp))


def _fox_sample_body(pt_ref, q_ref, lfn_ref, knew_ref, vnew_ref, *rest, pp, n_new):
    lf_pages, k_pages, v_pages = rest[:pp], rest[pp:2 * pp], rest[2 * pp:3 * pp]
    o_ref, m_ref, l_ref, acc_ref, cq_ref, carry_ref = rest[3 * pp:]
    c = pl.program_id(1)
    r_io = lax.broadcasted_iota(I32, (LANES, LANES), 0)
    c_io = lax.broadcasted_iota(I32, (LANES, LANES), 1)
    row = lax.broadcasted_iota(I32, (SUBLANES, LANES), 0)
    lane = lax.broadcasted_iota(I32, (SUBLANES, LANES), 1)
    hi = lax.Precision.HIGHEST

    @pl.when(c == 0)
    def _():
        _softmax_init(m_ref, l_ref, acc_ref)
        carry_ref[...] = jnp.zeros(carry_ref.shape, F32)
        incl = jnp.where(r_io <= c_io, 1.0, 0.0).astype(F32)
        cum = jnp.dot(lfn_ref[...], incl, precision=hi, preferred_element_type=F32)
        for kh in range(N_KV):
            cg = _group_rows(cum, kh)
            cq = jnp.sum(jnp.where(lane == row % n_new, cg, 0.0), axis=1, keepdims=True)
            cq_ref[kh] = cq
            lg = _nt(q_ref[kh].astype(BF16), knew_ref[kh].astype(BF16)) * ATT_SCALE + (cq - cg)
            _softmax_update(kh, lg, lane <= row % n_new, vnew_ref[kh].astype(BF16), m_ref, l_ref, acc_ref)

    later = jnp.where(r_io > c_io, 1.0, 0.0).astype(F32)
    for r in range(pp):
        lf = lf_pages[r][...]
        suffix = jnp.dot(lf, later, precision=hi, preferred_element_type=F32) + carry_ref[...]
        carry_ref[...] += jnp.sum(lf, axis=1, keepdims=True)
        for kh in range(N_KV):
            lg = (_nt(q_ref[kh].astype(BF16), _kv_head_rows(k_pages[r], kh)) * ATT_SCALE
                  + (cq_ref[kh] + _group_rows(suffix, kh)))
            _softmax_update(kh, lg, None, _kv_head_rows(v_pages[r], kh), m_ref, l_ref, acc_ref)

    @pl.when(c == pl.num_programs(1) - 1)
    def _():
        o_ref[...] = acc_ref[...] / l_ref[...]


def _fox_sample(page_table, q_s, lft_new, k_new, v_new, lf_cache_t, k_cache, v_cache, n_new):
    bd, n_pages = page_table.shape
    pp = _pages_per_step(n_pages)
    per_b = lambda shape: pl.BlockSpec((None,) + shape, lambda bb, c, pt: (bb,) + (0,) * len(shape))
    page_of = lambda c, r: n_pages - 1 - (c * pp + r)
    kv_block = (PAGE * N_KV, HEAD_DIM)
    return pl.pallas_call(
        functools.partial(_fox_sample_body, pp=pp, n_new=n_new),
        grid_spec=pltpu.PrefetchScalarGridSpec(
            num_scalar_prefetch=1,
            grid=(bd, n_pages // pp),
            in_specs=[per_b((N_KV, SUBLANES, HEAD_DIM)), per_b((N_HEADS, LANES)),
                      per_b((N_KV, LANES, HEAD_DIM)), per_b((N_KV, LANES, HEAD_DIM))]
            + _page_specs((N_HEADS, PAGE), pp, page_of) + _page_specs(kv_block, pp, page_of)
            + _page_specs(kv_block, pp, page_of),
            out_specs=per_b((N_KV, SUBLANES, HEAD_DIM)),
            scratch_shapes=_softmax_scratch() + [pltpu.VMEM((N_KV, SUBLANES, 1), F32), pltpu.VMEM((N_HEADS, 1), F32)],
        ),
        out_shape=jax.ShapeDtypeStruct((bd, N_KV, SUBLANES, HEAD_DIM), F32),
        compiler_params=_params("arbitrary", "arbitrary"),
        name="fox_sample",
    )(page_table, q_s, lft_new, k_new, v_new, *([lf_cache_t] * pp), *([k_cache] * pp), *([v_cache] * pp))


def _merge_body(x_ref, oa_ref, ob_ref, ga_ref, gb_ref, wa_ref, wb_ref, wo_ref, gn_ref, x1_ref, h2_ref):
    a = jnp.dot(oa_ref[...].astype(BF16), wa_ref[...], preferred_element_type=F32)
    b = jnp.dot(ob_ref[...].astype(BF16), wb_ref[...], preferred_element_type=F32)
    merged = jax.nn.sigmoid(ga_ref[...]) * a + jax.nn.sigmoid(gb_ref[...]) * b
    x1 = x_ref[...] + jnp.dot(merged.astype(BF16), wo_ref[...], preferred_element_type=F32)
    x1_ref[...] = x1
    h2_ref[...] = _rms(x1, gn_ref[...]).astype(BF16)


def _merge(x2d, o_a, o_b, proj, wa, wb, wo, ffn_norm, lay, tm):
    n, d = x2d.shape
    const = lambda shape: pl.BlockSpec(shape, lambda i: (0, 0), pipeline_mode=pl.Buffered(1))
    return pl.pallas_call(
        _merge_body,
        grid=(n // tm,),
        in_specs=[
            pl.BlockSpec((tm, d), lambda i: (i, 0)),
            pl.BlockSpec((tm, 1024), lambda i: (i, 0)),
            pl.BlockSpec((tm, 1024), lambda i: (i, 0)),
            pl.BlockSpec((tm, d), lambda i: (i, lay.ga // d)),
            pl.BlockSpec((tm, d), lambda i: (i, lay.gb // d)),
            const((1024, d)), const((1024, d)), const((d, d)), const((1, d)),
        ],
        out_specs=[pl.BlockSpec((tm, d), lambda i: (i, 0)), pl.BlockSpec((tm, d), lambda i: (i, 0))],
        out_shape=[jax.ShapeDtypeStruct((n, d), F32), jax.ShapeDtypeStruct((n, d), BF16)],
        compiler_params=_params("arbitrary"),
        name="merge",
    )(x2d, o_a, o_b, proj, proj, wa, wb, wo, ffn_norm)


def _ffn_tail(f, n_f, contrib, x1_ref, fn_ref, y_ref, acc_ref):
    @pl.when(f == 0)
    def _():
        acc_ref[...] = contrib

    @pl.when(f > 0)
    def _():
        acc_ref[...] += contrib

    @pl.when(f == n_f - 1)
    def _():
        y_ref[...] = _rms(x1_ref[...] + acc_ref[...], fn_ref[...])


def _ffn_prompt_body(h_ref, halo_ref, wg_ref, wu_ref, wd_ref, cw_ref, cb_ref, x1_ref, fn_ref,
                     y_ref, tail_ref, acc_ref, ext_ref, *, tiles_per_seq):
    i = pl.program_id(0)
    f = pl.program_id(1)
    tm = h_ref.shape[0]
    h = h_ref[...]
    gp = jnp.dot(h, wg_ref[...], preferred_element_type=F32)
    up = jnp.dot(h, wu_ref[...], preferred_element_type=F32)
    halo = jnp.dot(halo_ref[...], wg_ref[...], preferred_element_type=F32)
    ext_ref[0:SUBLANES, :] = jnp.where(i % tiles_per_seq == 0, 0.0, halo)
    ext_ref[SUBLANES:, :] = gp
    conv = cb_ref[...] + cw_ref[CONV_W - 1:CONV_W, :] * gp
    for j in range(CONV_W - 1):
        back = CONV_W - 1 - j
        conv = conv + cw_ref[j:j + 1, :] * ext_ref[SUBLANES - back:SUBLANES - back + tm, :]
    act = (conv * jax.nn.sigmoid(conv)) * up
    tail_ref[...] = gp[tm - SUBLANES:, :]
    contrib = jnp.dot(act.astype(BF16), wd_ref[...], preferred_element_type=F32)
    _ffn_tail(f, pl.num_programs(1), contrib, x1_ref, fn_ref, y_ref, acc_ref)


def _ffn_prompt(h2, x1, wg, wu, wd, conv_w, conv_b, final_norm, s, tm, tf):
    n, d = x1.shape
    ff = wg.shape[1]
    assert s % tm == 0 and tm % SUBLANES == 0
    hb = tm // SUBLANES
    return pl.pallas_call(
        functools.partial(_ffn_prompt_body, tiles_per_seq=s // tm),
        grid=(n // tm, ff // tf),
        in_specs=[
            pl.BlockSpec((tm, d), lambda i, f: (i, 0)),
            pl.BlockSpec((SUBLANES, d), lambda i, f: (jnp.maximum(i * hb - 1, 0), 0)),
            pl.BlockSpec((d, tf), lambda i, f: (0, f)),
            pl.BlockSpec((d, tf), lambda i, f: (0, f)),
            pl.BlockSpec((tf, d), lambda i, f: (f, 0)),
            pl.BlockSpec((CONV_W, tf), lambda i, f: (0, f)),
            pl.BlockSpec((1, tf), lambda i, f: (0, f)),
            pl.BlockSpec((tm, d), lambda i, f: (i, 0)),
            pl.BlockSpec((1, d), lambda i, f: (0, 0)),
        ],
        out_specs=[pl.BlockSpec((tm, d), lambda i, f: (i, 0)),
                   pl.BlockSpec((SUBLANES, tf), lambda i, f: (i, f))],
        out_shape=[jax.ShapeDtypeStruct((n, d), F32), jax.ShapeDtypeStruct((n // tm * SUBLANES, ff), F32)],
        scratch_shapes=[pltpu.VMEM((tm, d), F32), pltpu.VMEM((tm + SUBLANES, tf), F32)],
        compiler_params=_params("arbitrary", "arbitrary"),
        name="ffn_prompt",
    )(h2, h2, wg, wu, wd, conv_w, conv_b, x1, final_norm)


def _ffn_sample_body(h_ref, st_ref, wg_ref, wu_ref, wd_ref, cw_ref, cb_ref, x1_ref, fn_ref,
                     y_ref, new_st_ref, acc_ref, ext_ref, *, bd):
    f = pl.program_id(0)
    n = h_ref.shape[0]
    h = h_ref[...]
    gp = jnp.dot(h, wg_ref[...], preferred_element_type=F32)
    up = jnp.dot(h, wu_ref[...], preferred_element_type=F32)
    keep = (CONV_W - 1) * bd
    ext_ref[0:keep, :] = st_ref[...]
    ext_ref[keep:, :] = gp
    conv = cb_ref[...]
    for j in range(CONV_W):
        conv = conv + cw_ref[j:j + 1, :] * ext_ref[j * bd:j * bd + n, :]
    act = (conv * jax.nn.sigmoid(conv)) * up
    new_st_ref[...] = gp[n - keep:, :]
    contrib = jnp.dot(act.astype(BF16), wd_ref[...], preferred_element_type=F32)
    _ffn_tail(f, pl.num_programs(0), contrib, x1_ref, fn_ref, y_ref, acc_ref)


def _ffn_sample(h2, x1, state, wg, wu, wd, conv_w, conv_b, final_norm, bd, tf):
    n, d = x1.shape
    ff = wg.shape[1]
    keep = (CONV_W - 1) * bd
    assert bd % SUBLANES == 0 and n >= keep
    return pl.pallas_call(
        functools.partial(_ffn_sample_body, bd=bd),
        grid=(ff // tf,),
        in_specs=[
            pl.BlockSpec((n, d), lambda f: (0, 0)),
            pl.BlockSpec((keep, tf), lambda f: (0, f)),
            pl.BlockSpec((d, tf), lambda f: (0, f)),
            pl.BlockSpec((d, tf), lambda f: (0, f)),
            pl.BlockSpec((tf, d), lambda f: (f, 0)),
            pl.BlockSpec((CONV_W, tf), lambda f: (0, f)),
            pl.BlockSpec((1, tf), lambda f: (0, f)),
            pl.BlockSpec((n, d), lambda f: (0, 0)),
            pl.BlockSpec((1, d), lambda f: (0, 0)),
        ],
        out_specs=[pl.BlockSpec((n, d), lambda f: (0, 0)), pl.BlockSpec((keep, tf), lambda f: (0, f))],
        out_shape=[jax.ShapeDtypeStruct((n, d), F32), jax.ShapeDtypeStruct((keep, ff), F32)],
        scratch_shapes=[pltpu.VMEM((n, d), F32), pltpu.VMEM((keep + n, tf), F32)],
        compiler_params=_params("arbitrary"),
        name="ffn_sample",
    )(h2, state, wg, wu, wd, conv_w, conv_b, x1, final_norm)


def _largest_divisor(n, candidates):
    for c in candidates:
        if n % c == 0:
            return c
    raise ValueError(f"no tile for {n} among {candidates}")


def kernel(x_prompt, x_sample, cache_dsa_k, cache_dsa_v, cache_idx_k, cache_fox_k, cache_fox_v, cache_fox_logf, state_ffn_conv, page_table, attn_norm, w_in, b_forget, w_branch_a, w_branch_b, w_out, ffn_norm, w_gate, w_up, w_down, conv_w, conv_b, final_norm):
    b, s, d = x_prompt.shape
    bd, t_new, _ = x_sample.shape
    depth = attn_norm.shape[0]
    assert depth == 1 and t_new * GROUP == SUBLANES and s % Q_BLOCK == 0
    n_pages = page_table.shape[1]
    past = n_pages * PAGE
    n_pool = cache_dsa_k.shape[1]
    ff = w_gate.shape[2]
    lay = _Layout(d)

    w_perm = _permute_w_in(w_in[0], lay)
    bf_row = jnp.zeros((1, LANES), F32).at[0, LOGF_LANE:LOGF_LANE + N_HEADS].set(b_forget[0])
    wa, wb, wo = (w[0].astype(BF16) for w in (w_branch_a, w_branch_b, w_out))
    wg, wu, wd = (w[0].astype(BF16) for w in (w_gate, w_up, w_down))
    g_attn, g_ffn, g_fin = attn_norm[0][None, :], ffn_norm[0][None, :], final_norm[None, :]
    cw, cb = conv_w[0], conv_b[0][None, :]

    tm_p = _largest_divisor(s, (1024, 512, 256, 128))
    tab_p = _rope_tables(jnp.arange(s, dtype=I32))
    tab_s = jnp.tile(_rope_tables(past + jnp.arange(t_new, dtype=I32)), (bd, 1))
    xp2 = x_prompt.reshape(b * s, d)
    xs2 = x_sample.reshape(bd * t_new, d)
    proj_p = _norm_proj(xp2, g_attn, w_perm, tab_p, bf_row, lay, tm_p)
    proj_s = _norm_proj(xs2, g_attn, w_perm, tab_s, bf_row, lay, bd * t_new)

    oa_p = _dsa_prompt(proj_p, b, s, lay)
    ob_p = _fox_prompt(proj_p, b, s, lay)

    def cols(name, width):
        o = getattr(lay, name)
        return proj_s[:, o:o + width].reshape(bd, t_new, width)

    def heads_major(x):
        x = x.reshape(bd, t_new, N_KV, GROUP, HEAD_DIM).transpose(0, 2, 3, 1, 4)
        return x.reshape(bd, N_KV, GROUP * t_new, HEAD_DIM)

    def new_kv(x):
        x = x.reshape(bd, t_new, N_KV, HEAD_DIM).transpose(0, 2, 1, 3)
        return jnp.pad(x, ((0, 0), (0, 0), (0, LANES - t_new), (0, 0)))

    def heads_back(o):
        o = o.reshape(bd, N_KV, GROUP, t_new, HEAD_DIM).transpose(0, 3, 1, 2, 4)
        return o.reshape(bd * t_new, N_HEADS * HEAD_DIM)

    ka_s, va_s, kb_s, vb_s = cols("ka", 512), cols("va", 512), cols("kb", 512), cols("vb", 512)
    ik_s = cols("ik", IDX_DIM)
    iwf_s = cols("iwf", LANES)
    logf_s = iwf_s[..., LOGF_LANE:LOGF_LANE + N_HEADS]
    iq_s = cols("iq", 1024).reshape(bd, t_new, IDX_HEADS, IDX_DIM).transpose(0, 2, 1, 3)
    iq2 = jnp.broadcast_to(iq_s[:, :, None], (bd, IDX_HEADS, GROUP, t_new, IDX_DIM)).reshape(bd, IDX_HEADS * SUBLANES, IDX_DIM)
    iw_s = iwf_s[..., IW_LANE:IW_LANE + IDX_HEADS].transpose(0, 2, 1)
    iw2 = jnp.broadcast_to(iw_s[:, :, None], (bd, IDX_HEADS, GROUP, t_new)).reshape(bd, IDX_HEADS * SUBLANES, 1)
    ikt_new = jnp.pad(ik_s.transpose(0, 2, 1), ((0, 0), (0, 0), (0, LANES - t_new)))
    lft_new = jnp.pad(logf_s.transpose(0, 2, 1), ((0, 0), (0, 0), (0, LANES - t_new)))

    ik_cache_t = jnp.swapaxes(cache_idx_k[0], 1, 2)
    lf_cache_t = jnp.swapaxes(cache_fox_logf[0], 1, 2)
    kv_rows = lambda c: c.reshape(n_pool, PAGE * N_KV, HEAD_DIM)

    keys_past, keys_new, thr = _dsa_sample_keys(page_table, iq2, iw2, ikt_new, ik_cache_t, t_new)
    oa_s = heads_back(_dsa_sample_attend(page_table, heads_major(cols("qa", 1024)), keys_past, keys_new, thr,
                                         new_kv(ka_s), new_kv(va_s), kv_rows(cache_dsa_k), kv_rows(cache_dsa_v)))
    ob_s = heads_back(_fox_sample(page_table, heads_major(cols("qb", 1024)), lft_new, new_kv(kb_s), new_kv(vb_s),
                                  lf_cache_t, kv_rows(cache_fox_k), kv_rows(cache_fox_v), t_new))

    tm_m = _largest_divisor(s, (256, 128))
    x1_p, h2_p = _merge(xp2, oa_p, ob_p, proj_p, wa, wb, wo, g_ffn, lay, tm_m)
    x1_s, h2_s = _merge(xs2, oa_s, ob_s, proj_s, wa, wb, wo, g_ffn, lay, bd * t_new)

    tf = _largest_divisor(ff, (512, 256, 128))
    tm_f = _largest_divisor(s, (512, 256, 128))
    y_p, tails = _ffn_prompt(h2_p, x1_p, wg, wu, wd, cw, cb, g_fin, s, tm_f, tf)
    conv_p = tails.reshape(b, s // tm_f, SUBLANES, ff)[:, -1, SUBLANES - (CONV_W - 1):, :]

    t_major = lambda x: x.reshape(bd, t_new, -1).transpose(1, 0, 2).reshape(t_new * bd, -1)
    state_t = state_ffn_conv[0].transpose(1, 0, 2).reshape((CONV_W - 1) * bd, ff)
    y_s_t, st_t = _ffn_sample(t_major(h2_s), t_major(x1_s), state_t, wg, wu, wd, cw, cb, g_fin, bd, tf)
    y_s = y_s_t.reshape(t_new, bd, d).transpose(1, 0, 2)
    conv_s = st_t.reshape(CONV_W - 1, bd, ff).transpose(1, 0, 2)

    def pcols(name, width):
        o = getattr(lay, name)
        return proj_p[:, o:o + width]

    p_out = (pcols("ka", 512).reshape(1, b, s, N_KV, HEAD_DIM), pcols("va", 512).reshape(1, b, s, N_KV, HEAD_DIM),
             pcols("ik", IDX_DIM).reshape(1, b, s, IDX_DIM),
             pcols("kb", 512).reshape(1, b, s, N_KV, HEAD_DIM), pcols("vb", 512).reshape(1, b, s, N_KV, HEAD_DIM),
             proj_p[:, lay.iwf + LOGF_LANE:lay.iwf + LOGF_LANE + N_HEADS].reshape(1, b, s, N_HEADS),
             conv_p[None])
    s_out = (ka_s.reshape(1, bd, t_new, N_KV, HEAD_DIM), va_s.reshape(1, bd, t_new, N_KV, HEAD_DIM), ik_s[None],
             kb_s.reshape(1, bd, t_new, N_KV, HEAD_DIM), vb_s.reshape(1, bd, t_new, N_KV, HEAD_DIM), logf_s[None],
             conv_s[None])
    return (y_p.reshape(b, s, d), y_s) + p_out + s_out
```

```python
import functools

import numpy as np
import jax
import jax.numpy as jnp
from jax import lax
from jax.experimental import pallas as pl
from jax.experimental.pallas import tpu as pltpu

F32 = jnp.float32
BF16 = jnp.bfloat16
I32 = jnp.int32

HEAD_DIM = 128
N_HEADS = 8
N_KV = 4
GROUP = N_HEADS // N_KV
IDX_HEADS = 16
IDX_DIM = 64
TOPK_MAX = 256
ROPE_THETA = 500000.0
ROT_DIM = HEAD_DIM // 4
IDX_ROT_DIM = IDX_DIM // 4
PAGE = 128
Q_BLOCK = 128
CONV_W = 3
RMS_EPS = 1e-6
ATT_SCALE = HEAD_DIM ** -0.5
IDX_SCALE = (IDX_HEADS * IDX_DIM) ** -0.5

LANES = 128
SUBLANES = 8
NEG = -1e30
INT_MIN = -2 ** 31
VMEM_LIMIT = 56 * 1024 * 1024

IW_LANE = 0
LOGF_LANE = IDX_HEADS

NT_DIMS = (((1,), (1,)), ((), ()))


def _params(*sem):
    return pltpu.CompilerParams(dimension_semantics=sem, vmem_limit_bytes=VMEM_LIMIT)


def _nt(a, b):
    return lax.dot_general(a, b, NT_DIMS, preferred_element_type=F32)


def _rms(x, g):
    ms = jnp.mean(x * x, axis=-1, keepdims=True)
    return (x * lax.rsqrt(ms + RMS_EPS)) * g


class _Layout:
    def __init__(self, d_model):
        self.d = d_model
        off = 0
        for name, size in (("ga", d_model), ("gb", d_model), ("qa", 1024), ("iq", 1024), ("qb", 1024),
                           ("ka", 512), ("va", 512), ("kb", 512), ("vb", 512), ("ik", LANES), ("iwf", LANES)):
            assert off % size == 0, (name, off, size)
            setattr(self, name, off)
            off += size
        self.tn = 512
        self.nc = -(-off // self.tn) * self.tn

    def chunk_kinds(self):
        kinds = ["plain"] * (self.nc // LANES)
        for name, size, kind in (("qa", 1024, "rope128"), ("ka", 512, "rope128"), ("iq", 1024, "rope64"),
                                 ("ik", LANES, "rope64"), ("iwf", LANES, "iwf")):
            start = getattr(self, name) // LANES
            for c in range(size // LANES):
                kinds[start + c] = kind
        return kinds


def _permute_w_in(w_in, lay):
    d = lay.d
    sizes = (1024, 512, 512, 1024, IDX_DIM, IDX_HEADS, 1024, 512, 512, N_HEADS, d, d)
    names = ("qa", "ka", "va", "iq", "ik", "iw", "qb", "kb", "vb", "fl", "ga", "gb")
    offs = np.concatenate([[0], np.cumsum(sizes)])
    src = {n: w_in[:, int(offs[k]):int(offs[k + 1])] for k, n in enumerate(names)}
    z = lambda n: jnp.zeros((d, n), w_in.dtype)
    used = lay.iwf + LANES
    parts = [src["ga"], src["gb"], src["qa"], src["iq"], src["qb"], src["ka"], src["va"], src["kb"], src["vb"],
             src["ik"], z(LANES - IDX_DIM), src["iw"], src["fl"], z(LANES - IDX_HEADS - N_HEADS), z(lay.nc - used)]
    return jnp.concatenate(parts, axis=1).astype(BF16)


def _rope_tables(pos):
    def one(rot_dim, period):
        half = rot_dim // 2
        inv_freq = jnp.power(ROPE_THETA, -jnp.arange(half, dtype=F32) * (2.0 / rot_dim))
        ang = pos.astype(F32)[:, None] * inv_freq[None, :]
        cos, sin = jnp.cos(ang), jnp.sin(ang)
        n = pos.shape[0]
        c = jnp.concatenate([cos, cos, jnp.ones((n, period - rot_dim), F32)], axis=1)
        sa = jnp.concatenate([-sin, jnp.zeros((n, period - half), F32)], axis=1)
        sb = jnp.concatenate([jnp.zeros((n, half), F32), sin, jnp.zeros((n, period - rot_dim), F32)], axis=1)
        rep = LANES // period
        return [jnp.tile(t, (1, rep)) for t in (c, sa, sb)]
    return jnp.concatenate(one(ROT_DIM, HEAD_DIM) + one(IDX_ROT_DIM, IDX_DIM), axis=1)


def _proj_body(x_ref, g_ref, w_ref, tab_ref, bf_ref, o_ref, h_ref, *, tile_kinds):
    j = pl.program_id(1)

    @pl.when(j == 0)
    def _():
        h_ref[...] = _rms(x_ref[...], g_ref[...]).astype(BF16)

    acc = jnp.dot(h_ref[...], w_ref[...], preferred_element_type=F32)

    def rope(a, base, half):
        c = tab_ref[:, base:base + LANES]
        sa = tab_ref[:, base + LANES:base + 2 * LANES]
        sb = tab_ref[:, base + 2 * LANES:base + 3 * LANES]
        return a * c + pltpu.roll(a, LANES - half, 1) * sa + pltpu.roll(a, half, 1) * sb

    def log_forget(a):
        z = a + bf_ref[...]
        ls = jnp.minimum(z, 0.0) - jnp.log1p(jnp.exp(-jnp.abs(z)))
        lane = lax.broadcasted_iota(I32, a.shape, 1)
        return jnp.where((lane >= LOGF_LANE) & (lane < LOGF_LANE + N_HEADS), ls, a)

    def emit(kinds):
        for c, kind in enumerate(kinds):
            a = acc[:, c * LANES:(c + 1) * LANES]
            if kind == "rope128":
                a = rope(a, 0, ROT_DIM // 2)
            elif kind == "rope64":
                a = rope(a, 3 * LANES, IDX_ROT_DIM // 2)
            elif kind == "iwf":
                a = log_forget(a)
            o_ref[:, c * LANES:(c + 1) * LANES] = a

    groups = {}
    for t, kinds in enumerate(tile_kinds):
        groups.setdefault(kinds, []).append(t)
    for kinds, tiles in groups.items():
        cond = functools.reduce(jnp.logical_or, [j == t for t in tiles])
        if all(k == "plain" for k in kinds):
            @pl.when(cond)
            def _():
                o_ref[...] = acc
        else:
            pl.when(cond)(functools.partial(emit, kinds))


def _norm_proj(x2d, gamma, w_perm, tab, bf_row, lay, tm):
    n, d = x2d.shape
    tn = lay.tn
    kinds = lay.chunk_kinds()
    per = tn // LANES
    tile_kinds = tuple(tuple(kinds[t * per:(t + 1) * per]) for t in range(lay.nc // tn))
    tab_blocks = tab.shape[0] // tm
    return pl.pallas_call(
        functools.partial(_proj_body, tile_kinds=tile_kinds),
        grid=(n // tm, lay.nc // tn),
        in_specs=[
            pl.BlockSpec((tm, d), lambda i, j: (i, 0)),
            pl.BlockSpec((1, d), lambda i, j: (0, 0)),
            pl.BlockSpec((d, tn), lambda i, j: (0, j)),
            pl.BlockSpec((tm, 6 * LANES), lambda i, j: (i % tab_blocks, 0)),
            pl.BlockSpec((1, LANES), lambda i, j: (0, 0)),
        ],
        out_specs=pl.BlockSpec((tm, tn), lambda i, j: (i, j)),
        out_shape=jax.ShapeDtypeStruct((n, lay.nc), F32),
        scratch_shapes=[pltpu.VMEM((tm, d), BF16)],
        compiler_params=_params("arbitrary", "arbitrary"),
        name="norm_proj",
    )(x2d, gamma, w_perm, tab, bf_row)


def _to_key(x):
    bits = pltpu.bitcast(x, I32)
    return jnp.where(bits < 0, bits ^ jnp.int32(0x7FFFFFFF), bits)


def _kth_largest_key(read_keys, rows, k):
    def body(it, res):
        cand = res + jnp.left_shift(jnp.int32(1), 31 - it)
        cnt = jnp.sum(jnp.where(read_keys() >= cand, 1.0, 0.0), axis=1, keepdims=True)
        return jnp.where(cnt >= k, cand, res)

    return lax.fori_loop(0, 32, body, jnp.full((rows, 1), INT_MIN, I32))


KEY_BUCKET = 512


def _key_limits(s):
    step = min(KEY_BUCKET, s)
    assert s % step == 0
    return tuple(range(step, s + 1, step))


def _for_causal_limit(i, limits, block_fn):
    q_end = (i + 1) * Q_BLOCK
    prev = 0
    for lim in limits:
        pl.when((q_end > prev) & (q_end <= lim))(functools.partial(block_fn, lim))
        prev = lim


def _attend_heads(q_ref, kb_ref, vb_ref, o_ref, n, logit_bias):
    for kh in range(N_KV):
        kk = kb_ref[0:n, kh * HEAD_DIM:(kh + 1) * HEAD_DIM]
        vv = vb_ref[0:n, kh * HEAD_DIM:(kh + 1) * HEAD_DIM]
        for g in range(GROUP):
            h = kh * GROUP + g
            q = q_ref[:, h * HEAD_DIM:(h + 1) * HEAD_DIM].astype(BF16)
            lg = _nt(q, kk) * ATT_SCALE + logit_bias(h)
            m = jnp.max(lg, axis=1, keepdims=True)
            e = jnp.exp(lg - m)
            l = jnp.sum(e, axis=1, keepdims=True)
            o = jnp.dot(e.astype(BF16), vv, preferred_element_type=F32)
            o_ref[:, h * HEAD_DIM:(h + 1) * HEAD_DIM] = o / l


def _dsa_prompt_body(iq_ref, iwf_ref, ik_ref, q_ref, k_ref, v_ref, o_ref,
                     ikb_ref, kb_ref, vb_ref, sc_ref, key_ref, bias_ref, *, topk, limits):
    i = pl.program_id(1)

    @pl.when(i == 0)
    def _():
        ikb_ref[...] = ik_ref[:, :IDX_DIM].astype(BF16)
        kb_ref[...] = k_ref[...].astype(BF16)
        vb_ref[...] = v_ref[...].astype(BF16)

    def block(n):
        ikb = ikb_ref[0:n, :]
        for h in range(IDX_HEADS):
            qh = iq_ref[:, h * IDX_DIM:(h + 1) * IDX_DIM].astype(BF16)
            s = jnp.maximum(_nt(qh, ikb), 0.0) * iwf_ref[:, IW_LANE + h:IW_LANE + h + 1]
            if h == 0:
                sc_ref[:, 0:n] = s
            else:
                sc_ref[:, 0:n] += s
        row = i * Q_BLOCK + lax.broadcasted_iota(I32, (Q_BLOCK, n), 0)
        col = lax.broadcasted_iota(I32, (Q_BLOCK, n), 1)
        causal = col <= row
        key_ref[:, 0:n] = jnp.where(causal, _to_key(sc_ref[:, 0:n] * IDX_SCALE), INT_MIN)
        thr = _kth_largest_key(lambda: key_ref[:, 0:n], Q_BLOCK, topk)
        bias_ref[:, 0:n] = jnp.where((key_ref[:, 0:n] >= thr) & causal, 0.0, NEG)
        _attend_heads(q_ref, kb_ref, vb_ref, o_ref, n, lambda h: bias_ref[:, 0:n])

    _for_causal_limit(i, limits, block)


def _dsa_prompt(proj, b, s, lay):
    nb = s // Q_BLOCK
    topk = min(TOPK_MAX, s // 4)
    row = lambda bb, i: bb * nb + i
    return pl.pallas_call(
        functools.partial(_dsa_prompt_body, topk=topk, limits=_key_limits(s)),
        grid=(b, nb),
        in_specs=[
            pl.BlockSpec((Q_BLOCK, 1024), lambda bb, i: (row(bb, i), lay.iq // 1024)),
            pl.BlockSpec((Q_BLOCK, LANES), lambda bb, i: (row(bb, i), lay.iwf // LANES)),
            pl.BlockSpec((s, LANES), lambda bb, i: (bb, lay.ik // LANES)),
            pl.BlockSpec((Q_BLOCK, 1024), lambda bb, i: (row(bb, i), lay.qa // 1024)),
            pl.BlockSpec((s, 512), lambda bb, i: (bb, lay.ka // 512)),
            pl.BlockSpec((s, 512), lambda bb, i: (bb, lay.va // 512)),
        ],
        out_specs=pl.BlockSpec((Q_BLOCK, 1024), lambda bb, i: (row(bb, i), 0)),
        out_shape=jax.ShapeDtypeStruct((b * s, 1024), F32),
        scratch_shapes=[pltpu.VMEM((s, IDX_DIM), BF16), pltpu.VMEM((s, 512), BF16), pltpu.VMEM((s, 512), BF16),
                        pltpu.VMEM((Q_BLOCK, s), F32), pltpu.VMEM((Q_BLOCK, s), I32), pltpu.VMEM((Q_BLOCK, s), F32)],
        compiler_params=_params("arbitrary", "arbitrary"),
        name="dsa_prompt",
    )(proj, proj, proj, proj, proj, proj)


CUM_BLOCK = 256


def _fox_prompt_body(q_ref, k_ref, v_ref, lf_ref, o_ref, kb_ref, vb_ref, c_ref, ct_ref, bias_ref, *, limits):
    i = pl.program_id(1)
    s_len = k_ref.shape[0]

    @pl.when(i == 0)
    def _():
        kb_ref[...] = k_ref[...].astype(BF16)
        vb_ref[...] = v_ref[...].astype(BF16)
        r = lax.broadcasted_iota(I32, (CUM_BLOCK, CUM_BLOCK), 0)
        c = lax.broadcasted_iota(I32, (CUM_BLOCK, CUM_BLOCK), 1)
        tri = jnp.where(c <= r, 1.0, 0.0).astype(F32)
        carry = jnp.zeros((1, LANES), F32)
        for blk in range(s_len // CUM_BLOCK):
            xb = lf_ref[blk * CUM_BLOCK:(blk + 1) * CUM_BLOCK, :]
            cb = jnp.dot(tri, xb, precision=lax.Precision.HIGHEST, preferred_element_type=F32) + carry
            c_ref[blk * CUM_BLOCK:(blk + 1) * CUM_BLOCK, :] = cb
            carry = cb[CUM_BLOCK - 1:CUM_BLOCK, :]
        ct_ref[...] = c_ref[...].T

    start = pl.multiple_of(i * Q_BLOCK, Q_BLOCK)

    def block(n):
        row = i * Q_BLOCK + lax.broadcasted_iota(I32, (Q_BLOCK, n), 0)
        col = lax.broadcasted_iota(I32, (Q_BLOCK, n), 1)
        bias_ref[:, 0:n] = jnp.where(col <= row, 0.0, NEG)

        def logit_bias(h):
            cq = c_ref[pl.ds(start, Q_BLOCK), LOGF_LANE + h:LOGF_LANE + h + 1]
            ck = ct_ref[LOGF_LANE + h:LOGF_LANE + h + 1, 0:n]
            return (cq - ck) + bias_ref[:, 0:n]

        _attend_heads(q_ref, kb_ref, vb_ref, o_ref, n, logit_bias)

    _for_causal_limit(i, limits, block)


def _fox_prompt(proj, b, s, lay):
    nb = s // Q_BLOCK
    assert s % CUM_BLOCK == 0
    row = lambda bb, i: bb * nb + i
    return pl.pallas_call(
        functools.partial(_fox_prompt_body, limits=_key_limits(s)),
        grid=(b, nb),
        in_specs=[
            pl.BlockSpec((Q_BLOCK, 1024), lambda bb, i: (row(bb, i), lay.qb // 1024)),
            pl.BlockSpec((s, 512), lambda bb, i: (bb, lay.kb // 512)),
            pl.BlockSpec((s, 512), lambda bb, i: (bb, lay.vb // 512)),
            pl.BlockSpec((s, LANES), lambda bb, i: (bb, lay.iwf // LANES)),
        ],
        out_specs=pl.BlockSpec((Q_BLOCK, 1024), lambda bb, i: (row(bb, i), 0)),
        out_shape=jax.ShapeDtypeStruct((b * s, 1024), F32),
        scratch_shapes=[pltpu.VMEM((s, 512), BF16), pltpu.VMEM((s, 512), BF16),
                        pltpu.VMEM((s, LANES), F32), pltpu.VMEM((LANES, s), F32), pltpu.VMEM((Q_BLOCK, s), F32)],
        compiler_params=_params("arbitrary", "arbitrary"),
        name="fox_prompt",
    )(proj, proj, proj, proj)


Q_ROWS = N_KV * SUBLANES


def _pages_per_step(n_pages):
    for pp in (16, 8, 4, 2, 1):
        if n_pages % pp == 0:
            return pp


def _page_specs(block, pp, page_of):
    def spec(r):
        return pl.BlockSpec((None,) + block, lambda bb, c, pt: (pt[bb, page_of(c, r)],) + (0,) * len(block))
    return [spec(r) for r in range(pp)]


def _per_seq(shape):
    return pl.BlockSpec((None,) + shape, lambda bb, c, pt: (bb,) + (0,) * len(shape))


def _shared(shape):
    return pl.BlockSpec(shape, lambda bb, c, pt: (0,) * len(shape))


def _kv_rows(pages, kh):
    return jnp.concatenate([p[pl.ds(kh, PAGE, stride=N_KV), :] for p in pages], axis=0).astype(BF16)


def _softmax_update(q_ref, keys_of, values_of, bias, sel, m_ref, l_ref, acc_ref):
    lg = jnp.concatenate([_nt(_q_rows(q_ref, kh), keys_of(kh)) for kh in range(N_KV)], axis=0) * ATT_SCALE
    if bias is not None:
        lg = lg + bias
    if sel is not None:
        lg = jnp.where(sel, lg, NEG)
    m_old = m_ref[...]
    m_new = jnp.maximum(m_old, jnp.max(lg, axis=1, keepdims=True))
    corr = jnp.exp(m_old - m_new)
    e = jnp.exp(lg - m_new)
    if sel is not None:
        e = jnp.where(sel, e, 0.0)
    l_ref[...] = l_ref[...] * corr + jnp.sum(e, axis=1, keepdims=True)
    pv = jnp.concatenate([jnp.dot(e[kh * SUBLANES:(kh + 1) * SUBLANES].astype(BF16), values_of(kh),
                                  preferred_element_type=F32) for kh in range(N_KV)], axis=0)
    acc_ref[...] = acc_ref[...] * corr + pv
    m_ref[...] = m_new


def _softmax_init(m_ref, l_ref, acc_ref):
    m_ref[...] = jnp.full(m_ref.shape, NEG, F32)
    l_ref[...] = jnp.zeros(l_ref.shape, F32)
    acc_ref[...] = jnp.zeros(acc_ref.shape, F32)


def _softmax_scratch():
    return [pltpu.VMEM((Q_ROWS, 1), F32), pltpu.VMEM((Q_ROWS, 1), F32), pltpu.VMEM((Q_ROWS, HEAD_DIM), F32)]


def _q_rows(q_ref, kh):
    return q_ref[kh * SUBLANES:(kh + 1) * SUBLANES, :].astype(BF16)


def _dsa_sample_keys_body(pt_ref, iq_ref, iw_ref, ikn_ref, *rest, pp, topk, n_new):
    pages = rest[:pp]
    kp_ref, kn_ref, thr_ref, key_ref = rest[pp:]
    c = pl.program_id(1)
    past = key_ref.shape[1] - LANES
    q = iq_ref[...].astype(BF16)
    w = iw_ref[...]

    def keys_of(ikt):
        s = jnp.maximum(jnp.dot(q, ikt.astype(BF16), preferred_element_type=F32), 0.0) * w
        acc = s[0:SUBLANES]
        for h in range(1, IDX_HEADS):
            acc = acc + s[h * SUBLANES:(h + 1) * SUBLANES]
        return _to_key(acc * IDX_SCALE)

    kk = keys_of(jnp.concatenate([p[...] for p in pages], axis=1))
    kp_ref[...] = kk
    key_ref[:, pl.ds(pl.multiple_of(c * (pp * PAGE), pp * PAGE), pp * PAGE)] = kk

    @pl.when(c == 0)
    def _():
        row = lax.broadcasted_iota(I32, (SUBLANES, LANES), 0)
        lane = lax.broadcasted_iota(I32, (SUBLANES, LANES), 1)
        kn = jnp.where(lane <= row % n_new, keys_of(ikn_ref[...]), INT_MIN)
        kn_ref[...] = kn
        key_ref[:, past:past + LANES] = kn

    @pl.when(c == pl.num_programs(1) - 1)
    def _():
        thr_ref[...] = jnp.broadcast_to(_kth_largest_key(lambda: key_ref[...], SUBLANES, topk), thr_ref.shape)


def _dsa_sample_keys(page_table, iq2, iw2, ikt_new, ik_cache_t, n_new):
    bd, n_pages = page_table.shape
    pp = _pages_per_step(n_pages)
    past = n_pages * PAGE
    topk = min(TOPK_MAX, (past + n_new) // 4)
    return pl.pallas_call(
        functools.partial(_dsa_sample_keys_body, pp=pp, topk=topk, n_new=n_new),
        grid_spec=pltpu.PrefetchScalarGridSpec(
            num_scalar_prefetch=1,
            grid=(bd, n_pages // pp),
            in_specs=[_per_seq((IDX_HEADS * SUBLANES, IDX_DIM)), _per_seq((IDX_HEADS * SUBLANES, 1)),
                      _per_seq((IDX_DIM, LANES))]
            + _page_specs((IDX_DIM, PAGE), pp, lambda c, r: c * pp + r),
            out_specs=[
                pl.BlockSpec((None, SUBLANES, pp * PAGE), lambda bb, c, pt: (bb, 0, c)),
                _per_seq((SUBLANES, LANES)),
                _per_seq((SUBLANES, LANES)),
            ],
            scratch_shapes=[pltpu.VMEM((SUBLANES, past + LANES), I32)],
        ),
        out_shape=[jax.ShapeDtypeStruct((bd, SUBLANES, past), I32),
                   jax.ShapeDtypeStruct((bd, SUBLANES, LANES), I32),
                   jax.ShapeDtypeStruct((bd, SUBLANES, LANES), I32)],
        compiler_params=_params("arbitrary", "arbitrary"),
        name="dsa_sample_keys",
    )(page_table, iq2, iw2, ikt_new, *([ik_cache_t] * pp))


def _dsa_sample_attend_body(pt_ref, q_ref, kp_ref, kn_ref, thr_ref, knew_ref, vnew_ref, *rest, pp):
    k_pages, v_pages = rest[:pp], rest[pp:2 * pp]
    o_ref, m_ref, l_ref, acc_ref = rest[2 * pp:]
    c = pl.program_id(1)
    thr = jnp.concatenate([thr_ref[:, 0:1]] * N_KV, axis=0)

    def selected(keys):
        keys = jnp.concatenate([keys] * N_KV, axis=0)
        return (keys >= thr) & (keys > INT_MIN)

    @pl.when(c == 0)
    def _():
        _softmax_init(m_ref, l_ref, acc_ref)
        _softmax_update(q_ref, lambda kh: knew_ref[kh].astype(BF16), lambda kh: vnew_ref[kh].astype(BF16),
                        None, selected(kn_ref[...]), m_ref, l_ref, acc_ref)

    _softmax_update(q_ref, functools.partial(_kv_rows, k_pages), functools.partial(_kv_rows, v_pages),
                    None, selected(kp_ref[...]), m_ref, l_ref, acc_ref)

    @pl.when(c == pl.num_programs(1) - 1)
    def _():
        o_ref[...] = acc_ref[...] / l_ref[...]


def _dsa_sample_attend(page_table, q_s, keys_past, keys_new, thr, k_new, v_new, k_cache, v_cache):
    bd, n_pages = page_table.shape
    pp = _pages_per_step(n_pages)
    page_of = lambda c, r: c * pp + r
    kv_block = (PAGE * N_KV, HEAD_DIM)
    return pl.pallas_call(
        functools.partial(_dsa_sample_attend_body, pp=pp),
        grid_spec=pltpu.PrefetchScalarGridSpec(
            num_scalar_prefetch=1,
            grid=(bd, n_pages // pp),
            in_specs=[_per_seq((Q_ROWS, HEAD_DIM)),
                      pl.BlockSpec((None, SUBLANES, pp * PAGE), lambda bb, c, pt: (bb, 0, c)),
                      _per_seq((SUBLANES, LANES)), _per_seq((SUBLANES, LANES)),
                      _per_seq((N_KV, LANES, HEAD_DIM)), _per_seq((N_KV, LANES, HEAD_DIM))]
            + _page_specs(kv_block, pp, page_of) + _page_specs(kv_block, pp, page_of),
            out_specs=_per_seq((Q_ROWS, HEAD_DIM)),
            scratch_shapes=_softmax_scratch(),
        ),
        out_shape=jax.ShapeDtypeStruct((bd, Q_ROWS, HEAD_DIM), F32),
        compiler_params=_params("arbitrary", "arbitrary"),
        name="dsa_sample_attend",
    )(page_table, q_s, keys_past, keys_new, thr, k_new, v_new, *([k_cache] * pp), *([v_cache] * pp))


def _fox_sample_body(pt_ref, q_ref, lfn_ref, knew_ref, vnew_ref, rep_ref, later_ref, *rest, pp, n_new):
    lf_pages, k_pages, v_pages = rest[:pp], rest[pp:2 * pp], rest[2 * pp:3 * pp]
    o_ref, m_ref, l_ref, acc_ref, cq_ref, carry_ref = rest[3 * pp:]
    c = pl.program_id(1)
    hi = lax.Precision.HIGHEST

    @pl.when(c == 0)
    def _():
        _softmax_init(m_ref, l_ref, acc_ref)
        carry_ref[...] = jnp.zeros(carry_ref.shape, F32)
        r_io = lax.broadcasted_iota(I32, (LANES, LANES), 0)
        c_io = lax.broadcasted_iota(I32, (LANES, LANES), 1)
        incl = jnp.where(r_io <= c_io, 1.0, 0.0).astype(F32)
        cum = jnp.dot(lfn_ref[...], incl, precision=hi, preferred_element_type=F32)
        cg = jnp.dot(rep_ref[0:Q_ROWS, 0:N_HEADS], cum, precision=hi, preferred_element_type=F32)
        row = lax.broadcasted_iota(I32, (Q_ROWS, LANES), 0)
        lane = lax.broadcasted_iota(I32, (Q_ROWS, LANES), 1)
        own = lane == row % n_new
        cq = jnp.sum(jnp.where(own, cg, 0.0), axis=1, keepdims=True)
        cq_ref[...] = cq
        _softmax_update(q_ref, lambda kh: knew_ref[kh].astype(BF16), lambda kh: vnew_ref[kh].astype(BF16),
                        cq - cg, lane <= row % n_new, m_ref, l_ref, acc_ref)

    lf_all = jnp.concatenate([p[...] for p in lf_pages], axis=0)
    lf_rows = jnp.dot(rep_ref[...], lf_all, precision=hi, preferred_element_type=F32)
    within = jnp.dot(lf_rows, later_ref[...], precision=hi, preferred_element_type=F32)
    total = within[:, 0:1] + lf_rows[:, 0:1]
    run = carry_ref[...]
    biases = []
    for r in range(pp):
        biases.append(within[r * Q_ROWS:(r + 1) * Q_ROWS] + (run + cq_ref[...]))
        run = run + total[r * Q_ROWS:(r + 1) * Q_ROWS]
    carry_ref[...] = run
    _softmax_update(q_ref, functools.partial(_kv_rows, k_pages), functools.partial(_kv_rows, v_pages),
                    jnp.concatenate(biases, axis=1), None, m_ref, l_ref, acc_ref)

    @pl.when(c == pl.num_programs(1) - 1)
    def _():
        o_ref[...] = acc_ref[...] / l_ref[...]


def _fox_sample(page_table, q_s, lft_new, k_new, v_new, lf_cache_t, k_cache, v_cache, n_new):
    bd, n_pages = page_table.shape
    pp = _pages_per_step(n_pages)
    page_of = lambda c, r: n_pages - 1 - (c * pp + r)
    kv_block = (PAGE * N_KV, HEAD_DIM)
    row_head = np.arange(Q_ROWS) // n_new
    rep_one = (row_head[:, None] == np.arange(N_HEADS)[None, :]).astype(np.float32)
    rep = jnp.asarray(np.kron(np.eye(pp, dtype=np.float32), rep_one))
    later = jnp.asarray((np.arange(PAGE)[:, None] > np.arange(PAGE)[None, :]).astype(np.float32))
    return pl.pallas_call(
        functools.partial(_fox_sample_body, pp=pp, n_new=n_new),
        grid_spec=pltpu.PrefetchScalarGridSpec(
            num_scalar_prefetch=1,
            grid=(bd, n_pages // pp),
            in_specs=[_per_seq((Q_ROWS, HEAD_DIM)), _per_seq((N_HEADS, LANES)),
                      _per_seq((N_KV, LANES, HEAD_DIM)), _per_seq((N_KV, LANES, HEAD_DIM)),
                      _shared((pp * Q_ROWS, pp * N_HEADS)), _shared((PAGE, PAGE))]
            + _page_specs((N_HEADS, PAGE), pp, page_of) + _page_specs(kv_block, pp, page_of)
            + _page_specs(kv_block, pp, page_of),
            out_specs=_per_seq((Q_ROWS, HEAD_DIM)),
            scratch_shapes=_softmax_scratch() + [pltpu.VMEM((Q_ROWS, 1), F32), pltpu.VMEM((Q_ROWS, 1), F32)],
        ),
        out_shape=jax.ShapeDtypeStruct((bd, Q_ROWS, HEAD_DIM), F32),
        compiler_params=_params("arbitrary", "arbitrary"),
        name="fox_sample",
    )(page_table, q_s, lft_new, k_new, v_new, rep, later, *([lf_cache_t] * pp), *([k_cache] * pp), *([v_cache] * pp))


def _merge_body(x_ref, oa_ref, ob_ref, ga_ref, gb_ref, wa_ref, wb_ref, wo_ref, gn_ref, x1_ref, h2_ref):
    a = jnp.dot(oa_ref[...].astype(BF16), wa_ref[...], preferred_element_type=F32)
    b = jnp.dot(ob_ref[...].astype(BF16), wb_ref[...], preferred_element_type=F32)
    merged = jax.nn.sigmoid(ga_ref[...]) * a + jax.nn.sigmoid(gb_ref[...]) * b
    x1 = x_ref[...] + jnp.dot(merged.astype(BF16), wo_ref[...], preferred_element_type=F32)
    x1_ref[...] = x1
    h2_ref[...] = _rms(x1, gn_ref[...]).astype(BF16)


def _merge(x2d, o_a, o_b, proj, wa, wb, wo, ffn_norm, lay, tm):
    n, d = x2d.shape
    const = lambda shape: pl.BlockSpec(shape, lambda i: (0, 0), pipeline_mode=pl.Buffered(1))
    return pl.pallas_call(
        _merge_body,
        grid=(n // tm,),
        in_specs=[
            pl.BlockSpec((tm, d), lambda i: (i, 0)),
            pl.BlockSpec((tm, 1024), lambda i: (i, 0)),
            pl.BlockSpec((tm, 1024), lambda i: (i, 0)),
            pl.BlockSpec((tm, d), lambda i: (i, lay.ga // d)),
            pl.BlockSpec((tm, d), lambda i: (i, lay.gb // d)),
            const((1024, d)), const((1024, d)), const((d, d)), const((1, d)),
        ],
        out_specs=[pl.BlockSpec((tm, d), lambda i: (i, 0)), pl.BlockSpec((tm, d), lambda i: (i, 0))],
        out_shape=[jax.ShapeDtypeStruct((n, d), F32), jax.ShapeDtypeStruct((n, d), BF16)],
        compiler_params=_params("arbitrary"),
        name="merge",
    )(x2d, o_a, o_b, proj, proj, wa, wb, wo, ffn_norm)


def _ffn_tail(f, n_f, contrib, x1_ref, fn_ref, y_ref, acc_ref):
    @pl.when(f == 0)
    def _():
        acc_ref[...] = contrib

    @pl.when(f > 0)
    def _():
        acc_ref[...] += contrib

    @pl.when(f == n_f - 1)
    def _():
        y_ref[...] = _rms(x1_ref[...] + acc_ref[...], fn_ref[...])


def _ffn_prompt_body(h_ref, halo_ref, wg_ref, wu_ref, wd_ref, cw_ref, cb_ref, x1_ref, fn_ref,
                     y_ref, tail_ref, acc_ref, ext_ref, *, tiles_per_seq):
    i = pl.program_id(0)
    f = pl.program_id(1)
    tm = h_ref.shape[0]
    h = h_ref[...]
    gp = jnp.dot(h, wg_ref[...], preferred_element_type=F32)
    up = jnp.dot(h, wu_ref[...], preferred_element_type=F32)
    halo = jnp.dot(halo_ref[...], wg_ref[...], preferred_element_type=F32)
    ext_ref[0:SUBLANES, :] = jnp.where(i % tiles_per_seq == 0, 0.0, halo)
    ext_ref[SUBLANES:, :] = gp
    conv = cb_ref[...] + cw_ref[CONV_W - 1:CONV_W, :] * gp
    for j in range(CONV_W - 1):
        back = CONV_W - 1 - j
        conv = conv + cw_ref[j:j + 1, :] * ext_ref[SUBLANES - back:SUBLANES - back + tm, :]
    act = (conv * jax.nn.sigmoid(conv)) * up
    tail_ref[...] = gp[tm - SUBLANES:, :]
    contrib = jnp.dot(act.astype(BF16), wd_ref[...], preferred_element_type=F32)
    _ffn_tail(f, pl.num_programs(1), contrib, x1_ref, fn_ref, y_ref, acc_ref)


def _ffn_prompt(h2, x1, wg, wu, wd, conv_w, conv_b, final_norm, s, tm, tf):
    n, d = x1.shape
    ff = wg.shape[1]
    assert s % tm == 0 and tm % SUBLANES == 0
    hb = tm // SUBLANES
    return pl.pallas_call(
        functools.partial(_ffn_prompt_body, tiles_per_seq=s // tm),
        grid=(n // tm, ff // tf),
        in_specs=[
            pl.BlockSpec((tm, d), lambda i, f: (i, 0)),
            pl.BlockSpec((SUBLANES, d), lambda i, f: (jnp.maximum(i * hb - 1, 0), 0)),
            pl.BlockSpec((d, tf), lambda i, f: (0, f)),
            pl.BlockSpec((d, tf), lambda i, f: (0, f)),
            pl.BlockSpec((tf, d), lambda i, f: (f, 0)),
            pl.BlockSpec((CONV_W, tf), lambda i, f: (0, f)),
            pl.BlockSpec((1, tf), lambda i, f: (0, f)),
            pl.BlockSpec((tm, d), lambda i, f: (i, 0)),
            pl.BlockSpec((1, d), lambda i, f: (0, 0)),
        ],
        out_specs=[pl.BlockSpec((tm, d), lambda i, f: (i, 0)),
                   pl.BlockSpec((SUBLANES, tf), lambda i, f: (i, f))],
        out_shape=[jax.ShapeDtypeStruct((n, d), F32), jax.ShapeDtypeStruct((n // tm * SUBLANES, ff), F32)],
        scratch_shapes=[pltpu.VMEM((tm, d), F32), pltpu.VMEM((tm + SUBLANES, tf), F32)],
        compiler_params=_params("arbitrary", "arbitrary"),
        name="ffn_prompt",
    )(h2, h2, wg, wu, wd, conv_w, conv_b, x1, final_norm)


def _ffn_sample_body(h_ref, st_ref, wg_ref, wu_ref, wd_ref, cw_ref, cb_ref, x1_ref, fn_ref,
                     y_ref, new_st_ref, acc_ref, ext_ref, *, bd):
    f = pl.program_id(0)
    n = h_ref.shape[0]
    h = h_ref[...]
    gp = jnp.dot(h, wg_ref[...], preferred_element_type=F32)
    up = jnp.dot(h, wu_ref[...], preferred_element_type=F32)
    keep = (CONV_W - 1) * bd
    ext_ref[0:keep, :] = st_ref[...]
    ext_ref[keep:, :] = gp
    conv = cb_ref[...]
    for j in range(CONV_W):
        conv = conv + cw_ref[j:j + 1, :] * ext_ref[j * bd:j * bd + n, :]
    act = (conv * jax.nn.sigmoid(conv)) * up
    new_st_ref[...] = gp[n - keep:, :]
    contrib = jnp.dot(act.astype(BF16), wd_ref[...], preferred_element_type=F32)
    _ffn_tail(f, pl.num_programs(0), contrib, x1_ref, fn_ref, y_ref, acc_ref)


def _ffn_sample(h2, x1, state, wg, wu, wd, conv_w, conv_b, final_norm, bd, tf):
    n, d = x1.shape
    ff = wg.shape[1]
    keep = (CONV_W - 1) * bd
    assert bd % SUBLANES == 0 and n >= keep
    return pl.pallas_call(
        functools.partial(_ffn_sample_body, bd=bd),
        grid=(ff // tf,),
        in_specs=[
            pl.BlockSpec((n, d), lambda f: (0, 0)),
            pl.BlockSpec((keep, tf), lambda f: (0, f)),
            pl.BlockSpec((d, tf), lambda f: (0, f)),
            pl.BlockSpec((d, tf), lambda f: (0, f)),
            pl.BlockSpec((tf, d), lambda f: (f, 0)),
            pl.BlockSpec((CONV_W, tf), lambda f: (0, f)),
            pl.BlockSpec((1, tf), lambda f: (0, f)),
            pl.BlockSpec((n, d), lambda f: (0, 0)),
            pl.BlockSpec((1, d), lambda f: (0, 0)),
        ],
        out_specs=[pl.BlockSpec((n, d), lambda f: (0, 0)), pl.BlockSpec((keep, tf), lambda f: (0, f))],
        out_shape=[jax.ShapeDtypeStruct((n, d), F32), jax.ShapeDtypeStruct((keep, ff), F32)],
        scratch_shapes=[pltpu.VMEM((n, d), F32), pltpu.VMEM((keep + n, tf), F32)],
        compiler_params=_params("arbitrary"),
        name="ffn_sample",
    )(h2, state, wg, wu, wd, conv_w, conv_b, x1, final_norm)


def _largest_divisor(n, candidates):
    for c in candidates:
        if n % c == 0:
            return c
    raise ValueError(f"no tile for {n} among {candidates}")


def kernel(x_prompt, x_sample, cache_dsa_k, cache_dsa_v, cache_idx_k, cache_fox_k, cache_fox_v, cache_fox_logf, state_ffn_conv, page_table, attn_norm, w_in, b_forget, w_branch_a, w_branch_b, w_out, ffn_norm, w_gate, w_up, w_down, conv_w, conv_b, final_norm):
    b, s, d = x_prompt.shape
    bd, t_new, _ = x_sample.shape
    depth = attn_norm.shape[0]
    assert depth == 1 and t_new * GROUP == SUBLANES and s % Q_BLOCK == 0
    n_pages = page_table.shape[1]
    past = n_pages * PAGE
    n_pool = cache_dsa_k.shape[1]
    ff = w_gate.shape[2]
    lay = _Layout(d)

    w_perm = _permute_w_in(w_in[0], lay)
    bf_row = jnp.zeros((1, LANES), F32).at[0, LOGF_LANE:LOGF_LANE + N_HEADS].set(b_forget[0])
    wa, wb, wo = (w[0].astype(BF16) for w in (w_branch_a, w_branch_b, w_out))
    wg, wu, wd = (w[0].astype(BF16) for w in (w_gate, w_up, w_down))
    g_attn, g_ffn, g_fin = attn_norm[0][None, :], ffn_norm[0][None, :], final_norm[None, :]
    cw, cb = conv_w[0], conv_b[0][None, :]

    tm_p = _largest_divisor(s, (1024, 512, 256, 128))
    tab_p = _rope_tables(jnp.arange(s, dtype=I32))
    tab_s = jnp.tile(_rope_tables(past + jnp.arange(t_new, dtype=I32)), (bd, 1))
    xp2 = x_prompt.reshape(b * s, d)
    xs2 = x_sample.reshape(bd * t_new, d)
    proj_p = _norm_proj(xp2, g_attn, w_perm, tab_p, bf_row, lay, tm_p)
    proj_s = _norm_proj(xs2, g_attn, w_perm, tab_s, bf_row, lay, bd * t_new)

    oa_p = _dsa_prompt(proj_p, b, s, lay)
    ob_p = _fox_prompt(proj_p, b, s, lay)

    def cols(name, width):
        o = getattr(lay, name)
        return proj_s[:, o:o + width].reshape(bd, t_new, width)

    def heads_major(x):
        x = x.reshape(bd, t_new, N_KV, GROUP, HEAD_DIM).transpose(0, 2, 3, 1, 4)
        return x.reshape(bd, Q_ROWS, HEAD_DIM)

    def new_kv(x):
        x = x.reshape(bd, t_new, N_KV, HEAD_DIM).transpose(0, 2, 1, 3)
        return jnp.pad(x, ((0, 0), (0, 0), (0, LANES - t_new), (0, 0)))

    def heads_back(o):
        o = o.reshape(bd, N_KV, GROUP, t_new, HEAD_DIM).transpose(0, 3, 1, 2, 4)
        return o.reshape(bd * t_new, N_HEADS * HEAD_DIM)

    ka_s, va_s, kb_s, vb_s = cols("ka", 512), cols("va", 512), cols("kb", 512), cols("vb", 512)
    ik_s = cols("ik", IDX_DIM)
    iwf_s = cols("iwf", LANES)
    logf_s = iwf_s[..., LOGF_LANE:LOGF_LANE + N_HEADS]
    iq_s = cols("iq", 1024).reshape(bd, t_new, IDX_HEADS, IDX_DIM).transpose(0, 2, 1, 3)
    iq2 = jnp.broadcast_to(iq_s[:, :, None], (bd, IDX_HEADS, GROUP, t_new, IDX_DIM)).reshape(bd, IDX_HEADS * SUBLANES, IDX_DIM)
    iw_s = iwf_s[..., IW_LANE:IW_LANE + IDX_HEADS].transpose(0, 2, 1)
    iw2 = jnp.broadcast_to(iw_s[:, :, None], (bd, IDX_HEADS, GROUP, t_new)).reshape(bd, IDX_HEADS * SUBLANES, 1)
    ikt_new = jnp.pad(ik_s.transpose(0, 2, 1), ((0, 0), (0, 0), (0, LANES - t_new)))
    lft_new = jnp.pad(logf_s.transpose(0, 2, 1), ((0, 0), (0, 0), (0, LANES - t_new)))

    ik_cache_t = jnp.swapaxes(cache_idx_k[0], 1, 2)
    lf_cache_t = jnp.swapaxes(cache_fox_logf[0], 1, 2)
    kv_rows = lambda c: c.reshape(n_pool, PAGE * N_KV, HEAD_DIM)

    keys_past, keys_new, thr = _dsa_sample_keys(page_table, iq2, iw2, ikt_new, ik_cache_t, t_new)
    oa_s = heads_back(_dsa_sample_attend(page_table, heads_major(cols("qa", 1024)), keys_past, keys_new, thr,
                                         new_kv(ka_s), new_kv(va_s), kv_rows(cache_dsa_k), kv_rows(cache_dsa_v)))
    ob_s = heads_back(_fox_sample(page_table, heads_major(cols("qb", 1024)), lft_new, new_kv(kb_s), new_kv(vb_s),
                                  lf_cache_t, kv_rows(cache_fox_k), kv_rows(cache_fox_v), t_new))

    tm_m = _largest_divisor(s, (256, 128))
    x1_p, h2_p = _merge(xp2, oa_p, ob_p, proj_p, wa, wb, wo, g_ffn, lay, tm_m)
    x1_s, h2_s = _merge(xs2, oa_s, ob_s, proj_s, wa, wb, wo, g_ffn, lay, bd * t_new)

    tf = _largest_divisor(ff, (512, 256, 128))
    tm_f = _largest_divisor(s, (512, 256, 128))
    y_p, tails = _ffn_prompt(h2_p, x1_p, wg, wu, wd, cw, cb, g_fin, s, tm_f, tf)
    conv_p = tails.reshape(b, s // tm_f, SUBLANES, ff)[:, -1, SUBLANES - (CONV_W - 1):, :]

    t_major = lambda x: x.reshape(bd, t_new, -1).transpose(1, 0, 2).reshape(t_new * bd, -1)
    state_t = state_ffn_conv[0].transpose(1, 0, 2).reshape((CONV_W - 1) * bd, ff)
    y_s_t, st_t = _ffn_sample(t_major(h2_s), t_major(x1_s), state_t, wg, wu, wd, cw, cb, g_fin, bd, tf)
    y_s = y_s_t.reshape(t_new, bd, d).transpose(1, 0, 2)
    conv_s = st_t.reshape(CONV_W - 1, bd, ff).transpose(1, 0, 2)

    def pcols(name, width):
        o = getattr(lay, name)
        return proj_p[:, o:o + width]

    p_out = (pcols("ka", 512).reshape(1, b, s, N_KV, HEAD_DIM), pcols("va", 512).reshape(1, b, s, N_KV, HEAD_DIM),
             pcols("ik", IDX_DIM).reshape(1, b, s, IDX_DIM),
             pcols("kb", 512).reshape(1, b, s, N_KV, HEAD_DIM), pcols("vb", 512).reshape(1, b, s, N_KV, HEAD_DIM),
             proj_p[:, lay.iwf + LOGF_LANE:lay.iwf + LOGF_LANE + N_HEADS].reshape(1, b, s, N_HEADS),
             conv_p[None])
    s_out = (ka_s.reshape(1, bd, t_new, N_KV, HEAD_DIM), va_s.reshape(1, bd, t_new, N_KV, HEAD_DIM), ik_s[None],
             kb_s.reshape(1, bd, t_new, N_KV, HEAD_DIM), vb_s.reshape(1, bd, t_new, N_KV, HEAD_DIM), logf_s[None],
             conv_s[None])
    return (y_p.reshape(b, s, d), y_s) + p_out + s_out
```

```python
import functools

import numpy as np
import jax
import jax.numpy as jnp
from jax import lax
from jax.experimental import pallas as pl
from jax.experimental.pallas import tpu as pltpu

F32 = jnp.float32
BF16 = jnp.bfloat16
I32 = jnp.int32

HEAD_DIM = 128
N_HEADS = 8
N_KV = 4
GROUP = N_HEADS // N_KV
IDX_HEADS = 16
IDX_DIM = 64
TOPK_MAX = 256
ROPE_THETA = 500000.0
ROT_DIM = HEAD_DIM // 4
IDX_ROT_DIM = IDX_DIM // 4
PAGE = 128
Q_BLOCK = 128
CONV_W = 3
RMS_EPS = 1e-6
ATT_SCALE = HEAD_DIM ** -0.5
IDX_SCALE = (IDX_HEADS * IDX_DIM) ** -0.5

LANES = 128
SUBLANES = 8
NEG = -1e30
INT_MIN = -2 ** 31
VMEM_LIMIT = 56 * 1024 * 1024

IW_LANE = 0
LOGF_LANE = IDX_HEADS

KV_NAMES = ("ka", "va", "kb", "vb")

NT_DIMS = (((1,), (1,)), ((), ()))


def _params(*sem):
    return pltpu.CompilerParams(dimension_semantics=sem, vmem_limit_bytes=VMEM_LIMIT)


def _nt(a, b):
    return lax.dot_general(a, b, NT_DIMS, preferred_element_type=F32)


def _rms(x, g):
    ms = jnp.mean(x * x, axis=-1, keepdims=True)
    return (x * lax.rsqrt(ms + RMS_EPS)) * g


class _Layout:
    def __init__(self, d_model):
        self.d = d_model
        self.tn = 512
        off = 0
        for name, size in (("ga", d_model), ("gb", d_model), ("qa", 1024), ("iq", 1024), ("qb", 1024),
                           ("ik", LANES), ("iwf", LANES)):
            assert off % size == 0, (name, off, size)
            setattr(self, name, off)
            off += size
        off = -(-off // self.tn) * self.tn
        self.proj_cols = off
        for name in KV_NAMES:
            setattr(self, name, off)
            off += N_KV * HEAD_DIM
        assert N_KV * HEAD_DIM == self.tn
        self.nc = off

    def chunk_kinds(self):
        kinds = ["plain"] * (self.nc // LANES)
        for name, size, kind in (("qa", 1024, "rope128"), ("iq", 1024, "rope64"), ("ik", LANES, "rope64"),
                                 ("iwf", LANES, "iwf"), ("ka", 512, "kv0rope"), ("va", 512, "kv1"),
                                 ("kb", 512, "kv2"), ("vb", 512, "kv3")):
            start = getattr(self, name) // LANES
            for c in range(size // LANES):
                kinds[start + c] = kind
        return kinds


def _w_in_plan(lay):
    d = lay.d
    sizes = (1024, 512, 512, 1024, IDX_DIM, IDX_HEADS, 1024, 512, 512, N_HEADS, d, d)
    names = ("qa", "ka", "va", "iq", "ik", "iw", "qb", "kb", "vb", "fl", "ga", "gb")
    offs = np.concatenate([[0], np.cumsum(sizes)])
    src = {n: int(offs[k]) for k, n in enumerate(names)}
    plan = [()] * (lay.nc // LANES)
    for name, size in (("ga", d), ("gb", d), ("qa", 1024), ("iq", 1024), ("qb", 1024),
                       ("ka", 512), ("va", 512), ("kb", 512), ("vb", 512)):
        for c in range(size // LANES):
            plan[getattr(lay, name) // LANES + c] = ((src[name] + c * LANES, 0, LANES),)
    plan[lay.ik // LANES] = ((src["ik"], 0, IDX_DIM),)
    plan[lay.iwf // LANES] = ((src["iw"], IW_LANE, IDX_HEADS), (src["fl"], LOGF_LANE, N_HEADS))
    return tuple(plan), int(offs[-1])


def _prep_w_body(w_ref, o_ref, *, plan):
    cols = w_ref.shape[1]
    for c, pieces in enumerate(plan):
        parts, pos = [], 0
        for first, at, height in pieces:
            if at > pos:
                parts.append(jnp.zeros((at - pos, cols), F32))
            parts.append(w_ref[first:first + height, :])
            pos = at + height
        if pos < LANES:
            parts.append(jnp.zeros((LANES - pos, cols), F32))
        chunk = parts[0] if len(parts) == 1 else jnp.concatenate(parts, axis=0)
        o_ref[c * LANES:(c + 1) * LANES, :] = chunk.astype(BF16)


def _prep_w_in_t(w_in_t, lay):
    n_src, d = w_in_t.shape
    plan, n_cols = _w_in_plan(lay)
    assert n_cols == n_src and all(f % SUBLANES == 0 and a % SUBLANES == 0 for p in plan for f, a, _ in p)
    tc = _largest_divisor(d, (256, 128))
    return pl.pallas_call(
        functools.partial(_prep_w_body, plan=plan),
        grid=(d // tc,),
        in_specs=[pl.BlockSpec((n_src, tc), lambda i: (0, i))],
        out_specs=pl.BlockSpec((lay.nc, tc), lambda i: (0, i)),
        out_shape=jax.ShapeDtypeStruct((lay.nc, d), BF16),
        compiler_params=_params("arbitrary"),
        name="prep_w_in",
    )(w_in_t)


def _rope_tables(pos):
    def one(rot_dim, period):
        half = rot_dim // 2
        inv_freq = jnp.power(ROPE_THETA, -jnp.arange(half, dtype=F32) * (2.0 / rot_dim))
        ang = pos.astype(F32)[:, None] * inv_freq[None, :]
        cos, sin = jnp.cos(ang), jnp.sin(ang)
        n = pos.shape[0]
        c = jnp.concatenate([cos, cos, jnp.ones((n, period - rot_dim), F32)], axis=1)
        sa = jnp.concatenate([-sin, jnp.zeros((n, period - half), F32)], axis=1)
        sb = jnp.concatenate([jnp.zeros((n, half), F32), sin, jnp.zeros((n, period - rot_dim), F32)], axis=1)
        rep = LANES // period
        return [jnp.tile(t, (1, rep)) for t in (c, sa, sb)]
    return jnp.concatenate(one(ROT_DIM, HEAD_DIM) + one(IDX_ROT_DIM, IDX_DIM), axis=1)


def _proj_body(x_ref, g_ref, w_ref, tab_ref, bf_ref, o_ref, *rest, tile_kinds):
    kv_refs, h_ref = rest[:len(KV_NAMES)], rest[len(KV_NAMES)]
    j = pl.program_id(1)
    tm = x_ref.shape[0]

    @pl.when(j == 0)
    def _():
        h_ref[...] = _rms(x_ref[...], g_ref[...]).astype(BF16)

    acc = _nt(h_ref[...], w_ref[...])

    def rope(a, base, half):
        c = tab_ref[:, base:base + LANES]
        sa = tab_ref[:, base + LANES:base + 2 * LANES]
        sb = tab_ref[:, base + 2 * LANES:base + 3 * LANES]
        return a * c + pltpu.roll(a, LANES - half, 1) * sa + pltpu.roll(a, half, 1) * sb

    def log_forget(a):
        z = a + bf_ref[...]
        ls = jnp.minimum(z, 0.0) - jnp.log1p(jnp.exp(-jnp.abs(z)))
        lane = lax.broadcasted_iota(I32, a.shape, 1)
        return jnp.where((lane >= LOGF_LANE) & (lane < LOGF_LANE + N_HEADS), ls, a)

    def emit(kinds):
        for c, kind in enumerate(kinds):
            a = acc[:, c * LANES:(c + 1) * LANES]
            if kind == "rope128":
                a = rope(a, 0, ROT_DIM // 2)
            elif kind == "rope64":
                a = rope(a, 3 * LANES, IDX_ROT_DIM // 2)
            elif kind == "iwf":
                a = log_forget(a)
            if kind.startswith("kv"):
                if kind.endswith("rope"):
                    a = rope(a, 0, ROT_DIM // 2)
                kv_refs[int(kind[2])][pl.ds(c, tm, stride=N_KV), :] = a
            else:
                o_ref[:, c * LANES:(c + 1) * LANES] = a

    groups = {}
    for t, kinds in enumerate(tile_kinds):
        groups.setdefault(kinds, []).append(t)
    for kinds, tiles in groups.items():
        cond = functools.reduce(jnp.logical_or, [j == t for t in tiles])
        if all(k == "plain" for k in kinds):
            @pl.when(cond)
            def _():
                o_ref[...] = acc
        else:
            pl.when(cond)(functools.partial(emit, kinds))


def _norm_proj(x2d, gamma, w_perm, tab, bf_row, lay, tm):
    n, d = x2d.shape
    tn = lay.tn
    kinds = lay.chunk_kinds()
    per = tn // LANES
    tile_kinds = tuple(tuple(kinds[t * per:(t + 1) * per]) for t in range(lay.nc // tn))
    tab_blocks = tab.shape[0] // tm
    last_proj_tile = lay.proj_cols // tn - 1
    kv_spec = pl.BlockSpec((tm * N_KV, HEAD_DIM), lambda i, j: (i, 0))
    return pl.pallas_call(
        functools.partial(_proj_body, tile_kinds=tile_kinds),
        grid=(n // tm, lay.nc // tn),
        in_specs=[
            pl.BlockSpec((tm, d), lambda i, j: (i, 0), pipeline_mode=pl.Buffered(1)),
            pl.BlockSpec((1, d), lambda i, j: (0, 0)),
            pl.BlockSpec((tn, d), lambda i, j: (j, 0)),
            pl.BlockSpec((tm, 6 * LANES), lambda i, j: (i % tab_blocks, 0)),
            pl.BlockSpec((1, LANES), lambda i, j: (0, 0)),
        ],
        out_specs=[pl.BlockSpec((tm, tn), lambda i, j: (i, jnp.minimum(j, last_proj_tile)))] + [kv_spec] * len(KV_NAMES),
        out_shape=[jax.ShapeDtypeStruct((n, lay.proj_cols), F32)]
        + [jax.ShapeDtypeStruct((n * N_KV, HEAD_DIM), F32)] * len(KV_NAMES),
        scratch_shapes=[pltpu.VMEM((tm, d), BF16)],
        compiler_params=_params("arbitrary", "arbitrary"),
        name="norm_proj",
    )(x2d, gamma, w_perm, tab, bf_row)


def _to_key(x):
    bits = pltpu.bitcast(x, I32)
    return jnp.where(bits < 0, bits ^ jnp.int32(0x7FFFFFFF), bits)


def _kth_largest_key(read_keys, rows, k):
    def body(it, res):
        cand = res + jnp.left_shift(jnp.int32(1), 31 - it)
        cnt = jnp.sum(jnp.where(read_keys() >= cand, 1.0, 0.0), axis=1, keepdims=True)
        return jnp.where(cnt >= k, cand, res)

    return lax.fori_loop(0, 32, body, jnp.full((rows, 1), INT_MIN, I32))


KEY_BUCKET = 512


def _key_limits(s):
    step = min(KEY_BUCKET, s)
    assert s % step == 0
    return tuple(range(step, s + 1, step))


def _for_causal_limit(i, limits, block_fn):
    q_end = (i + 1) * Q_BLOCK
    prev = 0
    for lim in limits:
        pl.when((q_end > prev) & (q_end <= lim))(functools.partial(block_fn, lim))
        prev = lim


def _load_kv_heads(src_ref, dst_ref):
    tokens = dst_ref.shape[0]
    for kh in range(N_KV):
        dst_ref[:, kh * HEAD_DIM:(kh + 1) * HEAD_DIM] = src_ref[pl.ds(kh, tokens, stride=N_KV), :].astype(BF16)


def _attend_heads(q_ref, kb_ref, vb_ref, o_ref, n, logit_bias):
    for kh in range(N_KV):
        kk = kb_ref[0:n, kh * HEAD_DIM:(kh + 1) * HEAD_DIM]
        vv = vb_ref[0:n, kh * HEAD_DIM:(kh + 1) * HEAD_DIM]
        for g in range(GROUP):
            h = kh * GROUP + g
            q = q_ref[:, h * HEAD_DIM:(h + 1) * HEAD_DIM].astype(BF16)
            lg = _nt(q, kk) * ATT_SCALE + logit_bias(h)
            m = jnp.max(lg, axis=1, keepdims=True)
            e = jnp.exp(lg - m)
            l = jnp.sum(e, axis=1, keepdims=True)
            o = jnp.dot(e.astype(BF16), vv, preferred_element_type=F32)
            o_ref[:, h * HEAD_DIM:(h + 1) * HEAD_DIM] = o / l


def _dsa_prompt_body(iq_ref, iwf_ref, ik_ref, q_ref, k_ref, v_ref, o_ref,
                     ikb_ref, kb_ref, vb_ref, sc_ref, key_ref, bias_ref, *, topk, limits):
    i = pl.program_id(1)

    @pl.when(i == 0)
    def _():
        ikb_ref[...] = ik_ref[:, :IDX_DIM].astype(BF16)
        _load_kv_heads(k_ref, kb_ref)
        _load_kv_heads(v_ref, vb_ref)

    def block(n):
        ikb = ikb_ref[0:n, :]
        for h in range(IDX_HEADS):
            qh = iq_ref[:, h * IDX_DIM:(h + 1) * IDX_DIM].astype(BF16)
            s = jnp.maximum(_nt(qh, ikb), 0.0) * iwf_ref[:, IW_LANE + h:IW_LANE + h + 1]
            if h == 0:
                sc_ref[:, 0:n] = s
            else:
                sc_ref[:, 0:n] += s
        row = i * Q_BLOCK + lax.broadcasted_iota(I32, (Q_BLOCK, n), 0)
        col = lax.broadcasted_iota(I32, (Q_BLOCK, n), 1)
        causal = col <= row
        key_ref[:, 0:n] = jnp.where(causal, _to_key(sc_ref[:, 0:n] * IDX_SCALE), INT_MIN)
        thr = _kth_largest_key(lambda: key_ref[:, 0:n], Q_BLOCK, topk)
        bias_ref[:, 0:n] = jnp.where((key_ref[:, 0:n] >= thr) & causal, 0.0, NEG)
        _attend_heads(q_ref, kb_ref, vb_ref, o_ref, n, lambda h: bias_ref[:, 0:n])

    _for_causal_limit(i, limits, block)


def _dsa_prompt(proj, k4, v4, b, s, lay):
    nb = s // Q_BLOCK
    topk = min(TOPK_MAX, s // 4)
    row = lambda bb, i: bb * nb + i
    return pl.pallas_call(
        functools.partial(_dsa_prompt_body, topk=topk, limits=_key_limits(s)),
        grid=(b, nb),
        in_specs=[
            pl.BlockSpec((Q_BLOCK, 1024), lambda bb, i: (row(bb, i), lay.iq // 1024)),
            pl.BlockSpec((Q_BLOCK, LANES), lambda bb, i: (row(bb, i), lay.iwf // LANES)),
            pl.BlockSpec((s, LANES), lambda bb, i: (bb, lay.ik // LANES)),
            pl.BlockSpec((Q_BLOCK, 1024), lambda bb, i: (row(bb, i), lay.qa // 1024)),
            pl.BlockSpec((s * N_KV, HEAD_DIM), lambda bb, i: (bb, 0)),
            pl.BlockSpec((s * N_KV, HEAD_DIM), lambda bb, i: (bb, 0)),
        ],
        out_specs=pl.BlockSpec((Q_BLOCK, 1024), lambda bb, i: (row(bb, i), 0)),
        out_shape=jax.ShapeDtypeStruct((b * s, 1024), F32),
        scratch_shapes=[pltpu.VMEM((s, IDX_DIM), BF16), pltpu.VMEM((s, 512), BF16), pltpu.VMEM((s, 512), BF16),
                        pltpu.VMEM((Q_BLOCK, s), F32), pltpu.VMEM((Q_BLOCK, s), I32), pltpu.VMEM((Q_BLOCK, s), F32)],
        compiler_params=_params("arbitrary", "arbitrary"),
        name="dsa_prompt",
    )(proj, proj, proj, proj, k4, v4)


CUM_BLOCK = 256


def _fox_prompt_body(q_ref, k_ref, v_ref, lf_ref, o_ref, kb_ref, vb_ref, c_ref, ct_ref, bias_ref, *, limits):
    i = pl.program_id(1)
    s_len = kb_ref.shape[0]

    @pl.when(i == 0)
    def _():
        _load_kv_heads(k_ref, kb_ref)
        _load_kv_heads(v_ref, vb_ref)
        r = lax.broadcasted_iota(I32, (CUM_BLOCK, CUM_BLOCK), 0)
        c = lax.broadcasted_iota(I32, (CUM_BLOCK, CUM_BLOCK), 1)
        tri = jnp.where(c <= r, 1.0, 0.0).astype(F32)
        carry = jnp.zeros((1, LANES), F32)
        for blk in range(s_len // CUM_BLOCK):
            xb = lf_ref[blk * CUM_BLOCK:(blk + 1) * CUM_BLOCK, :]
            cb = jnp.dot(tri, xb, precision=lax.Precision.HIGHEST, preferred_element_type=F32) + carry
            c_ref[blk * CUM_BLOCK:(blk + 1) * CUM_BLOCK, :] = cb
            carry = cb[CUM_BLOCK - 1:CUM_BLOCK, :]
        ct_ref[...] = c_ref[...].T

    start = pl.multiple_of(i * Q_BLOCK, Q_BLOCK)

    def block(n):
        row = i * Q_BLOCK + lax.broadcasted_iota(I32, (Q_BLOCK, n), 0)
        col = lax.broadcasted_iota(I32, (Q_BLOCK, n), 1)
        bias_ref[:, 0:n] = jnp.where(col <= row, 0.0, NEG)

        def logit_bias(h):
            cq = c_ref[pl.ds(start, Q_BLOCK), LOGF_LANE + h:LOGF_LANE + h + 1]
            ck = ct_ref[LOGF_LANE + h:LOGF_LANE + h + 1, 0:n]
            return (cq - ck) + bias_ref[:, 0:n]

        _attend_heads(q_ref, kb_ref, vb_ref, o_ref, n, logit_bias)

    _for_causal_limit(i, limits, block)


def _fox_prompt(proj, k4, v4, b, s, lay):
    nb = s // Q_BLOCK
    assert s % CUM_BLOCK == 0
    row = lambda bb, i: bb * nb + i
    return pl.pallas_call(
        functools.partial(_fox_prompt_body, limits=_key_limits(s)),
        grid=(b, nb),
        in_specs=[
            pl.BlockSpec((Q_BLOCK, 1024), lambda bb, i: (row(bb, i), lay.qb // 1024)),
            pl.BlockSpec((s * N_KV, HEAD_DIM), lambda bb, i: (bb, 0)),
            pl.BlockSpec((s * N_KV, HEAD_DIM), lambda bb, i: (bb, 0)),
            pl.BlockSpec((s, LANES), lambda bb, i: (bb, lay.iwf // LANES)),
        ],
        out_specs=pl.BlockSpec((Q_BLOCK, 1024), lambda bb, i: (row(bb, i), 0)),
        out_shape=jax.ShapeDtypeStruct((b * s, 1024), F32),
        scratch_shapes=[pltpu.VMEM((s, 512), BF16), pltpu.VMEM((s, 512), BF16),
                        pltpu.VMEM((s, LANES), F32), pltpu.VMEM((LANES, s), F32), pltpu.VMEM((Q_BLOCK, s), F32)],
        compiler_params=_params("arbitrary", "arbitrary"),
        name="fox_prompt",
    )(proj, k4, v4, proj)


Q_ROWS = N_KV * SUBLANES


def _pages_per_step(n_pages, most=16):
    for pp in (32, 16, 8, 4, 2, 1):
        if pp <= most and n_pages % pp == 0:
            return pp


def _page_specs(block, pp, page_of):
    def spec(r):
        return pl.BlockSpec((None,) + block, lambda bb, c, pt: (pt[bb, page_of(c, r)],) + (0,) * len(block))
    return [spec(r) for r in range(pp)]


def _per_seq(shape):
    return pl.BlockSpec((None,) + shape, lambda bb, c, pt: (bb,) + (0,) * len(shape))


def _shared(shape):
    return pl.BlockSpec(shape, lambda bb, c, pt: (0,) * len(shape))


def _kv_rows(pages, kh):
    return jnp.concatenate([p[pl.ds(kh, PAGE, stride=N_KV), :] for p in pages], axis=0).astype(BF16)


def _softmax_update(q_ref, keys_of, values_of, bias, sel, m_ref, l_ref, acc_ref):
    lg = jnp.concatenate([_nt(_q_rows(q_ref, kh), keys_of(kh)) for kh in range(N_KV)], axis=0) * ATT_SCALE
    if bias is not None:
        lg = lg + bias
    if sel is not None:
        lg = jnp.where(sel, lg, NEG)
    m_old = m_ref[...]
    m_new = jnp.maximum(m_old, jnp.max(lg, axis=1, keepdims=True))
    corr = jnp.exp(m_old - m_new)
    e = jnp.exp(lg - m_new)
    if sel is not None:
        e = jnp.where(sel, e, 0.0)
    l_ref[...] = l_ref[...] * corr + jnp.sum(e, axis=1, keepdims=True)
    pv = jnp.concatenate([jnp.dot(e[kh * SUBLANES:(kh + 1) * SUBLANES].astype(BF16), values_of(kh),
                                  preferred_element_type=F32) for kh in range(N_KV)], axis=0)
    acc_ref[...] = acc_ref[...] * corr + pv
    m_ref[...] = m_new


def _softmax_init(m_ref, l_ref, acc_ref):
    m_ref[...] = jnp.full(m_ref.shape, NEG, F32)
    l_ref[...] = jnp.zeros(l_ref.shape, F32)
    acc_ref[...] = jnp.zeros(acc_ref.shape, F32)


def _softmax_scratch():
    return [pltpu.VMEM((Q_ROWS, 1), F32), pltpu.VMEM((Q_ROWS, 1), F32), pltpu.VMEM((Q_ROWS, HEAD_DIM), F32)]


def _q_rows(q_ref, kh):
    return q_ref[kh * SUBLANES:(kh + 1) * SUBLANES, :].astype(BF16)


def _dsa_sample_keys_body(pt_ref, iq_ref, iw_ref, ikn_ref, *rest, pp, topk, n_new):
    pages = rest[:pp]
    kp_ref, kn_ref, thr_ref, key_ref = rest[pp:]
    c = pl.program_id(1)
    past = key_ref.shape[1] - LANES
    q = iq_ref[...].astype(BF16)
    w = iw_ref[...]

    def keys_of(ikt):
        s = jnp.maximum(jnp.dot(q, ikt.astype(BF16), preferred_element_type=F32), 0.0) * w
        acc = s[0:SUBLANES]
        for h in range(1, IDX_HEADS):
            acc = acc + s[h * SUBLANES:(h + 1) * SUBLANES]
        return _to_key(acc * IDX_SCALE)

    kk = keys_of(jnp.concatenate([p[...] for p in pages], axis=1))
    kp_ref[...] = kk
    key_ref[:, pl.ds(pl.multiple_of(c * (pp * PAGE), pp * PAGE), pp * PAGE)] = kk

    @pl.when(c == 0)
    def _():
        row = lax.broadcasted_iota(I32, (SUBLANES, LANES), 0)
        lane = lax.broadcasted_iota(I32, (SUBLANES, LANES), 1)
        kn = jnp.where(lane <= row % n_new, keys_of(ikn_ref[...]), INT_MIN)
        kn_ref[...] = kn
        key_ref[:, past:past + LANES] = kn

    @pl.when(c == pl.num_programs(1) - 1)
    def _():
        thr_ref[...] = jnp.broadcast_to(_kth_largest_key(lambda: key_ref[...], SUBLANES, topk), thr_ref.shape)


def _dsa_sample_keys(page_table, iq2, iw2, ikt_new, ik_cache_t, n_new):
    bd, n_pages = page_table.shape
    pp = _pages_per_step(n_pages, 32)
    past = n_pages * PAGE
    topk = min(TOPK_MAX, (past + n_new) // 4)
    return pl.pallas_call(
        functools.partial(_dsa_sample_keys_body, pp=pp, topk=topk, n_new=n_new),
        grid_spec=pltpu.PrefetchScalarGridSpec(
            num_scalar_prefetch=1,
            grid=(bd, n_pages // pp),
            in_specs=[_per_seq((IDX_HEADS * SUBLANES, IDX_DIM)), _per_seq((IDX_HEADS * SUBLANES, 1)),
                      _per_seq((IDX_DIM, LANES))]
            + _page_specs((IDX_DIM, PAGE), pp, lambda c, r: c * pp + r),
            out_specs=[
                pl.BlockSpec((None, SUBLANES, pp * PAGE), lambda bb, c, pt: (bb, 0, c)),
                _per_seq((SUBLANES, LANES)),
                _per_seq((SUBLANES, LANES)),
            ],
            scratch_shapes=[pltpu.VMEM((SUBLANES, past + LANES), I32)],
        ),
        out_shape=[jax.ShapeDtypeStruct((bd, SUBLANES, past), I32),
                   jax.ShapeDtypeStruct((bd, SUBLANES, LANES), I32),
                   jax.ShapeDtypeStruct((bd, SUBLANES, LANES), I32)],
        compiler_params=_params("arbitrary", "arbitrary"),
        name="dsa_sample_keys",
    )(page_table, iq2, iw2, ikt_new, *([ik_cache_t] * pp))


def _dsa_sample_attend_body(pt_ref, q_ref, kp_ref, kn_ref, thr_ref, knew_ref, vnew_ref, *rest, pp):
    k_pages, v_pages = rest[:pp], rest[pp:2 * pp]
    o_ref, m_ref, l_ref, acc_ref = rest[2 * pp:]
    c = pl.program_id(1)
    thr = jnp.concatenate([thr_ref[:, 0:1]] * N_KV, axis=0)

    def selected(keys):
        keys = jnp.concatenate([keys] * N_KV, axis=0)
        return (keys >= thr) & (keys > INT_MIN)

    @pl.when(c == 0)
    def _():
        _softmax_init(m_ref, l_ref, acc_ref)
        _softmax_update(q_ref, lambda kh: knew_ref[kh].astype(BF16), lambda kh: vnew_ref[kh].astype(BF16),
                        None, selected(kn_ref[...]), m_ref, l_ref, acc_ref)

    _softmax_update(q_ref, functools.partial(_kv_rows, k_pages), functools.partial(_kv_rows, v_pages),
                    None, selected(kp_ref[...]), m_ref, l_ref, acc_ref)

    @pl.when(c == pl.num_programs(1) - 1)
    def _():
        o_ref[...] = acc_ref[...] / l_ref[...]


def _dsa_sample_attend(page_table, q_s, keys_past, keys_new, thr, k_new, v_new, k_cache, v_cache):
    bd, n_pages = page_table.shape
    pp = _pages_per_step(n_pages)
    page_of = lambda c, r: c * pp + r
    kv_block = (PAGE * N_KV, HEAD_DIM)
    return pl.pallas_call(
        functools.partial(_dsa_sample_attend_body, pp=pp),
        grid_spec=pltpu.PrefetchScalarGridSpec(
            num_scalar_prefetch=1,
            grid=(bd, n_pages // pp),
            in_specs=[_per_seq((Q_ROWS, HEAD_DIM)),
                      pl.BlockSpec((None, SUBLANES, pp * PAGE), lambda bb, c, pt: (bb, 0, c)),
                      _per_seq((SUBLANES, LANES)), _per_seq((SUBLANES, LANES)),
                      _per_seq((N_KV, LANES, HEAD_DIM)), _per_seq((N_KV, LANES, HEAD_DIM))]
            + _page_specs(kv_block, pp, page_of) + _page_specs(kv_block, pp, page_of),
            out_specs=_per_seq((Q_ROWS, HEAD_DIM)),
            scratch_shapes=_softmax_scratch(),
        ),
        out_shape=jax.ShapeDtypeStruct((bd, Q_ROWS, HEAD_DIM), F32),
        compiler_params=_params("arbitrary", "arbitrary"),
        name="dsa_sample_attend",
    )(page_table, q_s, keys_past, keys_new, thr, k_new, v_new, *([k_cache] * pp), *([v_cache] * pp))


def _fox_sample_body(pt_ref, q_ref, lfn_ref, knew_ref, vnew_ref, rep_ref, later_ref, *rest, pp, n_new):
    lf_pages, k_pages, v_pages = rest[:pp], rest[pp:2 * pp], rest[2 * pp:3 * pp]
    o_ref, m_ref, l_ref, acc_ref, cq_ref, carry_ref = rest[3 * pp:]
    c = pl.program_id(1)
    hi = lax.Precision.HIGHEST

    @pl.when(c == 0)
    def _():
        _softmax_init(m_ref, l_ref, acc_ref)
        carry_ref[...] = jnp.zeros(carry_ref.shape, F32)
        r_io = lax.broadcasted_iota(I32, (LANES, LANES), 0)
        c_io = lax.broadcasted_iota(I32, (LANES, LANES), 1)
        incl = jnp.where(r_io <= c_io, 1.0, 0.0).astype(F32)
        cum = jnp.dot(lfn_ref[...], incl, precision=hi, preferred_element_type=F32)
        cg = jnp.dot(rep_ref[0:Q_ROWS, 0:N_HEADS], cum, precision=hi, preferred_element_type=F32)
        row = lax.broadcasted_iota(I32, (Q_ROWS, LANES), 0)
        lane = lax.broadcasted_iota(I32, (Q_ROWS, LANES), 1)
        own = lane == row % n_new
        cq = jnp.sum(jnp.where(own, cg, 0.0), axis=1, keepdims=True)
        cq_ref[...] = cq
        _softmax_update(q_ref, lambda kh: knew_ref[kh].astype(BF16), lambda kh: vnew_ref[kh].astype(BF16),
                        cq - cg, lane <= row % n_new, m_ref, l_ref, acc_ref)

    lf_all = jnp.concatenate([p[...] for p in lf_pages], axis=0)
    lf_rows = jnp.dot(rep_ref[...], lf_all, precision=hi, preferred_element_type=F32)
    within = jnp.dot(lf_rows, later_ref[...], precision=hi, preferred_element_type=F32)
    total = within[:, 0:1] + lf_rows[:, 0:1]
    run = carry_ref[...]
    biases = []
    for r in range(pp):
        biases.append(within[r * Q_ROWS:(r + 1) * Q_ROWS] + (run + cq_ref[...]))
        run = run + total[r * Q_ROWS:(r + 1) * Q_ROWS]
    carry_ref[...] = run
    _softmax_update(q_ref, functools.partial(_kv_rows, k_pages), functools.partial(_kv_rows, v_pages),
                    jnp.concatenate(biases, axis=1), None, m_ref, l_ref, acc_ref)

    @pl.when(c == pl.num_programs(1) - 1)
    def _():
        o_ref[...] = acc_ref[...] / l_ref[...]


def _fox_sample(page_table, q_s, lft_new, k_new, v_new, lf_cache_t, k_cache, v_cache, n_new):
    bd, n_pages = page_table.shape
    pp = _pages_per_step(n_pages)
    page_of = lambda c, r: n_pages - 1 - (c * pp + r)
    kv_block = (PAGE * N_KV, HEAD_DIM)
    row_head = np.arange(Q_ROWS) // n_new
    rep_one = (row_head[:, None] == np.arange(N_HEADS)[None, :]).astype(np.float32)
    rep = jnp.asarray(np.kron(np.eye(pp, dtype=np.float32), rep_one))
    later = jnp.asarray((np.arange(PAGE)[:, None] > np.arange(PAGE)[None, :]).astype(np.float32))
    return pl.pallas_call(
        functools.partial(_fox_sample_body, pp=pp, n_new=n_new),
        grid_spec=pltpu.PrefetchScalarGridSpec(
            num_scalar_prefetch=1,
            grid=(bd, n_pages // pp),
            in_specs=[_per_seq((Q_ROWS, HEAD_DIM)), _per_seq((N_HEADS, LANES)),
                      _per_seq((N_KV, LANES, HEAD_DIM)), _per_seq((N_KV, LANES, HEAD_DIM)),
                      _shared((pp * Q_ROWS, pp * N_HEADS)), _shared((PAGE, PAGE))]
            + _page_specs((N_HEADS, PAGE), pp, page_of) + _page_specs(kv_block, pp, page_of)
            + _page_specs(kv_block, pp, page_of),
            out_specs=_per_seq((Q_ROWS, HEAD_DIM)),
            scratch_shapes=_softmax_scratch() + [pltpu.VMEM((Q_ROWS, 1), F32), pltpu.VMEM((Q_ROWS, 1), F32)],
        ),
        out_shape=jax.ShapeDtypeStruct((bd, Q_ROWS, HEAD_DIM), F32),
        compiler_params=_params("arbitrary", "arbitrary"),
        name="fox_sample",
    )(page_table, q_s, lft_new, k_new, v_new, rep, later, *([lf_cache_t] * pp), *([k_cache] * pp), *([v_cache] * pp))


def _merge_body(x_ref, oa_ref, ob_ref, ga_ref, gb_ref, wa_ref, wb_ref, wo_ref, gn_ref, x1_ref, h2_ref):
    a = jnp.dot(oa_ref[...].astype(BF16), wa_ref[...], preferred_element_type=F32)
    b = jnp.dot(ob_ref[...].astype(BF16), wb_ref[...], preferred_element_type=F32)
    merged = jax.nn.sigmoid(ga_ref[...]) * a + jax.nn.sigmoid(gb_ref[...]) * b
    x1 = x_ref[...] + jnp.dot(merged.astype(BF16), wo_ref[...], preferred_element_type=F32)
    x1_ref[...] = x1
    h2_ref[...] = _rms(x1, gn_ref[...]).astype(BF16)


def _merge(x2d, o_a, o_b, proj, wa, wb, wo, ffn_norm, lay, tm):
    n, d = x2d.shape
    const = lambda shape: pl.BlockSpec(shape, lambda i: (0, 0), pipeline_mode=pl.Buffered(1))
    return pl.pallas_call(
        _merge_body,
        grid=(n // tm,),
        in_specs=[
            pl.BlockSpec((tm, d), lambda i: (i, 0)),
            pl.BlockSpec((tm, 1024), lambda i: (i, 0)),
            pl.BlockSpec((tm, 1024), lambda i: (i, 0)),
            pl.BlockSpec((tm, d), lambda i: (i, lay.ga // d)),
            pl.BlockSpec((tm, d), lambda i: (i, lay.gb // d)),
            const((1024, d)), const((1024, d)), const((d, d)), const((1, d)),
        ],
        out_specs=[pl.BlockSpec((tm, d), lambda i: (i, 0)), pl.BlockSpec((tm, d), lambda i: (i, 0))],
        out_shape=[jax.ShapeDtypeStruct((n, d), F32), jax.ShapeDtypeStruct((n, d), BF16)],
        compiler_params=_params("arbitrary"),
        name="merge",
    )(x2d, o_a, o_b, proj, proj, wa, wb, wo, ffn_norm)


FFN_ROW_CHUNKS = 2


def _ffn_zero_acc(f, acc_ref):
    @pl.when(f == 0)
    def _():
        acc_ref[...] = jnp.zeros(acc_ref.shape, F32)


def _ffn_finish(f, n_f, x1_ref, fn_ref, y_ref, acc_ref):
    @pl.when(f == n_f - 1)
    def _():
        y_ref[...] = _rms(x1_ref[...] + acc_ref[...], fn_ref[...])


def _ffn_rows(h, first, rows, n_rows, taps, wg_ref, wu_ref, wd_ref, cw_ref, cb_ref, ext_ref, acc_ref):
    gp = jnp.dot(h, wg_ref[...], preferred_element_type=F32)
    up = jnp.dot(h, wu_ref[...], preferred_element_type=F32)
    ext_ref[first + rows:first + rows + n_rows, :] = gp
    conv = cb_ref[...]
    for j, back in enumerate(taps):
        src = gp if back == 0 else ext_ref[first + rows - back:first + rows - back + n_rows, :]
        conv = conv + cw_ref[j:j + 1, :] * src
    act = (conv * jax.nn.sigmoid(conv)) * up
    acc_ref[rows:rows + n_rows, :] += jnp.dot(act.astype(BF16), wd_ref[...], preferred_element_type=F32)


def _ffn_prompt_body(h_ref, halo_ref, wg_ref, wu_ref, wd_ref, cw_ref, cb_ref, x1_ref, fn_ref,
                     y_ref, tail_ref, acc_ref, ext_ref, *, tiles_per_seq):
    i = pl.program_id(0)
    f = pl.program_id(1)
    tm = h_ref.shape[0]
    _ffn_zero_acc(f, acc_ref)
    halo = jnp.dot(halo_ref[...], wg_ref[...], preferred_element_type=F32)
    ext_ref[0:SUBLANES, :] = jnp.where(i % tiles_per_seq == 0, 0.0, halo)
    taps = tuple(CONV_W - 1 - j for j in range(CONV_W))
    rc = tm // FFN_ROW_CHUNKS
    for c in range(FFN_ROW_CHUNKS):
        _ffn_rows(h_ref[c * rc:(c + 1) * rc, :], SUBLANES, c * rc, rc, taps,
                  wg_ref, wu_ref, wd_ref, cw_ref, cb_ref, ext_ref, acc_ref)
    tail_ref[...] = ext_ref[tm:tm + SUBLANES, :]
    _ffn_finish(f, pl.num_programs(1), x1_ref, fn_ref, y_ref, acc_ref)


def _ffn_prompt(h2, x1, wg, wu, wd, conv_w, conv_b, final_norm, s, tm, tf):
    n, d = x1.shape
    ff = wg.shape[1]
    assert s % tm == 0 and tm % SUBLANES == 0
    hb = tm // SUBLANES
    return pl.pallas_call(
        functools.partial(_ffn_prompt_body, tiles_per_seq=s // tm),
        grid=(n // tm, ff // tf),
        in_specs=[
            pl.BlockSpec((tm, d), lambda i, f: (i, 0)),
            pl.BlockSpec((SUBLANES, d), lambda i, f: (jnp.maximum(i * hb - 1, 0), 0)),
            pl.BlockSpec((d, tf), lambda i, f: (0, f)),
            pl.BlockSpec((d, tf), lambda i, f: (0, f)),
            pl.BlockSpec((tf, d), lambda i, f: (f, 0)),
            pl.BlockSpec((CONV_W, tf), lambda i, f: (0, f)),
            pl.BlockSpec((1, tf), lambda i, f: (0, f)),
            pl.BlockSpec((tm, d), lambda i, f: (i, 0)),
            pl.BlockSpec((1, d), lambda i, f: (0, 0)),
        ],
        out_specs=[pl.BlockSpec((tm, d), lambda i, f: (i, 0)),
                   pl.BlockSpec((SUBLANES, tf), lambda i, f: (i, f))],
        out_shape=[jax.ShapeDtypeStruct((n, d), F32), jax.ShapeDtypeStruct((n // tm * SUBLANES, ff), F32)],
        scratch_shapes=[pltpu.VMEM((tm, d), F32), pltpu.VMEM((tm + SUBLANES, tf), F32)],
        compiler_params=_params("arbitrary", "arbitrary"),
        name="ffn_prompt",
    )(h2, h2, wg, wu, wd, conv_w, conv_b, x1, final_norm)


def _ffn_sample_body(h_ref, st_ref, wg_ref, wu_ref, wd_ref, cw_ref, cb_ref, x1_ref, fn_ref,
                     y_ref, new_st_ref, acc_ref, ext_ref, *, bd):
    f = pl.program_id(0)
    n = h_ref.shape[0]
    keep = (CONV_W - 1) * bd
    _ffn_zero_acc(f, acc_ref)
    ext_ref[0:keep, :] = st_ref[...]
    taps = tuple((CONV_W - 1 - j) * bd for j in range(CONV_W))
    _ffn_rows(h_ref[...], keep, 0, n, taps, wg_ref, wu_ref, wd_ref, cw_ref, cb_ref, ext_ref, acc_ref)
    new_st_ref[...] = ext_ref[n:n + keep, :]
    _ffn_finish(f, pl.num_programs(0), x1_ref, fn_ref, y_ref, acc_ref)


def _ffn_sample(h2, x1, state, wg, wu, wd, conv_w, conv_b, final_norm, bd, tf):
    n, d = x1.shape
    ff = wg.shape[1]
    keep = (CONV_W - 1) * bd
    assert bd % SUBLANES == 0 and n >= keep
    return pl.pallas_call(
        functools.partial(_ffn_sample_body, bd=bd),
        grid=(ff // tf,),
        in_specs=[
            pl.BlockSpec((n, d), lambda f: (0, 0)),
            pl.BlockSpec((keep, tf), lambda f: (0, f)),
            pl.BlockSpec((d, tf), lambda f: (0, f)),
            pl.BlockSpec((d, tf), lambda f: (0, f)),
            pl.BlockSpec((tf, d), lambda f: (f, 0)),
            pl.BlockSpec((CONV_W, tf), lambda f: (0, f)),
            pl.BlockSpec((1, tf), lambda f: (0, f)),
            pl.BlockSpec((n, d), lambda f: (0, 0)),
            pl.BlockSpec((1, d), lambda f: (0, 0)),
        ],
        out_specs=[pl.BlockSpec((n, d), lambda f: (0, 0)), pl.BlockSpec((keep, tf), lambda f: (0, f))],
        out_shape=[jax.ShapeDtypeStruct((n, d), F32), jax.ShapeDtypeStruct((keep, ff), F32)],
        scratch_shapes=[pltpu.VMEM((n, d), F32), pltpu.VMEM((keep + n, tf), F32)],
        compiler_params=_params("arbitrary"),
        name="ffn_sample",
    )(h2, state, wg, wu, wd, conv_w, conv_b, x1, final_norm)


def _largest_divisor(n, candidates):
    for c in candidates:
        if n % c == 0:
            return c
    raise ValueError(f"no tile for {n} among {candidates}")


def kernel(x_prompt, x_sample, cache_dsa_k, cache_dsa_v, cache_idx_k, cache_fox_k, cache_fox_v, cache_fox_logf, state_ffn_conv, page_table, attn_norm, w_in, b_forget, w_branch_a, w_branch_b, w_out, ffn_norm, w_gate, w_up, w_down, conv_w, conv_b, final_norm):
    b, s, d = x_prompt.shape
    bd, t_new, _ = x_sample.shape
    depth = attn_norm.shape[0]
    assert depth == 1 and t_new * GROUP == SUBLANES and s % Q_BLOCK == 0
    n_pages = page_table.shape[1]
    past = n_pages * PAGE
    n_pool = cache_dsa_k.shape[1]
    ff = w_gate.shape[2]
    lay = _Layout(d)

    w_perm = _prep_w_in_t(jnp.swapaxes(w_in[0], 0, 1), lay)
    bf_row = jnp.zeros((1, LANES), F32).at[0, LOGF_LANE:LOGF_LANE + N_HEADS].set(b_forget[0])
    wa, wb, wo = (w[0].astype(BF16) for w in (w_branch_a, w_branch_b, w_out))
    wg, wu, wd = (w[0].astype(BF16) for w in (w_gate, w_up, w_down))
    g_attn, g_ffn, g_fin = attn_norm[0][None, :], ffn_norm[0][None, :], final_norm[None, :]
    cw, cb = conv_w[0], conv_b[0][None, :]

    tm_p = _largest_divisor(s, (1024, 512, 256, 128))
    tab_p = _rope_tables(jnp.arange(s, dtype=I32))
    tab_s = jnp.tile(_rope_tables(past + jnp.arange(t_new, dtype=I32)), (bd, 1))
    xp2 = x_prompt.reshape(b * s, d)
    xs2 = x_sample.reshape(bd * t_new, d)
    proj_p, *kv_p = _norm_proj(xp2, g_attn, w_perm, tab_p, bf_row, lay, tm_p)
    proj_s, *kv_s = _norm_proj(xs2, g_attn, w_perm, tab_s, bf_row, lay, bd * t_new)

    oa_p = _dsa_prompt(proj_p, kv_p[0], kv_p[1], b, s, lay)
    ob_p = _fox_prompt(proj_p, kv_p[2], kv_p[3], b, s, lay)

    def cols(name, width):
        o = getattr(lay, name)
        return proj_s[:, o:o + width].reshape(bd, t_new, width)

    def heads_major(x):
        x = x.reshape(bd, t_new, N_KV, GROUP, HEAD_DIM).transpose(0, 2, 3, 1, 4)
        return x.reshape(bd, Q_ROWS, HEAD_DIM)

    def new_kv(x):
        x = x.reshape(bd, t_new, N_KV, HEAD_DIM).transpose(0, 2, 1, 3)
        return jnp.pad(x, ((0, 0), (0, 0), (0, LANES - t_new), (0, 0)))

    def heads_back(o):
        o = o.reshape(bd, N_KV, GROUP, t_new, HEAD_DIM).transpose(0, 3, 1, 2, 4)
        return o.reshape(bd * t_new, N_HEADS * HEAD_DIM)

    ka_s, va_s, kb_s, vb_s = (x.reshape(bd, t_new, N_KV * HEAD_DIM) for x in kv_s)
    ik_s = cols("ik", IDX_DIM)
    iwf_s = cols("iwf", LANES)
    logf_s = iwf_s[..., LOGF_LANE:LOGF_LANE + N_HEADS]
    iq_s = cols("iq", 1024).reshape(bd, t_new, IDX_HEADS, IDX_DIM).transpose(0, 2, 1, 3)
    iq2 = jnp.broadcast_to(iq_s[:, :, None], (bd, IDX_HEADS, GROUP, t_new, IDX_DIM)).reshape(bd, IDX_HEADS * SUBLANES, IDX_DIM)
    iw_s = iwf_s[..., IW_LANE:IW_LANE + IDX_HEADS].transpose(0, 2, 1)
    iw2 = jnp.broadcast_to(iw_s[:, :, None], (bd, IDX_HEADS, GROUP, t_new)).reshape(bd, IDX_HEADS * SUBLANES, 1)
    ikt_new = jnp.pad(ik_s.transpose(0, 2, 1), ((0, 0), (0, 0), (0, LANES - t_new)))
    lft_new = jnp.pad(logf_s.transpose(0, 2, 1), ((0, 0), (0, 0), (0, LANES - t_new)))

    ik_cache_t = jnp.swapaxes(cache_idx_k[0], 1, 2)
    lf_cache_t = jnp.swapaxes(cache_fox_logf[0], 1, 2)
    kv_rows = lambda c: c.reshape(n_pool, PAGE * N_KV, HEAD_DIM)

    keys_past, keys_new, thr = _dsa_sample_keys(page_table, iq2, iw2, ikt_new, ik_cache_t, t_new)
    oa_s = heads_back(_dsa_sample_attend(page_table, heads_major(cols("qa", 1024)), keys_past, keys_new, thr,
                                         new_kv(ka_s), new_kv(va_s), kv_rows(cache_dsa_k), kv_rows(cache_dsa_v)))
    ob_s = heads_back(_fox_sample(page_table, heads_major(cols("qb", 1024)), lft_new, new_kv(kb_s), new_kv(vb_s),
                                  lf_cache_t, kv_rows(cache_fox_k), kv_rows(cache_fox_v), t_new))

    tm_m = _largest_divisor(s, (256, 128))
    x1_p, h2_p = _merge(xp2, oa_p, ob_p, proj_p, wa, wb, wo, g_ffn, lay, tm_m)
    x1_s, h2_s = _merge(xs2, oa_s, ob_s, proj_s, wa, wb, wo, g_ffn, lay, bd * t_new)

    tf = _largest_divisor(ff, (512, 256, 128))
    tm_f = _largest_divisor(s, (512, 256, 128))
    y_p, tails = _ffn_prompt(h2_p, x1_p, wg, wu, wd, cw, cb, g_fin, s, tm_f, tf)
    conv_p = tails.reshape(b, s // tm_f, SUBLANES, ff)[:, -1, SUBLANES - (CONV_W - 1):, :]

    t_major = lambda x: x.reshape(bd, t_new, -1).transpose(1, 0, 2).reshape(t_new * bd, -1)
    state_t = state_ffn_conv[0].transpose(1, 0, 2).reshape((CONV_W - 1) * bd, ff)
    y_s_t, st_t = _ffn_sample(t_major(h2_s), t_major(x1_s), state_t, wg, wu, wd, cw, cb, g_fin, bd, tf)
    y_s = y_s_t.reshape(t_new, bd, d).transpose(1, 0, 2)
    conv_s = st_t.reshape(CONV_W - 1, bd, ff).transpose(1, 0, 2)

    def pcols(name, width):
        o = getattr(lay, name)
        return proj_p[:, o:o + width]

    p_kv = [x.reshape(1, b, s, N_KV, HEAD_DIM) for x in kv_p]
    p_out = (p_kv[0], p_kv[1], pcols("ik", IDX_DIM).reshape(1, b, s, IDX_DIM), p_kv[2], p_kv[3],
             proj_p[:, lay.iwf + LOGF_LANE:lay.iwf + LOGF_LANE + N_HEADS].reshape(1, b, s, N_HEADS),
             conv_p[None])
    s_out = (ka_s.reshape(1, bd, t_new, N_KV, HEAD_DIM), va_s.reshape(1, bd, t_new, N_KV, HEAD_DIM), ik_s[None],
             kb_s.reshape(1, bd, t_new, N_KV, HEAD_DIM), vb_s.reshape(1, bd, t_new, N_KV, HEAD_DIM), logf_s[None],
             conv_s[None])
    return (y_p.reshape(b, s, d), y_s) + p_out + s_out
```

```python
import functools

import numpy as np
import jax
import jax.numpy as jnp
from jax import lax
from jax.experimental import pallas as pl
from jax.experimental.pallas import tpu as pltpu

F32 = jnp.float32
BF16 = jnp.bfloat16
I32 = jnp.int32

HEAD_DIM = 128
N_HEADS = 8
N_KV = 4
GROUP = N_HEADS // N_KV
IDX_HEADS = 16
IDX_DIM = 64
TOPK_MAX = 256
ROPE_THETA = 500000.0
ROT_DIM = HEAD_DIM // 4
IDX_ROT_DIM = IDX_DIM // 4
PAGE = 128
Q_BLOCK = 128
CONV_W = 3
RMS_EPS = 1e-6
ATT_SCALE = HEAD_DIM ** -0.5
IDX_SCALE = (IDX_HEADS * IDX_DIM) ** -0.5

LANES = 128
SUBLANES = 8
NEG = -1e30
NEG_INF = float("-inf")
INT_MIN = -2 ** 31
VMEM_LIMIT = 56 * 1024 * 1024

IW_LANE = 0
LOGF_LANE = IDX_HEADS

KV_NAMES = ("ka", "va", "kb", "vb")

NT_DIMS = (((1,), (1,)), ((), ()))


def _params(*sem):
    return pltpu.CompilerParams(dimension_semantics=sem, vmem_limit_bytes=VMEM_LIMIT)


def _nt(a, b):
    return lax.dot_general(a, b, NT_DIMS, preferred_element_type=F32)


def _rms(x, g):
    ms = jnp.mean(x * x, axis=-1, keepdims=True)
    return (x * lax.rsqrt(ms + RMS_EPS)) * g


class _Layout:
    def __init__(self, d_model):
        self.d = d_model
        self.tn = 512
        off = 0
        for name, size in (("ga", d_model), ("gb", d_model), ("qa", 1024), ("iq", 1024), ("qb", 1024)):
            assert off % size == 0, (name, off, size)
            setattr(self, name, off)
            off += size
        assert off % self.tn == 0
        self.proj_cols = off
        self.ik, self.iwf = off, off + LANES
        self.small_ik, self.small_iwf = 0, LANES
        off += self.tn
        for name in KV_NAMES:
            setattr(self, name, off)
            off += N_KV * HEAD_DIM
        assert N_KV * HEAD_DIM == self.tn
        self.nc = off

    def chunk_kinds(self):
        kinds = ["plain"] * (self.nc // LANES)
        for name, size, kind in (("qa", 1024, "qrope"), ("qb", 1024, "q"), ("iq", 1024, "rope64"),
                                 ("ik", self.tn, "s:plain"), ("ik", LANES, "s:rope64"), ("iwf", LANES, "s:iwf"),
                                 ("ka", 512, "kv0rope"), ("va", 512, "kv1"), ("kb", 512, "kv2"), ("vb", 512, "kv3")):
            start = getattr(self, name) // LANES
            for c in range(size // LANES):
                kinds[start + c] = kind
        return kinds


def _w_in_plan(lay):
    d = lay.d
    sizes = (1024, 512, 512, 1024, IDX_DIM, IDX_HEADS, 1024, 512, 512, N_HEADS, d, d)
    names = ("qa", "ka", "va", "iq", "ik", "iw", "qb", "kb", "vb", "fl", "ga", "gb")
    offs = np.concatenate([[0], np.cumsum(sizes)])
    src = {n: int(offs[k]) for k, n in enumerate(names)}
    plan = [()] * (lay.nc // LANES)
    for name, size in (("ga", d), ("gb", d), ("qa", 1024), ("iq", 1024), ("qb", 1024),
                       ("ka", 512), ("va", 512), ("kb", 512), ("vb", 512)):
        for c in range(size // LANES):
            plan[getattr(lay, name) // LANES + c] = ((src[name] + c * LANES, 0, LANES),)
    plan[lay.ik // LANES] = ((src["ik"], 0, IDX_DIM),)
    plan[lay.iwf // LANES] = ((src["iw"], IW_LANE, IDX_HEADS), (src["fl"], LOGF_LANE, N_HEADS))
    return tuple(plan), int(offs[-1])


def _prep_w_body(w_ref, o_ref, *, plan):
    cols = w_ref.shape[1]
    for c, pieces in enumerate(plan):
        parts, pos = [], 0
        for first, at, height in pieces:
            if at > pos:
                parts.append(jnp.zeros((at - pos, cols), F32))
            parts.append(w_ref[first:first + height, :])
            pos = at + height
        if pos < LANES:
            parts.append(jnp.zeros((LANES - pos, cols), F32))
        chunk = parts[0] if len(parts) == 1 else jnp.concatenate(parts, axis=0)
        o_ref[c * LANES:(c + 1) * LANES, :] = chunk.astype(BF16)


def _prep_w_in_t(w_in_t, lay):
    n_src, d = w_in_t.shape
    plan, n_cols = _w_in_plan(lay)
    assert n_cols == n_src and all(f % SUBLANES == 0 and a % SUBLANES == 0 for p in plan for f, a, _ in p)
    tc = _largest_divisor(d, (256, 128))
    return pl.pallas_call(
        functools.partial(_prep_w_body, plan=plan),
        grid=(d // tc,),
        in_specs=[pl.BlockSpec((n_src, tc), lambda i: (0, i))],
        out_specs=pl.BlockSpec((lay.nc, tc), lambda i: (0, i)),
        out_shape=jax.ShapeDtypeStruct((lay.nc, d), BF16),
        compiler_params=_params("arbitrary"),
        name="prep_w_in",
    )(w_in_t)


def _rope_tables(pos):
    def one(rot_dim, period):
        half = rot_dim // 2
        inv_freq = jnp.power(ROPE_THETA, -jnp.arange(half, dtype=F32) * (2.0 / rot_dim))
        ang = pos.astype(F32)[:, None] * inv_freq[None, :]
        cos, sin = jnp.cos(ang), jnp.sin(ang)
        n = pos.shape[0]
        c = jnp.concatenate([cos, cos, jnp.ones((n, period - rot_dim), F32)], axis=1)
        sa = jnp.concatenate([-sin, jnp.zeros((n, period - half), F32)], axis=1)
        sb = jnp.concatenate([jnp.zeros((n, half), F32), sin, jnp.zeros((n, period - rot_dim), F32)], axis=1)
        rep = LANES // period
        return [jnp.tile(t, (1, rep)) for t in (c, sa, sb)]
    return jnp.concatenate(one(ROT_DIM, HEAD_DIM) + one(IDX_ROT_DIM, IDX_DIM), axis=1)


def _proj_body(x_ref, g_ref, w_ref, tab_ref, bf_ref, o_ref, small_ref, *rest, tile_kinds):
    kv_refs, h_ref = rest[:len(KV_NAMES)], rest[len(KV_NAMES)]
    j = pl.program_id(1)
    tm = x_ref.shape[0]

    @pl.when(j == 0)
    def _():
        h_ref[...] = _rms(x_ref[...], g_ref[...]).astype(BF16)

    acc = _nt(h_ref[...], w_ref[...])

    def rope(a, base, half):
        c = tab_ref[:, base:base + LANES]
        sa = tab_ref[:, base + LANES:base + 2 * LANES]
        sb = tab_ref[:, base + 2 * LANES:base + 3 * LANES]
        return a * c + pltpu.roll(a, LANES - half, 1) * sa + pltpu.roll(a, half, 1) * sb

    def indexer_weight_and_log_forget(a):
        z = a + bf_ref[...]
        ls = jnp.minimum(z, 0.0) - jnp.log1p(jnp.exp(-jnp.abs(z)))
        lane = lax.broadcasted_iota(I32, a.shape, 1)
        return jnp.where(lane < LOGF_LANE, a * IDX_SCALE, jnp.where(lane < LOGF_LANE + N_HEADS, ls, a))

    def emit(kinds):
        for c, kind in enumerate(kinds):
            a = acc[:, c * LANES:(c + 1) * LANES]
            to_small = kind.startswith("s:")
            kind = kind[2:] if to_small else kind
            if kind == "qrope":
                a = rope(a, 0, ROT_DIM // 2) * ATT_SCALE
            elif kind == "q":
                a = a * ATT_SCALE
            elif kind == "rope64":
                a = rope(a, 3 * LANES, IDX_ROT_DIM // 2)
            elif kind == "iwf":
                a = indexer_weight_and_log_forget(a)
            if kind.startswith("kv"):
                if kind.endswith("rope"):
                    a = rope(a, 0, ROT_DIM // 2)
                kv_refs[int(kind[2])][pl.ds(c, tm, stride=N_KV), :] = a
            elif to_small:
                small_ref[:, c * LANES:(c + 1) * LANES] = a
            else:
                o_ref[:, c * LANES:(c + 1) * LANES] = a.astype(BF16)

    groups = {}
    for t, kinds in enumerate(tile_kinds):
        groups.setdefault(kinds, []).append(t)
    for kinds, tiles in groups.items():
        cond = functools.reduce(jnp.logical_or, [j == t for t in tiles])
        if all(k == "plain" for k in kinds):
            @pl.when(cond)
            def _():
                o_ref[...] = acc.astype(BF16)
        else:
            pl.when(cond)(functools.partial(emit, kinds))


def _norm_proj(x2d, gamma, w_perm, tab, bf_row, lay, tm):
    n, d = x2d.shape
    tn = lay.tn
    kinds = lay.chunk_kinds()
    per = tn // LANES
    tile_kinds = tuple(tuple(kinds[t * per:(t + 1) * per]) for t in range(lay.nc // tn))
    tab_blocks = tab.shape[0] // tm
    last_proj_tile = lay.proj_cols // tn - 1
    kv_spec = pl.BlockSpec((tm * N_KV, HEAD_DIM), lambda i, j: (i, 0))
    return pl.pallas_call(
        functools.partial(_proj_body, tile_kinds=tile_kinds),
        grid=(n // tm, lay.nc // tn),
        in_specs=[
            pl.BlockSpec((tm, d), lambda i, j: (i, 0), pipeline_mode=pl.Buffered(1)),
            pl.BlockSpec((1, d), lambda i, j: (0, 0)),
            pl.BlockSpec((tn, d), lambda i, j: (j, 0)),
            pl.BlockSpec((tm, 6 * LANES), lambda i, j: (i % tab_blocks, 0)),
            pl.BlockSpec((1, LANES), lambda i, j: (0, 0)),
        ],
        out_specs=[pl.BlockSpec((tm, tn), lambda i, j: (i, jnp.minimum(j, last_proj_tile))),
                   pl.BlockSpec((tm, tn), lambda i, j: (i, 0))] + [kv_spec] * len(KV_NAMES),
        out_shape=[jax.ShapeDtypeStruct((n, lay.proj_cols), BF16), jax.ShapeDtypeStruct((n, tn), F32)]
        + [jax.ShapeDtypeStruct((n * N_KV, HEAD_DIM), F32)] * len(KV_NAMES),
        scratch_shapes=[pltpu.VMEM((tm, d), BF16)],
        compiler_params=_params("arbitrary", "arbitrary"),
        name="norm_proj",
    )(x2d, gamma, w_perm, tab, bf_row)


def _code_to_float(code):
    bits = jnp.where(code < 0, code ^ jnp.int32(0x7FFFFFFF), code)
    return pltpu.bitcast(bits, F32)


def _kth_largest(read_scores, rows, k):
    def body(it, code):
        cand = code + jnp.left_shift(jnp.int32(1), 31 - it)
        cnt = jnp.sum(jnp.where(read_scores() >= _code_to_float(cand), 1.0, 0.0), axis=1, keepdims=True)
        return jnp.where(cnt >= k, cand, code)

    code = lax.fori_loop(0, 32, body, jnp.full((rows, 1), INT_MIN, I32))
    return _code_to_float(code), code == INT_MIN


def _count(mask):
    return jnp.sum(jnp.where(mask, 1.0, 0.0), axis=1, keepdims=True)


def _total(mask):
    ones = jnp.where(mask, 1.0, 0.0)
    return jnp.sum(jnp.sum(ones, axis=0, keepdims=True), axis=1, keepdims=True)[0, 0]


def _earlier_in_chunk():
    r = lax.broadcasted_iota(I32, (LANES, LANES), 0)
    c = lax.broadcasted_iota(I32, (LANES, LANES), 1)
    return jnp.where(r < c, 1.0, 0.0).astype(BF16)


def _tied_keys_to_keep(eq, seen, need, earlier):
    ones = jnp.where(eq, 1.0, 0.0)
    rank = seen + jnp.dot(ones.astype(BF16), earlier, preferred_element_type=F32)
    return eq & (rank < need), seen + jnp.sum(ones, axis=1, keepdims=True)


KEY_BUCKET = 512


def _key_limits(s):
    step = min(KEY_BUCKET, s)
    assert s % step == 0
    return tuple(range(step, s + 1, step))


def _for_causal_limit(i, limits, block_fn):
    q_end = (i + 1) * Q_BLOCK
    prev = 0
    for lim in limits:
        pl.when((q_end > prev) & (q_end <= lim))(functools.partial(block_fn, lim))
        prev = lim


def _load_kv_heads(src_ref, dst_ref):
    tokens = dst_ref.shape[0]
    for kh in range(N_KV):
        dst_ref[:, kh * HEAD_DIM:(kh + 1) * HEAD_DIM] = src_ref[pl.ds(kh, tokens, stride=N_KV), :].astype(BF16)


def _attend_heads(q_ref, kb_ref, vb_ref, o_ref, n, logit_bias):
    for kh in range(N_KV):
        kk = kb_ref[0:n, kh * HEAD_DIM:(kh + 1) * HEAD_DIM]
        vv = vb_ref[0:n, kh * HEAD_DIM:(kh + 1) * HEAD_DIM]
        for g in range(GROUP):
            h = kh * GROUP + g
            q = q_ref[:, h * HEAD_DIM:(h + 1) * HEAD_DIM]
            lg = _nt(q, kk) + logit_bias(h)
            m = jnp.max(lg, axis=1, keepdims=True)
            e = jnp.exp(lg - m)
            l = jnp.sum(e, axis=1, keepdims=True)
            o = jnp.dot(e.astype(BF16), vv, preferred_element_type=F32)
            o_ref[:, h * HEAD_DIM:(h + 1) * HEAD_DIM] = o / l


def _dsa_prompt_body(iq_ref, iwf_ref, ik_ref, q_ref, k_ref, v_ref, o_ref,
                     ikb_ref, kb_ref, vb_ref, sc_ref, bias_ref, *, topk, limits):
    i = pl.program_id(1)

    @pl.when(i == 0)
    def _():
        ikb_ref[...] = ik_ref[:, :IDX_DIM].astype(BF16)
        _load_kv_heads(k_ref, kb_ref)
        _load_kv_heads(v_ref, vb_ref)

    def block(n):
        ikb = ikb_ref[0:n, :]
        for h in range(IDX_HEADS):
            qh = iq_ref[:, h * IDX_DIM:(h + 1) * IDX_DIM]
            s = jnp.maximum(_nt(qh, ikb), 0.0) * iwf_ref[:, IW_LANE + h:IW_LANE + h + 1]
            if h == 0:
                sc_ref[:, 0:n] = s
            else:
                sc_ref[:, 0:n] += s
        row = i * Q_BLOCK + lax.broadcasted_iota(I32, (Q_BLOCK, n), 0)
        col = lax.broadcasted_iota(I32, (Q_BLOCK, n), 1)
        causal = col <= row
        sc_ref[:, 0:n] = jnp.where(causal, sc_ref[:, 0:n], NEG_INF)
        thr, none = _kth_largest(lambda: sc_ref[:, 0:n], Q_BLOCK, topk)
        chosen = ((sc_ref[:, 0:n] >= thr) | none) & causal
        bias_ref[:, 0:n] = jnp.where(chosen, 0.0, NEG)

        first = i * Q_BLOCK
        short = jnp.clip(topk - first, 0, Q_BLOCK)
        expected = short * first + (short * (short + 1)) // 2 + (Q_BLOCK - short) * topk

        @pl.when(_total(chosen) > expected.astype(F32))
        def _():
            need = topk - _count((sc_ref[:, 0:n] > thr) & causal)
            earlier = _earlier_in_chunk()
            seen = jnp.zeros((Q_BLOCK, 1), F32)
            for c in range(n // LANES):
                lanes = slice(c * LANES, (c + 1) * LANES)
                sc = sc_ref[:, lanes]
                ok = (c * LANES + lax.broadcasted_iota(I32, (Q_BLOCK, LANES), 1)
                      <= first + lax.broadcasted_iota(I32, (Q_BLOCK, LANES), 0))
                keep, seen = _tied_keys_to_keep((sc == thr) & ok, seen, need, earlier)
                bias_ref[:, lanes] = jnp.where((((sc > thr) | none) & ok) | keep, 0.0, NEG)
        _attend_heads(q_ref, kb_ref, vb_ref, o_ref, n, lambda h: bias_ref[:, 0:n])

    _for_causal_limit(i, limits, block)


def _dsa_prompt(proj, small, k4, v4, b, s, lay):
    nb = s // Q_BLOCK
    topk = min(TOPK_MAX, s // 4)
    row = lambda bb, i: bb * nb + i
    return pl.pallas_call(
        functools.partial(_dsa_prompt_body, topk=topk, limits=_key_limits(s)),
        grid=(b, nb),
        in_specs=[
            pl.BlockSpec((Q_BLOCK, 1024), lambda bb, i: (row(bb, i), lay.iq // 1024)),
            pl.BlockSpec((Q_BLOCK, LANES), lambda bb, i: (row(bb, i), lay.small_iwf // LANES)),
            pl.BlockSpec((s, LANES), lambda bb, i: (bb, lay.small_ik // LANES)),
            pl.BlockSpec((Q_BLOCK, 1024), lambda bb, i: (row(bb, i), lay.qa // 1024)),
            pl.BlockSpec((s * N_KV, HEAD_DIM), lambda bb, i: (bb, 0)),
            pl.BlockSpec((s * N_KV, HEAD_DIM), lambda bb, i: (bb, 0)),
        ],
        out_specs=pl.BlockSpec((Q_BLOCK, 1024), lambda bb, i: (row(bb, i), 0)),
        out_shape=jax.ShapeDtypeStruct((b * s, 1024), F32),
        scratch_shapes=[pltpu.VMEM((s, IDX_DIM), BF16), pltpu.VMEM((s, 512), BF16), pltpu.VMEM((s, 512), BF16),
                        pltpu.VMEM((Q_BLOCK, s), F32), pltpu.VMEM((Q_BLOCK, s), F32)],
        compiler_params=_params("arbitrary", "arbitrary"),
        name="dsa_prompt",
    )(proj, small, small, proj, k4, v4)


CUM_BLOCK = 256


def _fox_prompt_body(q_ref, k_ref, v_ref, lf_ref, o_ref, kb_ref, vb_ref, c_ref, ct_ref, bias_ref, *, limits):
    i = pl.program_id(1)
    s_len = kb_ref.shape[0]

    @pl.when(i == 0)
    def _():
        _load_kv_heads(k_ref, kb_ref)
        _load_kv_heads(v_ref, vb_ref)
        r = lax.broadcasted_iota(I32, (CUM_BLOCK, CUM_BLOCK), 0)
        c = lax.broadcasted_iota(I32, (CUM_BLOCK, CUM_BLOCK), 1)
        tri = jnp.where(c <= r, 1.0, 0.0).astype(F32)
        carry = jnp.zeros((1, LANES), F32)
        for blk in range(s_len // CUM_BLOCK):
            xb = lf_ref[blk * CUM_BLOCK:(blk + 1) * CUM_BLOCK, :]
            cb = jnp.dot(tri, xb, precision=lax.Precision.HIGHEST, preferred_element_type=F32) + carry
            c_ref[blk * CUM_BLOCK:(blk + 1) * CUM_BLOCK, :] = cb
            carry = cb[CUM_BLOCK - 1:CUM_BLOCK, :]
        ct_ref[...] = c_ref[...].T

    start = pl.multiple_of(i * Q_BLOCK, Q_BLOCK)

    def block(n):
        row = i * Q_BLOCK + lax.broadcasted_iota(I32, (Q_BLOCK, n), 0)
        col = lax.broadcasted_iota(I32, (Q_BLOCK, n), 1)
        bias_ref[:, 0:n] = jnp.where(col <= row, 0.0, NEG)

        def logit_bias(h):
            cq = c_ref[pl.ds(start, Q_BLOCK), LOGF_LANE + h:LOGF_LANE + h + 1]
            ck = ct_ref[LOGF_LANE + h:LOGF_LANE + h + 1, 0:n]
            return (cq - ck) + bias_ref[:, 0:n]

        _attend_heads(q_ref, kb_ref, vb_ref, o_ref, n, logit_bias)

    _for_causal_limit(i, limits, block)


def _fox_prompt(proj, small, k4, v4, b, s, lay):
    nb = s // Q_BLOCK
    assert s % CUM_BLOCK == 0
    row = lambda bb, i: bb * nb + i
    return pl.pallas_call(
        functools.partial(_fox_prompt_body, limits=_key_limits(s)),
        grid=(b, nb),
        in_specs=[
            pl.BlockSpec((Q_BLOCK, 1024), lambda bb, i: (row(bb, i), lay.qb // 1024)),
            pl.BlockSpec((s * N_KV, HEAD_DIM), lambda bb, i: (bb, 0)),
            pl.BlockSpec((s * N_KV, HEAD_DIM), lambda bb, i: (bb, 0)),
            pl.BlockSpec((s, LANES), lambda bb, i: (bb, lay.small_iwf // LANES)),
        ],
        out_specs=pl.BlockSpec((Q_BLOCK, 1024), lambda bb, i: (row(bb, i), 0)),
        out_shape=jax.ShapeDtypeStruct((b * s, 1024), F32),
        scratch_shapes=[pltpu.VMEM((s, 512), BF16), pltpu.VMEM((s, 512), BF16),
                        pltpu.VMEM((s, LANES), F32), pltpu.VMEM((LANES, s), F32), pltpu.VMEM((Q_BLOCK, s), F32)],
        compiler_params=_params("arbitrary", "arbitrary"),
        name="fox_prompt",
    )(proj, k4, v4, small)


Q_ROWS = N_KV * SUBLANES


def _pages_per_step(n_pages, most=16):
    for pp in (32, 16, 8, 4, 2, 1):
        if pp <= most and n_pages % pp == 0:
            return pp


def _page_specs(block, pp, page_of):
    def spec(r):
        return pl.BlockSpec((None,) + block, lambda bb, c, pt: (pt[bb, page_of(c, r)],) + (0,) * len(block))
    return [spec(r) for r in range(pp)]


def _per_seq(shape):
    return pl.BlockSpec((None,) + shape, lambda bb, c, pt: (bb,) + (0,) * len(shape))


def _shared(shape):
    return pl.BlockSpec(shape, lambda bb, c, pt: (0,) * len(shape))


def _kv_rows(pages, kh):
    return jnp.concatenate([p[pl.ds(kh, PAGE, stride=N_KV), :] for p in pages], axis=0).astype(BF16)


def _softmax_update(q_ref, keys_of, values_of, bias, sel, m_ref, l_ref, acc_ref):
    lg = jnp.concatenate([_nt(_q_rows(q_ref, kh), keys_of(kh)) for kh in range(N_KV)], axis=0)
    if bias is not None:
        lg = lg + bias
    if sel is not None:
        lg = jnp.where(sel, lg, NEG)
    m_old = m_ref[...]
    m_new = jnp.maximum(m_old, jnp.max(lg, axis=1, keepdims=True))
    corr = jnp.exp(m_old - m_new)
    e = jnp.exp(lg - m_new)
    if sel is not None:
        e = jnp.where(sel, e, 0.0)
    l_ref[...] = l_ref[...] * corr + jnp.sum(e, axis=1, keepdims=True)
    pv = jnp.concatenate([jnp.dot(e[kh * SUBLANES:(kh + 1) * SUBLANES].astype(BF16), values_of(kh),
                                  preferred_element_type=F32) for kh in range(N_KV)], axis=0)
    acc_ref[...] = acc_ref[...] * corr + pv
    m_ref[...] = m_new


def _softmax_init(m_ref, l_ref, acc_ref):
    m_ref[...] = jnp.full(m_ref.shape, NEG, F32)
    l_ref[...] = jnp.zeros(l_ref.shape, F32)
    acc_ref[...] = jnp.zeros(acc_ref.shape, F32)


def _softmax_scratch():
    return [pltpu.VMEM((Q_ROWS, 1), F32), pltpu.VMEM((Q_ROWS, 1), F32), pltpu.VMEM((Q_ROWS, HEAD_DIM), F32)]


def _q_rows(q_ref, kh):
    return q_ref[kh * SUBLANES:(kh + 1) * SUBLANES, :].astype(BF16)


def _dsa_sample_keys_body(pt_ref, iq_ref, iw_ref, ikn_ref, *rest, pp, n_new):
    pages, sc_ref = rest[:pp], rest[pp]
    c = pl.program_id(1)
    past = sc_ref.shape[1] - LANES
    q = iq_ref[...].astype(BF16)
    w = iw_ref[...]

    def scores_of(ikt):
        s = jnp.maximum(jnp.dot(q, ikt.astype(BF16), preferred_element_type=F32), 0.0) * w
        acc = s[0:SUBLANES]
        for h in range(1, IDX_HEADS):
            acc = acc + s[h * SUBLANES:(h + 1) * SUBLANES]
        return acc

    sc_ref[:, pl.ds(pl.multiple_of(c * (pp * PAGE), pp * PAGE), pp * PAGE)] = scores_of(
        jnp.concatenate([p[...] for p in pages], axis=1))

    @pl.when(c == 0)
    def _():
        row = lax.broadcasted_iota(I32, (SUBLANES, LANES), 0)
        lane = lax.broadcasted_iota(I32, (SUBLANES, LANES), 1)
        sc_ref[:, past:past + LANES] = jnp.where(lane <= row % n_new, scores_of(ikn_ref[...]), NEG_INF)


def _dsa_sample_keys(page_table, iq2, iw2, ikt_new, ik_cache_t, n_new):
    bd, n_pages = page_table.shape
    pp = _pages_per_step(n_pages, 32)
    past = n_pages * PAGE
    return pl.pallas_call(
        functools.partial(_dsa_sample_keys_body, pp=pp, n_new=n_new),
        grid_spec=pltpu.PrefetchScalarGridSpec(
            num_scalar_prefetch=1,
            grid=(bd, n_pages // pp),
            in_specs=[_per_seq((IDX_HEADS * SUBLANES, IDX_DIM)), _per_seq((IDX_HEADS * SUBLANES, 1)),
                      _per_seq((IDX_DIM, LANES))]
            + _page_specs((IDX_DIM, PAGE), pp, lambda c, r: c * pp + r),
            out_specs=pl.BlockSpec((SUBLANES, past + LANES), lambda bb, c, pt: (bb, 0)),
        ),
        out_shape=jax.ShapeDtypeStruct((bd * SUBLANES, past + LANES), F32),
        compiler_params=_params("arbitrary", "arbitrary"),
        name="dsa_sample_keys",
    )(page_table, iq2, iw2, ikt_new, *([ik_cache_t] * pp))


def _dsa_sample_select_body(sc_ref, sel_ref, thr_ref, *, topk):
    rows, n = sc_ref.shape
    thr, _ = _kth_largest(lambda: sc_ref[...], rows, topk)
    thr_ref[...] = jnp.broadcast_to(thr, thr_ref.shape)
    sel_ref[...] = sc_ref[...]

    @pl.when(jnp.max(_count(sc_ref[...] >= thr)) > topk)
    def _():
        need = topk - _count(sc_ref[...] > thr)
        earlier = _earlier_in_chunk()

        def chunk(c, seen):
            lanes = pl.ds(pl.multiple_of(c * LANES, LANES), LANES)
            sc = sc_ref[:, lanes]
            eq = sc == thr
            keep, seen = _tied_keys_to_keep(eq, seen, need, earlier)
            sel_ref[:, lanes] = jnp.where(eq & jnp.logical_not(keep), NEG_INF, sc)
            return seen

        lax.fori_loop(0, n // LANES, chunk, jnp.zeros((rows, 1), F32))


def _dsa_sample_select(scores, topk, seqs_per_step):
    rows_total, n = scores.shape
    rows = seqs_per_step * SUBLANES
    return pl.pallas_call(
        functools.partial(_dsa_sample_select_body, topk=topk),
        grid=(rows_total // rows,),
        in_specs=[pl.BlockSpec((rows, n), lambda i: (i, 0))],
        out_specs=[pl.BlockSpec((rows, n), lambda i: (i, 0)), pl.BlockSpec((rows, LANES), lambda i: (i, 0))],
        out_shape=[jax.ShapeDtypeStruct((rows_total, n), F32), jax.ShapeDtypeStruct((rows_total, LANES), F32)],
        compiler_params=_params("arbitrary"),
        name="dsa_sample_select",
    )(scores)


def _dsa_sample_attend_body(pt_ref, q_ref, kp_ref, kn_ref, thr_ref, knew_ref, vnew_ref, *rest, pp):
    k_pages, v_pages = rest[:pp], rest[pp:2 * pp]
    o_ref, m_ref, l_ref, acc_ref = rest[2 * pp:]
    c = pl.program_id(1)
    thr = jnp.concatenate([thr_ref[:, 0:1]] * N_KV, axis=0)

    def selected(scores):
        scores = jnp.concatenate([scores] * N_KV, axis=0)
        return (scores >= thr) & (scores > NEG_INF)

    @pl.when(c == 0)
    def _():
        _softmax_init(m_ref, l_ref, acc_ref)
        _softmax_update(q_ref, lambda kh: knew_ref[kh].astype(BF16), lambda kh: vnew_ref[kh].astype(BF16),
                        None, selected(kn_ref[...]), m_ref, l_ref, acc_ref)

    _softmax_update(q_ref, functools.partial(_kv_rows, k_pages), functools.partial(_kv_rows, v_pages),
                    None, selected(kp_ref[...]), m_ref, l_ref, acc_ref)

    @pl.when(c == pl.num_programs(1) - 1)
    def _():
        o_ref[...] = acc_ref[...] / l_ref[...]


def _dsa_sample_attend(page_table, q_s, scores, thr, k_new, v_new, k_cache, v_cache):
    bd, n_pages = page_table.shape
    pp = _pages_per_step(n_pages)
    page_of = lambda c, r: c * pp + r
    new_chunk = n_pages * PAGE // LANES
    kv_block = (PAGE * N_KV, HEAD_DIM)
    return pl.pallas_call(
        functools.partial(_dsa_sample_attend_body, pp=pp),
        grid_spec=pltpu.PrefetchScalarGridSpec(
            num_scalar_prefetch=1,
            grid=(bd, n_pages // pp),
            in_specs=[_per_seq((Q_ROWS, HEAD_DIM)),
                      pl.BlockSpec((SUBLANES, pp * PAGE), lambda bb, c, pt: (bb, c)),
                      pl.BlockSpec((SUBLANES, LANES), lambda bb, c, pt: (bb, new_chunk)),
                      pl.BlockSpec((SUBLANES, LANES), lambda bb, c, pt: (bb, 0)),
                      _per_seq((N_KV, LANES, HEAD_DIM)), _per_seq((N_KV, LANES, HEAD_DIM))]
            + _page_specs(kv_block, pp, page_of) + _page_specs(kv_block, pp, page_of),
            out_specs=_per_seq((Q_ROWS, HEAD_DIM)),
            scratch_shapes=_softmax_scratch(),
        ),
        out_shape=jax.ShapeDtypeStruct((bd, Q_ROWS, HEAD_DIM), F32),
        compiler_params=_params("arbitrary", "arbitrary"),
        name="dsa_sample_attend",
    )(page_table, q_s, scores, scores, thr, k_new, v_new, *([k_cache] * pp), *([v_cache] * pp))


def _fox_sample_body(pt_ref, q_ref, lfn_ref, knew_ref, vnew_ref, rep_ref, later_ref, *rest, pp, n_new):
    lf_pages, k_pages, v_pages = rest[:pp], rest[pp:2 * pp], rest[2 * pp:3 * pp]
    o_ref, m_ref, l_ref, acc_ref, cq_ref, carry_ref = rest[3 * pp:]
    c = pl.program_id(1)
    hi = lax.Precision.HIGHEST

    @pl.when(c == 0)
    def _():
        _softmax_init(m_ref, l_ref, acc_ref)
        carry_ref[...] = jnp.zeros(carry_ref.shape, F32)
        r_io = lax.broadcasted_iota(I32, (LANES, LANES), 0)
        c_io = lax.broadcasted_iota(I32, (LANES, LANES), 1)
        incl = jnp.where(r_io <= c_io, 1.0, 0.0).astype(F32)
        cum = jnp.dot(lfn_ref[...], incl, precision=hi, preferred_element_type=F32)
        cg = jnp.dot(rep_ref[0:Q_ROWS, 0:N_HEADS], cum, precision=hi, preferred_element_type=F32)
        row = lax.broadcasted_iota(I32, (Q_ROWS, LANES), 0)
        lane = lax.broadcasted_iota(I32, (Q_ROWS, LANES), 1)
        own = lane == row % n_new
        cq = jnp.sum(jnp.where(own, cg, 0.0), axis=1, keepdims=True)
        cq_ref[...] = cq
        _softmax_update(q_ref, lambda kh: knew_ref[kh].astype(BF16), lambda kh: vnew_ref[kh].astype(BF16),
                        cq - cg, lane <= row % n_new, m_ref, l_ref, acc_ref)

    lf_all = jnp.concatenate([p[...] for p in lf_pages], axis=0)
    lf_rows = jnp.dot(rep_ref[...], lf_all, precision=hi, preferred_element_type=F32)
    within = jnp.dot(lf_rows, later_ref[...], precision=hi, preferred_element_type=F32)
    total = within[:, 0:1] + lf_rows[:, 0:1]
    run = carry_ref[...]
    biases = []
    for r in range(pp):
        biases.append(within[r * Q_ROWS:(r + 1) * Q_ROWS] + (run + cq_ref[...]))
        run = run + total[r * Q_ROWS:(r + 1) * Q_ROWS]
    carry_ref[...] = run
    _softmax_update(q_ref, functools.partial(_kv_rows, k_pages), functools.partial(_kv_rows, v_pages),
                    jnp.concatenate(biases, axis=1), None, m_ref, l_ref, acc_ref)

    @pl.when(c == pl.num_programs(1) - 1)
    def _():
        o_ref[...] = acc_ref[...] / l_ref[...]


def _fox_sample(page_table, q_s, lft_new, k_new, v_new, lf_cache_t, k_cache, v_cache, n_new):
    bd, n_pages = page_table.shape
    pp = _pages_per_step(n_pages)
    page_of = lambda c, r: n_pages - 1 - (c * pp + r)
    kv_block = (PAGE * N_KV, HEAD_DIM)
    row_head = np.arange(Q_ROWS) // n_new
    rep_one = (row_head[:, None] == np.arange(N_HEADS)[None, :]).astype(np.float32)
    rep = jnp.asarray(np.kron(np.eye(pp, dtype=np.float32), rep_one))
    later = jnp.asarray((np.arange(PAGE)[:, None] > np.arange(PAGE)[None, :]).astype(np.float32))
    return pl.pallas_call(
        functools.partial(_fox_sample_body, pp=pp, n_new=n_new),
        grid_spec=pltpu.PrefetchScalarGridSpec(
            num_scalar_prefetch=1,
            grid=(bd, n_pages // pp),
            in_specs=[_per_seq((Q_ROWS, HEAD_DIM)), _per_seq((N_HEADS, LANES)),
                      _per_seq((N_KV, LANES, HEAD_DIM)), _per_seq((N_KV, LANES, HEAD_DIM)),
                      _shared((pp * Q_ROWS, pp * N_HEADS)), _shared((PAGE, PAGE))]
            + _page_specs((N_HEADS, PAGE), pp, page_of) + _page_specs(kv_block, pp, page_of)
            + _page_specs(kv_block, pp, page_of),
            out_specs=_per_seq((Q_ROWS, HEAD_DIM)),
            scratch_shapes=_softmax_scratch() + [pltpu.VMEM((Q_ROWS, 1), F32), pltpu.VMEM((Q_ROWS, 1), F32)],
        ),
        out_shape=jax.ShapeDtypeStruct((bd, Q_ROWS, HEAD_DIM), F32),
        compiler_params=_params("arbitrary", "arbitrary"),
        name="fox_sample",
    )(page_table, q_s, lft_new, k_new, v_new, rep, later, *([lf_cache_t] * pp), *([k_cache] * pp), *([v_cache] * pp))


def _merge_body(x_ref, oa_ref, ob_ref, ga_ref, gb_ref, wa_ref, wb_ref, wo_ref, gn_ref, x1_ref, h2_ref):
    a = jnp.dot(oa_ref[...].astype(BF16), wa_ref[...], preferred_element_type=F32)
    b = jnp.dot(ob_ref[...].astype(BF16), wb_ref[...], preferred_element_type=F32)
    merged = jax.nn.sigmoid(ga_ref[...].astype(F32)) * a + jax.nn.sigmoid(gb_ref[...].astype(F32)) * b
    x1 = x_ref[...] + jnp.dot(merged.astype(BF16), wo_ref[...], preferred_element_type=F32)
    x1_ref[...] = x1
    h2_ref[...] = _rms(x1, gn_ref[...]).astype(BF16)


def _merge(x2d, o_a, o_b, proj, wa, wb, wo, ffn_norm, lay, tm):
    n, d = x2d.shape
    const = lambda shape: pl.BlockSpec(shape, lambda i: (0, 0), pipeline_mode=pl.Buffered(1))
    return pl.pallas_call(
        _merge_body,
        grid=(n // tm,),
        in_specs=[
            pl.BlockSpec((tm, d), lambda i: (i, 0)),
            pl.BlockSpec((tm, 1024), lambda i: (i, 0)),
            pl.BlockSpec((tm, 1024), lambda i: (i, 0)),
            pl.BlockSpec((tm, d), lambda i: (i, lay.ga // d)),
            pl.BlockSpec((tm, d), lambda i: (i, lay.gb // d)),
            const((1024, d)), const((1024, d)), const((d, d)), const((1, d)),
        ],
        out_specs=[pl.BlockSpec((tm, d), lambda i: (i, 0)), pl.BlockSpec((tm, d), lambda i: (i, 0))],
        out_shape=[jax.ShapeDtypeStruct((n, d), F32), jax.ShapeDtypeStruct((n, d), BF16)],
        compiler_params=_params("arbitrary"),
        name="merge",
    )(x2d, o_a, o_b, proj, proj, wa, wb, wo, ffn_norm)


FFN_ROW_CHUNKS = 2


def _ffn_zero_acc(f, acc_ref):
    @pl.when(f == 0)
    def _():
        acc_ref[...] = jnp.zeros(acc_ref.shape, F32)


def _ffn_finish(f, n_f, x1_ref, fn_ref, y_ref, acc_ref):
    @pl.when(f == n_f - 1)
    def _():
        y_ref[...] = _rms(x1_ref[...] + acc_ref[...], fn_ref[...])


def _ffn_rows(h, first, rows, n_rows, taps, wg_ref, wu_ref, wd_ref, cw_ref, cb_ref, ext_ref, acc_ref):
    gp = jnp.dot(h, wg_ref[...], preferred_element_type=F32)
    up = jnp.dot(h, wu_ref[...], preferred_element_type=F32)
    ext_ref[first + rows:first + rows + n_rows, :] = gp
    conv = cb_ref[...]
    for j, back in enumerate(taps):
        src = gp if back == 0 else ext_ref[first + rows - back:first + rows - back + n_rows, :]
        conv = conv + cw_ref[j:j + 1, :] * src
    act = (conv * jax.nn.sigmoid(conv)) * up
    acc_ref[rows:rows + n_rows, :] += jnp.dot(act.astype(BF16), wd_ref[...], preferred_element_type=F32)


def _ffn_prompt_body(h_ref, halo_ref, wg_ref, wu_ref, wd_ref, cw_ref, cb_ref, x1_ref, fn_ref,
                     y_ref, tail_ref, acc_ref, ext_ref, *, tiles_per_seq):
    i = pl.program_id(0)
    f = pl.program_id(1)
    tm = h_ref.shape[0]
    _ffn_zero_acc(f, acc_ref)
    halo = jnp.dot(halo_ref[...], wg_ref[...], preferred_element_type=F32)
    ext_ref[0:SUBLANES, :] = jnp.where(i % tiles_per_seq == 0, 0.0, halo)
    taps = tuple(CONV_W - 1 - j for j in range(CONV_W))
    rc = tm // FFN_ROW_CHUNKS
    for c in range(FFN_ROW_CHUNKS):
        _ffn_rows(h_ref[c * rc:(c + 1) * rc, :], SUBLANES, c * rc, rc, taps,
                  wg_ref, wu_ref, wd_ref, cw_ref, cb_ref, ext_ref, acc_ref)
    tail_ref[...] = ext_ref[tm:tm + SUBLANES, :]
    _ffn_finish(f, pl.num_programs(1), x1_ref, fn_ref, y_ref, acc_ref)


def _ffn_prompt(h2, x1, wg, wu, wd, conv_w, conv_b, final_norm, s, tm, tf):
    n, d = x1.shape
    ff = wg.shape[1]
    assert s % tm == 0 and tm % SUBLANES == 0
    hb = tm // SUBLANES
    return pl.pallas_call(
        functools.partial(_ffn_prompt_body, tiles_per_seq=s // tm),
        grid=(n // tm, ff // tf),
        in_specs=[
            pl.BlockSpec((tm, d), lambda i, f: (i, 0)),
            pl.BlockSpec((SUBLANES, d), lambda i, f: (jnp.maximum(i * hb - 1, 0), 0)),
            pl.BlockSpec((d, tf), lambda i, f: (0, f)),
            pl.BlockSpec((d, tf), lambda i, f: (0, f)),
            pl.BlockSpec((tf, d), lambda i, f: (f, 0)),
            pl.BlockSpec((CONV_W, tf), lambda i, f: (0, f)),
            pl.BlockSpec((1, tf), lambda i, f: (0, f)),
            pl.BlockSpec((tm, d), lambda i, f: (i, 0)),
            pl.BlockSpec((1, d), lambda i, f: (0, 0)),
        ],
        out_specs=[pl.BlockSpec((tm, d), lambda i, f: (i, 0)),
                   pl.BlockSpec((SUBLANES, tf), lambda i, f: (i, f))],
        out_shape=[jax.ShapeDtypeStruct((n, d), F32), jax.ShapeDtypeStruct((n // tm * SUBLANES, ff), F32)],
        scratch_shapes=[pltpu.VMEM((tm, d), F32), pltpu.VMEM((tm + SUBLANES, tf), F32)],
        compiler_params=_params("arbitrary", "arbitrary"),
        name="ffn_prompt",
    )(h2, h2, wg, wu, wd, conv_w, conv_b, x1, final_norm)


def _ffn_sample_body(h_ref, st_ref, wg_ref, wu_ref, wd_ref, cw_ref, cb_ref, x1_ref, fn_ref,
                     y_ref, new_st_ref, acc_ref, ext_ref, *, bd):
    f = pl.program_id(0)
    n = h_ref.shape[0]
    keep = (CONV_W - 1) * bd
    _ffn_zero_acc(f, acc_ref)
    ext_ref[0:keep, :] = st_ref[...]
    taps = tuple((CONV_W - 1 - j) * bd for j in range(CONV_W))
    _ffn_rows(h_ref[...], keep, 0, n, taps, wg_ref, wu_ref, wd_ref, cw_ref, cb_ref, ext_ref, acc_ref)
    new_st_ref[...] = ext_ref[n:n + keep, :]
    _ffn_finish(f, pl.num_programs(0), x1_ref, fn_ref, y_ref, acc_ref)


def _ffn_sample(h2, x1, state, wg, wu, wd, conv_w, conv_b, final_norm, bd, tf):
    n, d = x1.shape
    ff = wg.shape[1]
    keep = (CONV_W - 1) * bd
    assert bd % SUBLANES == 0 and n >= keep
    return pl.pallas_call(
        functools.partial(_ffn_sample_body, bd=bd),
        grid=(ff // tf,),
        in_specs=[
            pl.BlockSpec((n, d), lambda f: (0, 0)),
            pl.BlockSpec((keep, tf), lambda f: (0, f)),
            pl.BlockSpec((d, tf), lambda f: (0, f)),
            pl.BlockSpec((d, tf), lambda f: (0, f)),
            pl.BlockSpec((tf, d), lambda f: (f, 0)),
            pl.BlockSpec((CONV_W, tf), lambda f: (0, f)),
            pl.BlockSpec((1, tf), lambda f: (0, f)),
            pl.BlockSpec((n, d), lambda f: (0, 0)),
            pl.BlockSpec((1, d), lambda f: (0, 0)),
        ],
        out_specs=[pl.BlockSpec((n, d), lambda f: (0, 0)), pl.BlockSpec((keep, tf), lambda f: (0, f))],
        out_shape=[jax.ShapeDtypeStruct((n, d), F32), jax.ShapeDtypeStruct((keep, ff), F32)],
        scratch_shapes=[pltpu.VMEM((n, d), F32), pltpu.VMEM((keep + n, tf), F32)],
        compiler_params=_params("arbitrary"),
        name="ffn_sample",
    )(h2, state, wg, wu, wd, conv_w, conv_b, x1, final_norm)


def _largest_divisor(n, candidates):
    for c in candidates:
        if n % c == 0:
            return c
    raise ValueError(f"no tile for {n} among {candidates}")


def kernel(x_prompt, x_sample, cache_dsa_k, cache_dsa_v, cache_idx_k, cache_fox_k, cache_fox_v, cache_fox_logf, state_ffn_conv, page_table, attn_norm, w_in, b_forget, w_branch_a, w_branch_b, w_out, ffn_norm, w_gate, w_up, w_down, conv_w, conv_b, final_norm):
    b, s, d = x_prompt.shape
    bd, t_new, _ = x_sample.shape
    depth = attn_norm.shape[0]
    assert depth == 1 and t_new * GROUP == SUBLANES and s % Q_BLOCK == 0
    n_pages = page_table.shape[1]
    past = n_pages * PAGE
    n_pool = cache_dsa_k.shape[1]
    ff = w_gate.shape[2]
    lay = _Layout(d)

    w_perm = _prep_w_in_t(jnp.swapaxes(w_in[0], 0, 1), lay)
    bf_row = jnp.zeros((1, LANES), F32).at[0, LOGF_LANE:LOGF_LANE + N_HEADS].set(b_forget[0])
    wa, wb, wo = (w[0].astype(BF16) for w in (w_branch_a, w_branch_b, w_out))
    wg, wu, wd = (w[0].astype(BF16) for w in (w_gate, w_up, w_down))
    g_attn, g_ffn, g_fin = attn_norm[0][None, :], ffn_norm[0][None, :], final_norm[None, :]
    cw, cb = conv_w[0], conv_b[0][None, :]

    tm_p = _largest_divisor(s, (1024, 512, 256, 128))
    tab_p = _rope_tables(jnp.arange(s, dtype=I32))
    tab_s = jnp.tile(_rope_tables(past + jnp.arange(t_new, dtype=I32)), (bd, 1))
    xp2 = x_prompt.reshape(b * s, d)
    xs2 = x_sample.reshape(bd * t_new, d)
    proj_p, small_p, *kv_p = _norm_proj(xp2, g_attn, w_perm, tab_p, bf_row, lay, tm_p)
    proj_s, small_s, *kv_s = _norm_proj(xs2, g_attn, w_perm, tab_s, bf_row, lay, bd * t_new)

    oa_p = _dsa_prompt(proj_p, small_p, kv_p[0], kv_p[1], b, s, lay)
    ob_p = _fox_prompt(proj_p, small_p, kv_p[2], kv_p[3], b, s, lay)

    def cols(name, width):
        o = getattr(lay, name)
        return proj_s[:, o:o + width].astype(F32).reshape(bd, t_new, width)

    def heads_major(x):
        x = x.reshape(bd, t_new, N_KV, GROUP, HEAD_DIM).transpose(0, 2, 3, 1, 4)
        return x.reshape(bd, Q_ROWS, HEAD_DIM)

    def new_kv(x):
        x = x.reshape(bd, t_new, N_KV, HEAD_DIM).transpose(0, 2, 1, 3)
        return jnp.pad(x, ((0, 0), (0, 0), (0, LANES - t_new), (0, 0)))

    def heads_back(o):
        o = o.reshape(bd, N_KV, GROUP, t_new, HEAD_DIM).transpose(0, 3, 1, 2, 4)
        return o.reshape(bd * t_new, N_HEADS * HEAD_DIM)

    ka_s, va_s, kb_s, vb_s = (x.reshape(bd, t_new, N_KV * HEAD_DIM) for x in kv_s)
    ik_s = small_s[:, lay.small_ik:lay.small_ik + IDX_DIM].reshape(bd, t_new, IDX_DIM)
    iwf_s = small_s[:, lay.small_iwf:lay.small_iwf + LANES].reshape(bd, t_new, LANES)
    logf_s = iwf_s[..., LOGF_LANE:LOGF_LANE + N_HEADS]
    iq_s = cols("iq", 1024).reshape(bd, t_new, IDX_HEADS, IDX_DIM).transpose(0, 2, 1, 3)
    iq2 = jnp.broadcast_to(iq_s[:, :, None], (bd, IDX_HEADS, GROUP, t_new, IDX_DIM)).reshape(bd, IDX_HEADS * SUBLANES, IDX_DIM)
    iw_s = iwf_s[..., IW_LANE:IW_LANE + IDX_HEADS].transpose(0, 2, 1)
    iw2 = jnp.broadcast_to(iw_s[:, :, None], (bd, IDX_HEADS, GROUP, t_new)).reshape(bd, IDX_HEADS * SUBLANES, 1)
    ikt_new = jnp.pad(ik_s.transpose(0, 2, 1), ((0, 0), (0, 0), (0, LANES - t_new)))
    lft_new = jnp.pad(logf_s.transpose(0, 2, 1), ((0, 0), (0, 0), (0, LANES - t_new)))

    ik_cache_t = jnp.swapaxes(cache_idx_k[0], 1, 2)
    lf_cache_t = jnp.swapaxes(cache_fox_logf[0], 1, 2)
    kv_rows = lambda c: c.reshape(n_pool, PAGE * N_KV, HEAD_DIM)

    scores = _dsa_sample_keys(page_table, iq2, iw2, ikt_new, ik_cache_t, t_new)
    scores, thr = _dsa_sample_select(scores, min(TOPK_MAX, (past + t_new) // 4), _largest_divisor(bd, (8, 4, 2, 1)))
    oa_s = heads_back(_dsa_sample_attend(page_table, heads_major(cols("qa", 1024)), scores, thr,
                                         new_kv(ka_s), new_kv(va_s), kv_rows(cache_dsa_k), kv_rows(cache_dsa_v)))
    ob_s = heads_back(_fox_sample(page_table, heads_major(cols("qb", 1024)), lft_new, new_kv(kb_s), new_kv(vb_s),
                                  lf_cache_t, kv_rows(cache_fox_k), kv_rows(cache_fox_v), t_new))

    tm_m = _largest_divisor(s, (256, 128))
    x1_p, h2_p = _merge(xp2, oa_p, ob_p, proj_p, wa, wb, wo, g_ffn, lay, tm_m)
    x1_s, h2_s = _merge(xs2, oa_s, ob_s, proj_s, wa, wb, wo, g_ffn, lay, bd * t_new)

    tf = _largest_divisor(ff, (512, 256, 128))
    tm_f = _largest_divisor(s, (512, 256, 128))
    y_p, tails = _ffn_prompt(h2_p, x1_p, wg, wu, wd, cw, cb, g_fin, s, tm_f, tf)
    conv_p = tails.reshape(b, s // tm_f, SUBLANES, ff)[:, -1, SUBLANES - (CONV_W - 1):, :]

    t_major = lambda x: x.reshape(bd, t_new, -1).transpose(1, 0, 2).reshape(t_new * bd, -1)
    state_t = state_ffn_conv[0].transpose(1, 0, 2).reshape((CONV_W - 1) * bd, ff)
    y_s_t, st_t = _ffn_sample(t_major(h2_s), t_major(x1_s), state_t, wg, wu, wd, cw, cb, g_fin, bd, tf)
    y_s = y_s_t.reshape(t_new, bd, d).transpose(1, 0, 2)
    conv_s = st_t.reshape(CONV_W - 1, bd, ff).transpose(1, 0, 2)

    p_kv = [x.reshape(1, b, s, N_KV, HEAD_DIM) for x in kv_p]
    logf_at = lay.small_iwf + LOGF_LANE
    p_out = (p_kv[0], p_kv[1], small_p[:, lay.small_ik:lay.small_ik + IDX_DIM].reshape(1, b, s, IDX_DIM), p_kv[2], p_kv[3],
             small_p[:, logf_at:logf_at + N_HEADS].reshape(1, b, s, N_HEADS),
             conv_p[None])
    s_out = (ka_s.reshape(1, bd, t_new, N_KV, HEAD_DIM), va_s.reshape(1, bd, t_new, N_KV, HEAD_DIM), ik_s[None],
             kb_s.reshape(1, bd, t_new, N_KV, HEAD_DIM), vb_s.reshape(1, bd, t_new, N_KV, HEAD_DIM), logf_s[None],
             conv_s[None])
    return (y_p.reshape(b, s, d), y_s) + p_out + s_out
```

```python
import functools

import numpy as np
import jax
import jax.numpy as jnp
from jax import lax
from jax.experimental import pallas as pl
from jax.experimental.pallas import tpu as pltpu

F32 = jnp.float32
BF16 = jnp.bfloat16
I32 = jnp.int32

HEAD_DIM = 128
N_HEADS = 8
N_KV = 4
GROUP = N_HEADS // N_KV
IDX_HEADS = 16
IDX_DIM = 64
TOPK_MAX = 256
ROPE_THETA = 500000.0
ROT_DIM = HEAD_DIM // 4
IDX_ROT_DIM = IDX_DIM // 4
PAGE = 128
Q_BLOCK = 128
CONV_W = 3
RMS_EPS = 1e-6
ATT_SCALE = HEAD_DIM ** -0.5
IDX_SCALE = (IDX_HEADS * IDX_DIM) ** -0.5

LANES = 128
SUBLANES = 8
NEG = -1e30
NEG_INF = float("-inf")
INT_MIN = -2 ** 31
VMEM_LIMIT = 56 * 1024 * 1024

IW_LANE = 0
LOGF_LANE = IDX_HEADS

KV_NAMES = ("ka", "va", "kb", "vb")

NT_DIMS = (((1,), (1,)), ((), ()))


def _params(*sem):
    return pltpu.CompilerParams(dimension_semantics=sem, vmem_limit_bytes=VMEM_LIMIT)


def _nt(a, b):
    return lax.dot_general(a, b, NT_DIMS, preferred_element_type=F32)


def _rms(x, g):
    ms = jnp.mean(x * x, axis=-1, keepdims=True)
    return (x * lax.rsqrt(ms + RMS_EPS)) * g


class _Layout:
    def __init__(self, d_model):
        self.d = d_model
        self.tn = 512
        off = 0
        for name, size in (("ga", d_model), ("gb", d_model), ("qa", 1024), ("iq", 1024), ("qb", 1024)):
            assert off % size == 0, (name, off, size)
            setattr(self, name, off)
            off += size
        assert off % self.tn == 0
        self.proj_cols = off
        self.ik, self.iwf = off, off + LANES
        self.small_ik, self.small_iwf = 0, LANES
        off += self.tn
        for name in KV_NAMES:
            setattr(self, name, off)
            off += N_KV * HEAD_DIM
        assert N_KV * HEAD_DIM == self.tn
        self.nc = off

    def chunk_kinds(self):
        kinds = ["plain"] * (self.nc // LANES)
        for name, size, kind in (("qa", 1024, "qrope"), ("qb", 1024, "q"), ("iq", 1024, "rope64"),
                                 ("ik", self.tn, "s:plain"), ("ik", LANES, "s:rope64"), ("iwf", LANES, "s:iwf"),
                                 ("ka", 512, "kv0rope"), ("va", 512, "kv1"), ("kb", 512, "kv2"), ("vb", 512, "kv3")):
            start = getattr(self, name) // LANES
            for c in range(size // LANES):
                kinds[start + c] = kind
        return kinds


def _w_in_plan(lay):
    d = lay.d
    sizes = (1024, 512, 512, 1024, IDX_DIM, IDX_HEADS, 1024, 512, 512, N_HEADS, d, d)
    names = ("qa", "ka", "va", "iq", "ik", "iw", "qb", "kb", "vb", "fl", "ga", "gb")
    offs = np.concatenate([[0], np.cumsum(sizes)])
    src = {n: int(offs[k]) for k, n in enumerate(names)}
    plan = [()] * (lay.nc // LANES)
    for name, size in (("ga", d), ("gb", d), ("qa", 1024), ("iq", 1024), ("qb", 1024),
                       ("ka", 512), ("va", 512), ("kb", 512), ("vb", 512)):
        for c in range(size // LANES):
            plan[getattr(lay, name) // LANES + c] = ((src[name] + c * LANES, 0, LANES),)
    plan[lay.ik // LANES] = ((src["ik"], 0, IDX_DIM),)
    plan[lay.iwf // LANES] = ((src["iw"], IW_LANE, IDX_HEADS), (src["fl"], LOGF_LANE, N_HEADS))
    return tuple(plan), int(offs[-1])


def _prep_w_body(w_ref, o_ref, *, plan):
    cols = w_ref.shape[1]
    for c, pieces in enumerate(plan):
        parts, pos = [], 0
        for first, at, height in pieces:
            if at > pos:
                parts.append(jnp.zeros((at - pos, cols), F32))
            parts.append(w_ref[first:first + height, :])
            pos = at + height
        if pos < LANES:
            parts.append(jnp.zeros((LANES - pos, cols), F32))
        chunk = parts[0] if len(parts) == 1 else jnp.concatenate(parts, axis=0)
        o_ref[c * LANES:(c + 1) * LANES, :] = chunk.astype(BF16)


def _prep_w_in_t(w_in_t, lay):
    n_src, d = w_in_t.shape
    plan, n_cols = _w_in_plan(lay)
    assert n_cols == n_src and all(f % SUBLANES == 0 and a % SUBLANES == 0 for p in plan for f, a, _ in p)
    tc = _largest_divisor(d, (256, 128))
    return pl.pallas_call(
        functools.partial(_prep_w_body, plan=plan),
        grid=(d // tc,),
        in_specs=[pl.BlockSpec((n_src, tc), lambda i: (0, i))],
        out_specs=pl.BlockSpec((lay.nc, tc), lambda i: (0, i)),
        out_shape=jax.ShapeDtypeStruct((lay.nc, d), BF16),
        compiler_params=_params("arbitrary"),
        name="prep_w_in",
    )(w_in_t)


def _rope_tables(pos):
    def one(rot_dim, period):
        half = rot_dim // 2
        inv_freq = jnp.power(ROPE_THETA, -jnp.arange(half, dtype=F32) * (2.0 / rot_dim))
        ang = pos.astype(F32)[:, None] * inv_freq[None, :]
        cos, sin = jnp.cos(ang), jnp.sin(ang)
        n = pos.shape[0]
        c = jnp.concatenate([cos, cos, jnp.ones((n, period - rot_dim), F32)], axis=1)
        sa = jnp.concatenate([-sin, jnp.zeros((n, period - half), F32)], axis=1)
        sb = jnp.concatenate([jnp.zeros((n, half), F32), sin, jnp.zeros((n, period - rot_dim), F32)], axis=1)
        rep = LANES // period
        return [jnp.tile(t, (1, rep)) for t in (c, sa, sb)]
    return jnp.concatenate(one(ROT_DIM, HEAD_DIM) + one(IDX_ROT_DIM, IDX_DIM), axis=1)


def _proj_body(x_ref, g_ref, w_ref, tab_ref, bf_ref, o_ref, small_ref, *rest, tile_kinds):
    kv_refs, h_ref = rest[:len(KV_NAMES)], rest[len(KV_NAMES)]
    j = pl.program_id(1)
    tm = x_ref.shape[0]

    @pl.when(j == 0)
    def _():
        h_ref[...] = _rms(x_ref[...], g_ref[...]).astype(BF16)

    acc = _nt(h_ref[...], w_ref[...])

    def rope(a, base, half):
        c = tab_ref[:, base:base + LANES]
        sa = tab_ref[:, base + LANES:base + 2 * LANES]
        sb = tab_ref[:, base + 2 * LANES:base + 3 * LANES]
        return a * c + pltpu.roll(a, LANES - half, 1) * sa + pltpu.roll(a, half, 1) * sb

    def indexer_weight_and_log_forget(a):
        z = a + bf_ref[...]
        ls = jnp.minimum(z, 0.0) - jnp.log1p(jnp.exp(-jnp.abs(z)))
        lane = lax.broadcasted_iota(I32, a.shape, 1)
        return jnp.where(lane < LOGF_LANE, a * IDX_SCALE, jnp.where(lane < LOGF_LANE + N_HEADS, ls, a))

    def emit(kinds):
        for c, kind in enumerate(kinds):
            a = acc[:, c * LANES:(c + 1) * LANES]
            to_small = kind.startswith("s:")
            kind = kind[2:] if to_small else kind
            if kind == "qrope":
                a = rope(a, 0, ROT_DIM // 2) * ATT_SCALE
            elif kind == "q":
                a = a * ATT_SCALE
            elif kind == "rope64":
                a = rope(a, 3 * LANES, IDX_ROT_DIM // 2)
            elif kind == "iwf":
                a = indexer_weight_and_log_forget(a)
            if kind.startswith("kv"):
                if kind.endswith("rope"):
                    a = rope(a, 0, ROT_DIM // 2)
                kv_refs[int(kind[2])][pl.ds(c, tm, stride=N_KV), :] = a
            elif to_small:
                small_ref[:, c * LANES:(c + 1) * LANES] = a
            else:
                o_ref[:, c * LANES:(c + 1) * LANES] = a.astype(BF16)

    groups = {}
    for t, kinds in enumerate(tile_kinds):
        groups.setdefault(kinds, []).append(t)
    for kinds, tiles in groups.items():
        cond = functools.reduce(jnp.logical_or, [j == t for t in tiles])
        if all(k == "plain" for k in kinds):
            @pl.when(cond)
            def _():
                o_ref[...] = acc.astype(BF16)
        else:
            pl.when(cond)(functools.partial(emit, kinds))


def _norm_proj(x2d, gamma, w_perm, tab, bf_row, lay, tm):
    n, d = x2d.shape
    tn = lay.tn
    kinds = lay.chunk_kinds()
    per = tn // LANES
    tile_kinds = tuple(tuple(kinds[t * per:(t + 1) * per]) for t in range(lay.nc // tn))
    tab_blocks = tab.shape[0] // tm
    last_proj_tile = lay.proj_cols // tn - 1
    kv_spec = pl.BlockSpec((tm * N_KV, HEAD_DIM), lambda i, j: (i, 0))
    return pl.pallas_call(
        functools.partial(_proj_body, tile_kinds=tile_kinds),
        grid=(n // tm, lay.nc // tn),
        in_specs=[
            pl.BlockSpec((tm, d), lambda i, j: (i, 0), pipeline_mode=pl.Buffered(1)),
            pl.BlockSpec((1, d), lambda i, j: (0, 0)),
            pl.BlockSpec((tn, d), lambda i, j: (j, 0)),
            pl.BlockSpec((tm, 6 * LANES), lambda i, j: (i % tab_blocks, 0)),
            pl.BlockSpec((1, LANES), lambda i, j: (0, 0)),
        ],
        out_specs=[pl.BlockSpec((tm, tn), lambda i, j: (i, jnp.minimum(j, last_proj_tile))),
                   pl.BlockSpec((tm, tn), lambda i, j: (i, 0))] + [kv_spec] * len(KV_NAMES),
        out_shape=[jax.ShapeDtypeStruct((n, lay.proj_cols), BF16), jax.ShapeDtypeStruct((n, tn), F32)]
        + [jax.ShapeDtypeStruct((n * N_KV, HEAD_DIM), F32)] * len(KV_NAMES),
        scratch_shapes=[pltpu.VMEM((tm, d), BF16)],
        compiler_params=_params("arbitrary", "arbitrary"),
        name="norm_proj",
    )(x2d, gamma, w_perm, tab, bf_row)


def _code_to_float(code):
    bits = jnp.where(code < 0, code ^ jnp.int32(0x7FFFFFFF), code)
    return pltpu.bitcast(bits, F32)


def _kth_largest(read_scores, rows, k):
    def body(it, code):
        cand = code + jnp.left_shift(jnp.int32(1), 31 - it)
        cnt = jnp.sum(jnp.where(read_scores() >= _code_to_float(cand), 1.0, 0.0), axis=1, keepdims=True)
        return jnp.where(cnt >= k, cand, code)

    code = lax.fori_loop(0, 32, body, jnp.full((rows, 1), INT_MIN, I32))
    return _code_to_float(code), code == INT_MIN


def _count(mask):
    return jnp.sum(jnp.where(mask, 1.0, 0.0), axis=1, keepdims=True)


def _total(mask):
    ones = jnp.where(mask, 1.0, 0.0)
    return jnp.sum(jnp.sum(ones, axis=0, keepdims=True), axis=1, keepdims=True)[0, 0]


def _earlier_in_chunk():
    r = lax.broadcasted_iota(I32, (LANES, LANES), 0)
    c = lax.broadcasted_iota(I32, (LANES, LANES), 1)
    return jnp.where(r < c, 1.0, 0.0).astype(BF16)


def _tied_keys_to_keep(eq, seen, need, earlier):
    ones = jnp.where(eq, 1.0, 0.0)
    rank = seen + jnp.dot(ones.astype(BF16), earlier, preferred_element_type=F32)
    return eq & (rank < need), seen + jnp.sum(ones, axis=1, keepdims=True)


KEY_BUCKET = 512


def _key_limits(s):
    step = min(KEY_BUCKET, s)
    assert s % step == 0
    return tuple(range(step, s + 1, step))


def _for_causal_limit(i, limits, block_fn):
    q_end = (i + 1) * Q_BLOCK
    prev = 0
    for lim in limits:
        pl.when((q_end > prev) & (q_end <= lim))(functools.partial(block_fn, lim))
        prev = lim


def _load_kv_heads(src_ref, dst_ref):
    tokens = dst_ref.shape[0]
    for kh in range(N_KV):
        dst_ref[:, kh * HEAD_DIM:(kh + 1) * HEAD_DIM] = src_ref[pl.ds(kh, tokens, stride=N_KV), :].astype(BF16)


def _attend_heads(q_ref, kb_ref, vb_ref, o_ref, n, logit_bias):
    for kh in range(N_KV):
        kk = kb_ref[0:n, kh * HEAD_DIM:(kh + 1) * HEAD_DIM]
        vv = vb_ref[0:n, kh * HEAD_DIM:(kh + 1) * HEAD_DIM]
        for g in range(GROUP):
            h = kh * GROUP + g
            q = q_ref[:, h * HEAD_DIM:(h + 1) * HEAD_DIM]
            lg = _nt(q, kk) + logit_bias(h)
            m = jnp.max(lg, axis=1, keepdims=True)
            e = jnp.exp(lg - m)
            l = jnp.sum(e, axis=1, keepdims=True)
            o = jnp.dot(e.astype(BF16), vv, preferred_element_type=F32)
            o_ref[:, h * HEAD_DIM:(h + 1) * HEAD_DIM] = o / l


def _dsa_prompt_body(iq_ref, iwf_ref, ik_ref, q_ref, k_ref, v_ref, o_ref,
                     ikb_ref, kb_ref, vb_ref, sc_ref, bias_ref, *, topk, limits):
    i = pl.program_id(1)

    @pl.when(i == 0)
    def _():
        ikb_ref[...] = ik_ref[:, :IDX_DIM].astype(BF16)
        _load_kv_heads(k_ref, kb_ref)
        _load_kv_heads(v_ref, vb_ref)

    def block(n):
        ikb = ikb_ref[0:n, :]
        for h in range(IDX_HEADS):
            qh = iq_ref[:, h * IDX_DIM:(h + 1) * IDX_DIM]
            s = jnp.maximum(_nt(qh, ikb), 0.0) * iwf_ref[:, IW_LANE + h:IW_LANE + h + 1]
            if h == 0:
                sc_ref[:, 0:n] = s
            else:
                sc_ref[:, 0:n] += s
        row = i * Q_BLOCK + lax.broadcasted_iota(I32, (Q_BLOCK, n), 0)
        col = lax.broadcasted_iota(I32, (Q_BLOCK, n), 1)
        causal = col <= row
        sc_ref[:, 0:n] = jnp.where(causal, sc_ref[:, 0:n], NEG_INF)
        thr, none = _kth_largest(lambda: sc_ref[:, 0:n], Q_BLOCK, topk)
        chosen = ((sc_ref[:, 0:n] >= thr) | none) & causal
        bias_ref[:, 0:n] = jnp.where(chosen, 0.0, NEG)

        first = i * Q_BLOCK
        short = jnp.clip(topk - first, 0, Q_BLOCK)
        expected = short * first + (short * (short + 1)) // 2 + (Q_BLOCK - short) * topk

        @pl.when(_total(chosen) > expected.astype(F32))
        def _():
            need = topk - _count((sc_ref[:, 0:n] > thr) & causal)
            earlier = _earlier_in_chunk()
            seen = jnp.zeros((Q_BLOCK, 1), F32)
            for c in range(n // LANES):
                lanes = slice(c * LANES, (c + 1) * LANES)
                sc = sc_ref[:, lanes]
                ok = (c * LANES + lax.broadcasted_iota(I32, (Q_BLOCK, LANES), 1)
                      <= first + lax.broadcasted_iota(I32, (Q_BLOCK, LANES), 0))
                keep, seen = _tied_keys_to_keep((sc == thr) & ok, seen, need, earlier)
                bias_ref[:, lanes] = jnp.where((((sc > thr) | none) & ok) | keep, 0.0, NEG)
        _attend_heads(q_ref, kb_ref, vb_ref, o_ref, n, lambda h: bias_ref[:, 0:n])

    _for_causal_limit(i, limits, block)


def _dsa_prompt(proj, small, k4, v4, b, s, lay):
    nb = s // Q_BLOCK
    topk = min(TOPK_MAX, s // 4)
    row = lambda bb, i: bb * nb + i
    return pl.pallas_call(
        functools.partial(_dsa_prompt_body, topk=topk, limits=_key_limits(s)),
        grid=(b, nb),
        in_specs=[
            pl.BlockSpec((Q_BLOCK, 1024), lambda bb, i: (row(bb, i), lay.iq // 1024)),
            pl.BlockSpec((Q_BLOCK, LANES), lambda bb, i: (row(bb, i), lay.small_iwf // LANES)),
            pl.BlockSpec((s, LANES), lambda bb, i: (bb, lay.small_ik // LANES)),
            pl.BlockSpec((Q_BLOCK, 1024), lambda bb, i: (row(bb, i), lay.qa // 1024)),
            pl.BlockSpec((s * N_KV, HEAD_DIM), lambda bb, i: (bb, 0)),
            pl.BlockSpec((s * N_KV, HEAD_DIM), lambda bb, i: (bb, 0)),
        ],
        out_specs=pl.BlockSpec((Q_BLOCK, 1024), lambda bb, i: (row(bb, i), 0)),
        out_shape=jax.ShapeDtypeStruct((b * s, 1024), F32),
        scratch_shapes=[pltpu.VMEM((s, IDX_DIM), BF16), pltpu.VMEM((s, 512), BF16), pltpu.VMEM((s, 512), BF16),
                        pltpu.VMEM((Q_BLOCK, s), F32), pltpu.VMEM((Q_BLOCK, s), F32)],
        compiler_params=_params("arbitrary", "arbitrary"),
        name="dsa_prompt",
    )(proj, small, small, proj, k4, v4)


CUM_BLOCK = 256


def _fox_prompt_body(q_ref, k_ref, v_ref, lf_ref, o_ref, kb_ref, vb_ref, c_ref, ct_ref, bias_ref, *, limits):
    i = pl.program_id(1)
    s_len = kb_ref.shape[0]

    @pl.when(i == 0)
    def _():
        _load_kv_heads(k_ref, kb_ref)
        _load_kv_heads(v_ref, vb_ref)
        r = lax.broadcasted_iota(I32, (CUM_BLOCK, CUM_BLOCK), 0)
        c = lax.broadcasted_iota(I32, (CUM_BLOCK, CUM_BLOCK), 1)
        tri = jnp.where(c <= r, 1.0, 0.0).astype(F32)
        carry = jnp.zeros((1, LANES), F32)
        for blk in range(s_len // CUM_BLOCK):
            xb = lf_ref[blk * CUM_BLOCK:(blk + 1) * CUM_BLOCK, :]
            cb = jnp.dot(tri, xb, precision=lax.Precision.HIGHEST, preferred_element_type=F32) + carry
            c_ref[blk * CUM_BLOCK:(blk + 1) * CUM_BLOCK, :] = cb
            carry = cb[CUM_BLOCK - 1:CUM_BLOCK, :]
        ct_ref[...] = c_ref[...].T

    start = pl.multiple_of(i * Q_BLOCK, Q_BLOCK)

    def block(n):
        row = i * Q_BLOCK + lax.broadcasted_iota(I32, (Q_BLOCK, n), 0)
        col = lax.broadcasted_iota(I32, (Q_BLOCK, n), 1)
        bias_ref[:, 0:n] = jnp.where(col <= row, 0.0, NEG)

        def logit_bias(h):
            cq = c_ref[pl.ds(start, Q_BLOCK), LOGF_LANE + h:LOGF_LANE + h + 1]
            ck = ct_ref[LOGF_LANE + h:LOGF_LANE + h + 1, 0:n]
            return (cq - ck) + bias_ref[:, 0:n]

        _attend_heads(q_ref, kb_ref, vb_ref, o_ref, n, logit_bias)

    _for_causal_limit(i, limits, block)


def _fox_prompt(proj, small, k4, v4, b, s, lay):
    nb = s // Q_BLOCK
    assert s % CUM_BLOCK == 0
    row = lambda bb, i: bb * nb + i
    return pl.pallas_call(
        functools.partial(_fox_prompt_body, limits=_key_limits(s)),
        grid=(b, nb),
        in_specs=[
            pl.BlockSpec((Q_BLOCK, 1024), lambda bb, i: (row(bb, i), lay.qb // 1024)),
            pl.BlockSpec((s * N_KV, HEAD_DIM), lambda bb, i: (bb, 0)),
            pl.BlockSpec((s * N_KV, HEAD_DIM), lambda bb, i: (bb, 0)),
            pl.BlockSpec((s, LANES), lambda bb, i: (bb, lay.small_iwf // LANES)),
        ],
        out_specs=pl.BlockSpec((Q_BLOCK, 1024), lambda bb, i: (row(bb, i), 0)),
        out_shape=jax.ShapeDtypeStruct((b * s, 1024), F32),
        scratch_shapes=[pltpu.VMEM((s, 512), BF16), pltpu.VMEM((s, 512), BF16),
                        pltpu.VMEM((s, LANES), F32), pltpu.VMEM((LANES, s), F32), pltpu.VMEM((Q_BLOCK, s), F32)],
        compiler_params=_params("arbitrary", "arbitrary"),
        name="fox_prompt",
    )(proj, k4, v4, small)


Q_ROWS = N_KV * SUBLANES


def _pages_per_step(n_pages, most=16):
    for pp in (32, 16, 8, 4, 2, 1):
        if pp <= most and n_pages % pp == 0:
            return pp


def _page_specs(block, pp, page_of):
    def spec(r):
        return pl.BlockSpec((None,) + block, lambda bb, c, pt: (pt[bb, page_of(c, r)],) + (0,) * len(block))
    return [spec(r) for r in range(pp)]


def _per_seq(shape):
    return pl.BlockSpec((None,) + shape, lambda bb, c, pt: (bb,) + (0,) * len(shape))


def _shared(shape):
    return pl.BlockSpec(shape, lambda bb, c, pt: (0,) * len(shape))


def _kv_rows(pages, kh):
    return jnp.concatenate([p[pl.ds(kh, PAGE, stride=N_KV), :] for p in pages], axis=0).astype(BF16)


def _softmax_update(q_ref, keys_of, values_of, bias, sel, m_ref, l_ref, acc_ref):
    lg = jnp.concatenate([_nt(_q_rows(q_ref, kh), keys_of(kh)) for kh in range(N_KV)], axis=0)
    if bias is not None:
        lg = lg + bias
    if sel is not None:
        lg = jnp.where(sel, lg, NEG)
    m_old = m_ref[...]
    m_new = jnp.maximum(m_old, jnp.max(lg, axis=1, keepdims=True))
    corr = jnp.exp(m_old - m_new)
    e = jnp.exp(lg - m_new)
    if sel is not None:
        e = jnp.where(sel, e, 0.0)
    l_ref[...] = l_ref[...] * corr + jnp.sum(e, axis=1, keepdims=True)
    pv = jnp.concatenate([jnp.dot(e[kh * SUBLANES:(kh + 1) * SUBLANES].astype(BF16), values_of(kh),
                                  preferred_element_type=F32) for kh in range(N_KV)], axis=0)
    acc_ref[...] = acc_ref[...] * corr + pv
    m_ref[...] = m_new


def _softmax_init(m_ref, l_ref, acc_ref):
    m_ref[...] = jnp.full(m_ref.shape, NEG, F32)
    l_ref[...] = jnp.zeros(l_ref.shape, F32)
    acc_ref[...] = jnp.zeros(acc_ref.shape, F32)


def _softmax_scratch():
    return [pltpu.VMEM((Q_ROWS, 1), F32), pltpu.VMEM((Q_ROWS, 1), F32), pltpu.VMEM((Q_ROWS, HEAD_DIM), F32)]


def _q_rows(q_ref, kh):
    return q_ref[kh * SUBLANES:(kh + 1) * SUBLANES, :].astype(BF16)


def _dsa_sample_keys_body(pt_ref, iq_ref, iw_ref, ikn_ref, *rest, pp, n_new):
    pages, sc_ref = rest[:pp], rest[pp]
    c = pl.program_id(1)
    past = sc_ref.shape[1] - LANES
    q = iq_ref[...].astype(BF16)
    w = iw_ref[...]

    def scores_of(ikt):
        s = jnp.maximum(jnp.dot(q, ikt.astype(BF16), preferred_element_type=F32), 0.0) * w
        acc = s[0:SUBLANES]
        for h in range(1, IDX_HEADS):
            acc = acc + s[h * SUBLANES:(h + 1) * SUBLANES]
        return acc

    sc_ref[:, pl.ds(pl.multiple_of(c * (pp * PAGE), pp * PAGE), pp * PAGE)] = scores_of(
        jnp.concatenate([p[...] for p in pages], axis=1))

    @pl.when(c == 0)
    def _():
        row = lax.broadcasted_iota(I32, (SUBLANES, LANES), 0)
        lane = lax.broadcasted_iota(I32, (SUBLANES, LANES), 1)
        sc_ref[:, past:past + LANES] = jnp.where(lane <= row % n_new, scores_of(ikn_ref[...]), NEG_INF)


def _dsa_sample_keys(page_table, iq2, iw2, ikt_new, ik_cache_t, n_new):
    bd, n_pages = page_table.shape
    pp = _pages_per_step(n_pages, 32)
    past = n_pages * PAGE
    return pl.pallas_call(
        functools.partial(_dsa_sample_keys_body, pp=pp, n_new=n_new),
        grid_spec=pltpu.PrefetchScalarGridSpec(
            num_scalar_prefetch=1,
            grid=(bd, n_pages // pp),
            in_specs=[_per_seq((IDX_HEADS * SUBLANES, IDX_DIM)), _per_seq((IDX_HEADS * SUBLANES, 1)),
                      _per_seq((IDX_DIM, LANES))]
            + _page_specs((IDX_DIM, PAGE), pp, lambda c, r: c * pp + r),
            out_specs=pl.BlockSpec((SUBLANES, past + LANES), lambda bb, c, pt: (bb, 0)),
        ),
        out_shape=jax.ShapeDtypeStruct((bd * SUBLANES, past + LANES), F32),
        compiler_params=_params("arbitrary", "arbitrary"),
        name="dsa_sample_keys",
    )(page_table, iq2, iw2, ikt_new, *([ik_cache_t] * pp))


def _dsa_sample_select_body(sc_ref, sel_ref, thr_ref, *, topk):
    rows, n = sc_ref.shape
    thr, _ = _kth_largest(lambda: sc_ref[...], rows, topk)
    thr_ref[...] = jnp.broadcast_to(thr, thr_ref.shape)
    sel_ref[...] = sc_ref[...]

    @pl.when(jnp.max(_count(sc_ref[...] >= thr)) > topk)
    def _():
        need = topk - _count(sc_ref[...] > thr)
        earlier = _earlier_in_chunk()

        def chunk(c, seen):
            lanes = pl.ds(pl.multiple_of(c * LANES, LANES), LANES)
            sc = sc_ref[:, lanes]
            eq = sc == thr
            keep, seen = _tied_keys_to_keep(eq, seen, need, earlier)
            sel_ref[:, lanes] = jnp.where(eq & jnp.logical_not(keep), NEG_INF, sc)
            return seen

        lax.fori_loop(0, n // LANES, chunk, jnp.zeros((rows, 1), F32))


def _dsa_sample_select(scores, topk, seqs_per_step):
    rows_total, n = scores.shape
    rows = seqs_per_step * SUBLANES
    return pl.pallas_call(
        functools.partial(_dsa_sample_select_body, topk=topk),
        grid=(rows_total // rows,),
        in_specs=[pl.BlockSpec((rows, n), lambda i: (i, 0))],
        out_specs=[pl.BlockSpec((rows, n), lambda i: (i, 0)), pl.BlockSpec((rows, LANES), lambda i: (i, 0))],
        out_shape=[jax.ShapeDtypeStruct((rows_total, n), F32), jax.ShapeDtypeStruct((rows_total, LANES), F32)],
        compiler_params=_params("arbitrary"),
        name="dsa_sample_select",
    )(scores)


def _chunk_pages(n_pages):
    for pp in (16, 8, 4, 2, 1):
        if n_pages % pp == 0 and (n_pages // pp) % 2 == 0:
            return pp
    raise ValueError(f"need an even number of page chunks, got {n_pages} pages")


def _walk_page_chunks(pt_ref, streams, pp, n_chunks, page_of, compute):
    b = pl.program_id(0)

    def copies(seq, c, slot, to_wait=False):
        out = []
        for r in range(pp):
            page = 0 if to_wait else pt_ref[seq, page_of(c, r)]
            for hbm, buf, sem in streams:
                out.append(pltpu.make_async_copy(hbm.at[page], buf.at[slot, r], sem.at[slot]))
        return out

    @pl.when(b == 0)
    def _():
        for d in copies(b, 0, 0):
            d.start()

    for c in range(n_chunks):
        slot = c % 2
        for d in copies(b, c, slot, to_wait=True):
            d.wait()
        if c + 1 < n_chunks:
            for d in copies(b, c + 1, 1 - slot):
                d.start()
        else:
            @pl.when(b + 1 < pl.num_programs(0))
            def _():
                for d in copies(b + 1, 0, 1 - slot):
                    d.start()
        compute(c, slot)


def _dsa_sample_attend_body(pt_ref, q_ref, sc_ref, thr_ref, knew_ref, vnew_ref, k_hbm, v_hbm, o_ref,
                            kbuf, vbuf, ksem, vsem, m_ref, l_ref, acc_ref, *, pp, n_chunks):
    past = n_chunks * pp * PAGE
    thr = jnp.concatenate([thr_ref[:, 0:1]] * N_KV, axis=0)

    def selected(scores):
        scores = jnp.concatenate([scores] * N_KV, axis=0)
        return (scores >= thr) & (scores > NEG_INF)

    _softmax_init(m_ref, l_ref, acc_ref)
    _softmax_update(q_ref, lambda kh: knew_ref[kh].astype(BF16), lambda kh: vnew_ref[kh].astype(BF16),
                    None, selected(sc_ref[:, past:past + LANES]), m_ref, l_ref, acc_ref)

    def compute(c, slot):
        k_pages = [kbuf.at[slot, r] for r in range(pp)]
        v_pages = [vbuf.at[slot, r] for r in range(pp)]
        sel = selected(sc_ref[:, c * pp * PAGE:(c + 1) * pp * PAGE])
        _softmax_update(q_ref, functools.partial(_kv_rows, k_pages), functools.partial(_kv_rows, v_pages),
                        None, sel, m_ref, l_ref, acc_ref)

    _walk_page_chunks(pt_ref, [(k_hbm, kbuf, ksem), (v_hbm, vbuf, vsem)], pp, n_chunks,
                      lambda c, r: c * pp + r, compute)
    o_ref[...] = acc_ref[...] / l_ref[...]


def _per_seq1(shape):
    return pl.BlockSpec((None,) + shape, lambda bb, pt: (bb,) + (0,) * len(shape))


def _kv_page_scratch(pp):
    page = (PAGE * N_KV, HEAD_DIM)
    return [pltpu.VMEM((2, pp) + page, F32), pltpu.VMEM((2, pp) + page, F32),
            pltpu.SemaphoreType.DMA((2,)), pltpu.SemaphoreType.DMA((2,))]


def _dsa_sample_attend(page_table, q_s, scores, thr, k_new, v_new, k_cache, v_cache):
    bd, n_pages = page_table.shape
    pp = _chunk_pages(n_pages)
    n_cols = scores.shape[1]
    hbm = pl.BlockSpec(memory_space=pl.ANY)
    return pl.pallas_call(
        functools.partial(_dsa_sample_attend_body, pp=pp, n_chunks=n_pages // pp),
        grid_spec=pltpu.PrefetchScalarGridSpec(
            num_scalar_prefetch=1,
            grid=(bd,),
            in_specs=[_per_seq1((Q_ROWS, HEAD_DIM)),
                      pl.BlockSpec((SUBLANES, n_cols), lambda bb, pt: (bb, 0)),
                      pl.BlockSpec((SUBLANES, LANES), lambda bb, pt: (bb, 0)),
                      _per_seq1((N_KV, LANES, HEAD_DIM)), _per_seq1((N_KV, LANES, HEAD_DIM)), hbm, hbm],
            out_specs=_per_seq1((Q_ROWS, HEAD_DIM)),
            scratch_shapes=_kv_page_scratch(pp) + _softmax_scratch(),
        ),
        out_shape=jax.ShapeDtypeStruct((bd, Q_ROWS, HEAD_DIM), F32),
        compiler_params=_params("arbitrary"),
        name="dsa_sample_attend",
    )(page_table, q_s, scores, thr, k_new, v_new, k_cache, v_cache)


def _fox_sample_body(pt_ref, q_ref, lfn_ref, knew_ref, vnew_ref, rep_ref, later_ref, lf_hbm, k_hbm, v_hbm, o_ref,
                     lfbuf, kbuf, vbuf, lfsem, ksem, vsem, m_ref, l_ref, acc_ref, cq_ref, carry_ref,
                     *, pp, n_chunks, n_new):
    hi = lax.Precision.HIGHEST
    n_pages = pp * n_chunks

    _softmax_init(m_ref, l_ref, acc_ref)
    carry_ref[...] = jnp.zeros(carry_ref.shape, F32)
    r_io = lax.broadcasted_iota(I32, (LANES, LANES), 0)
    c_io = lax.broadcasted_iota(I32, (LANES, LANES), 1)
    incl = jnp.where(r_io <= c_io, 1.0, 0.0).astype(F32)
    cum = jnp.dot(lfn_ref[...], incl, precision=hi, preferred_element_type=F32)
    cg = jnp.dot(rep_ref[0:Q_ROWS, 0:N_HEADS], cum, precision=hi, preferred_element_type=F32)
    row = lax.broadcasted_iota(I32, (Q_ROWS, LANES), 0)
    lane = lax.broadcasted_iota(I32, (Q_ROWS, LANES), 1)
    own = lane == row % n_new
    cq_ref[...] = jnp.sum(jnp.where(own, cg, 0.0), axis=1, keepdims=True)
    _softmax_update(q_ref, lambda kh: knew_ref[kh].astype(BF16), lambda kh: vnew_ref[kh].astype(BF16),
                    cq_ref[...] - cg, lane <= row % n_new, m_ref, l_ref, acc_ref)

    def compute(c, slot):
        k_pages = [kbuf.at[slot, r] for r in range(pp)]
        v_pages = [vbuf.at[slot, r] for r in range(pp)]
        lf_all = jnp.concatenate([lfbuf[slot, r] for r in range(pp)], axis=0)
        lf_rows = jnp.dot(rep_ref[...], lf_all, precision=hi, preferred_element_type=F32)
        within = jnp.dot(lf_rows, later_ref[...], precision=hi, preferred_element_type=F32)
        total = within[:, 0:1] + lf_rows[:, 0:1]
        run = carry_ref[...]
        biases = []
        for r in range(pp):
            biases.append(within[r * Q_ROWS:(r + 1) * Q_ROWS] + (run + cq_ref[...]))
            run = run + total[r * Q_ROWS:(r + 1) * Q_ROWS]
        carry_ref[...] = run
        _softmax_update(q_ref, functools.partial(_kv_rows, k_pages), functools.partial(_kv_rows, v_pages),
                        jnp.concatenate(biases, axis=1), None, m_ref, l_ref, acc_ref)

    _walk_page_chunks(pt_ref, [(lf_hbm, lfbuf, lfsem), (k_hbm, kbuf, ksem), (v_hbm, vbuf, vsem)], pp, n_chunks,
                      lambda c, r: n_pages - 1 - (c * pp + r), compute)
    o_ref[...] = acc_ref[...] / l_ref[...]


def _fox_sample(page_table, q_s, lft_new, k_new, v_new, lf_cache_t, k_cache, v_cache, n_new):
    bd, n_pages = page_table.shape
    pp = _chunk_pages(n_pages)
    row_head = np.arange(Q_ROWS) // n_new
    rep_one = (row_head[:, None] == np.arange(N_HEADS)[None, :]).astype(np.float32)
    rep = jnp.asarray(np.kron(np.eye(pp, dtype=np.float32), rep_one))
    later = jnp.asarray((np.arange(PAGE)[:, None] > np.arange(PAGE)[None, :]).astype(np.float32))
    hbm = pl.BlockSpec(memory_space=pl.ANY)
    shared = lambda shape: pl.BlockSpec(shape, lambda bb, pt: (0,) * len(shape))
    return pl.pallas_call(
        functools.partial(_fox_sample_body, pp=pp, n_chunks=n_pages // pp, n_new=n_new),
        grid_spec=pltpu.PrefetchScalarGridSpec(
            num_scalar_prefetch=1,
            grid=(bd,),
            in_specs=[_per_seq1((Q_ROWS, HEAD_DIM)), _per_seq1((N_HEADS, LANES)),
                      _per_seq1((N_KV, LANES, HEAD_DIM)), _per_seq1((N_KV, LANES, HEAD_DIM)),
                      shared((pp * Q_ROWS, pp * N_HEADS)), shared((PAGE, PAGE)), hbm, hbm, hbm],
            out_specs=_per_seq1((Q_ROWS, HEAD_DIM)),
            scratch_shapes=[pltpu.VMEM((2, pp, N_HEADS, PAGE), F32)] + _kv_page_scratch(pp)[:2]
            + [pltpu.SemaphoreType.DMA((2,))] * 3
            + _softmax_scratch() + [pltpu.VMEM((Q_ROWS, 1), F32), pltpu.VMEM((Q_ROWS, 1), F32)],
        ),
        out_shape=jax.ShapeDtypeStruct((bd, Q_ROWS, HEAD_DIM), F32),
        compiler_params=_params("arbitrary"),
        name="fox_sample",
    )(page_table, q_s, lft_new, k_new, v_new, rep, later, lf_cache_t, k_cache, v_cache)


def _merge_body(x_ref, oa_ref, ob_ref, ga_ref, gb_ref, wa_ref, wb_ref, wo_ref, gn_ref, x1_ref, h2_ref):
    a = jnp.dot(oa_ref[...].astype(BF16), wa_ref[...], preferred_element_type=F32)
    b = jnp.dot(ob_ref[...].astype(BF16), wb_ref[...], preferred_element_type=F32)
    merged = jax.nn.sigmoid(ga_ref[...].astype(F32)) * a + jax.nn.sigmoid(gb_ref[...].astype(F32)) * b
    x1 = x_ref[...] + jnp.dot(merged.astype(BF16), wo_ref[...], preferred_element_type=F32)
    x1_ref[...] = x1
    h2_ref[...] = _rms(x1, gn_ref[...]).astype(BF16)


def _merge(x2d, o_a, o_b, proj, wa, wb, wo, ffn_norm, lay, tm):
    n, d = x2d.shape
    const = lambda shape: pl.BlockSpec(shape, lambda i: (0, 0), pipeline_mode=pl.Buffered(1))
    return pl.pallas_call(
        _merge_body,
        grid=(n // tm,),
        in_specs=[
            pl.BlockSpec((tm, d), lambda i: (i, 0)),
            pl.BlockSpec((tm, 1024), lambda i: (i, 0)),
            pl.BlockSpec((tm, 1024), lambda i: (i, 0)),
            pl.BlockSpec((tm, d), lambda i: (i, lay.ga // d)),
            pl.BlockSpec((tm, d), lambda i: (i, lay.gb // d)),
            const((1024, d)), const((1024, d)), const((d, d)), const((1, d)),
        ],
        out_specs=[pl.BlockSpec((tm, d), lambda i: (i, 0)), pl.BlockSpec((tm, d), lambda i: (i, 0))],
        out_shape=[jax.ShapeDtypeStruct((n, d), F32), jax.ShapeDtypeStruct((n, d), BF16)],
        compiler_params=_params("arbitrary"),
        name="merge",
    )(x2d, o_a, o_b, proj, proj, wa, wb, wo, ffn_norm)


FFN_ROW_CHUNKS = 2


def _ffn_zero_acc(f, acc_ref):
    @pl.when(f == 0)
    def _():
        acc_ref[...] = jnp.zeros(acc_ref.shape, F32)


def _ffn_finish(f, n_f, x1_ref, fn_ref, y_ref, acc_ref):
    @pl.when(f == n_f - 1)
    def _():
        y_ref[...] = _rms(x1_ref[...] + acc_ref[...], fn_ref[...])


def _ffn_rows(h, first, rows, n_rows, taps, wg_ref, wu_ref, wd_ref, cw_ref, cb_ref, ext_ref, acc_ref):
    gp = jnp.dot(h, wg_ref[...], preferred_element_type=F32)
    up = jnp.dot(h, wu_ref[...], preferred_element_type=F32)
    ext_ref[first + rows:first + rows + n_rows, :] = gp
    conv = cb_ref[...]
    for j, back in enumerate(taps):
        src = gp if back == 0 else ext_ref[first + rows - back:first + rows - back + n_rows, :]
        conv = conv + cw_ref[j:j + 1, :] * src
    act = (conv * jax.nn.sigmoid(conv)) * up
    acc_ref[rows:rows + n_rows, :] += jnp.dot(act.astype(BF16), wd_ref[...], preferred_element_type=F32)


def _ffn_prompt_body(h_ref, halo_ref, wg_ref, wu_ref, wd_ref, cw_ref, cb_ref, x1_ref, fn_ref,
                     y_ref, tail_ref, acc_ref, ext_ref, *, tiles_per_seq):
    i = pl.program_id(0)
    f = pl.program_id(1)
    tm = h_ref.shape[0]
    _ffn_zero_acc(f, acc_ref)
    halo = jnp.dot(halo_ref[...], wg_ref[...], preferred_element_type=F32)
    ext_ref[0:SUBLANES, :] = jnp.where(i % tiles_per_seq == 0, 0.0, halo)
    taps = tuple(CONV_W - 1 - j for j in range(CONV_W))
    rc = tm // FFN_ROW_CHUNKS
    for c in range(FFN_ROW_CHUNKS):
        _ffn_rows(h_ref[c * rc:(c + 1) * rc, :], SUBLANES, c * rc, rc, taps,
                  wg_ref, wu_ref, wd_ref, cw_ref, cb_ref, ext_ref, acc_ref)
    tail_ref[...] = ext_ref[tm:tm + SUBLANES, :]
    _ffn_finish(f, pl.num_programs(1), x1_ref, fn_ref, y_ref, acc_ref)


def _ffn_prompt(h2, x1, wg, wu, wd, conv_w, conv_b, final_norm, s, tm, tf):
    n, d = x1.shape
    ff = wg.shape[1]
    assert s % tm == 0 and tm % SUBLANES == 0
    hb = tm // SUBLANES
    return pl.pallas_call(
        functools.partial(_ffn_prompt_body, tiles_per_seq=s // tm),
        grid=(n // tm, ff // tf),
        in_specs=[
            pl.BlockSpec((tm, d), lambda i, f: (i, 0)),
            pl.BlockSpec((SUBLANES, d), lambda i, f: (jnp.maximum(i * hb - 1, 0), 0)),
            pl.BlockSpec((d, tf), lambda i, f: (0, f)),
            pl.BlockSpec((d, tf), lambda i, f: (0, f)),
            pl.BlockSpec((tf, d), lambda i, f: (f, 0)),
            pl.BlockSpec((CONV_W, tf), lambda i, f: (0, f)),
            pl.BlockSpec((1, tf), lambda i, f: (0, f)),
            pl.BlockSpec((tm, d), lambda i, f: (i, 0)),
            pl.BlockSpec((1, d), lambda i, f: (0, 0)),
        ],
        out_specs=[pl.BlockSpec((tm, d), lambda i, f: (i, 0)),
                   pl.BlockSpec((SUBLANES, tf), lambda i, f: (i, f))],
        out_shape=[jax.ShapeDtypeStruct((n, d), F32), jax.ShapeDtypeStruct((n // tm * SUBLANES, ff), F32)],
        scratch_shapes=[pltpu.VMEM((tm, d), F32), pltpu.VMEM((tm + SUBLANES, tf), F32)],
        compiler_params=_params("arbitrary", "arbitrary"),
        name="ffn_prompt",
    )(h2, h2, wg, wu, wd, conv_w, conv_b, x1, final_norm)


def _ffn_sample_body(h_ref, st_ref, wg_ref, wu_ref, wd_ref, cw_ref, cb_ref, x1_ref, fn_ref,
                     y_ref, new_st_ref, acc_ref, ext_ref, *, bd):
    f = pl.program_id(0)
    n = h_ref.shape[0]
    keep = (CONV_W - 1) * bd
    _ffn_zero_acc(f, acc_ref)
    ext_ref[0:keep, :] = st_ref[...]
    taps = tuple((CONV_W - 1 - j) * bd for j in range(CONV_W))
    _ffn_rows(h_ref[...], keep, 0, n, taps, wg_ref, wu_ref, wd_ref, cw_ref, cb_ref, ext_ref, acc_ref)
    new_st_ref[...] = ext_ref[n:n + keep, :]
    _ffn_finish(f, pl.num_programs(0), x1_ref, fn_ref, y_ref, acc_ref)


def _ffn_sample(h2, x1, state, wg, wu, wd, conv_w, conv_b, final_norm, bd, tf):
    n, d = x1.shape
    ff = wg.shape[1]
    keep = (CONV_W - 1) * bd
    assert bd % SUBLANES == 0 and n >= keep
    return pl.pallas_call(
        functools.partial(_ffn_sample_body, bd=bd),
        grid=(ff // tf,),
        in_specs=[
            pl.BlockSpec((n, d), lambda f: (0, 0)),
            pl.BlockSpec((keep, tf), lambda f: (0, f)),
            pl.BlockSpec((d, tf), lambda f: (0, f)),
            pl.BlockSpec((d, tf), lambda f: (0, f)),
            pl.BlockSpec((tf, d), lambda f: (f, 0)),
            pl.BlockSpec((CONV_W, tf), lambda f: (0, f)),
            pl.BlockSpec((1, tf), lambda f: (0, f)),
            pl.BlockSpec((n, d), lambda f: (0, 0)),
            pl.BlockSpec((1, d), lambda f: (0, 0)),
        ],
        out_specs=[pl.BlockSpec((n, d), lambda f: (0, 0)), pl.BlockSpec((keep, tf), lambda f: (0, f))],
        out_shape=[jax.ShapeDtypeStruct((n, d), F32), jax.ShapeDtypeStruct((keep, ff), F32)],
        scratch_shapes=[pltpu.VMEM((n, d), F32), pltpu.VMEM((keep + n, tf), F32)],
        compiler_params=_params("arbitrary"),
        name="ffn_sample",
    )(h2, state, wg, wu, wd, conv_w, conv_b, x1, final_norm)


def _largest_divisor(n, candidates):
    for c in candidates:
        if n % c == 0:
            return c
    raise ValueError(f"no tile for {n} among {candidates}")


def kernel(x_prompt, x_sample, cache_dsa_k, cache_dsa_v, cache_idx_k, cache_fox_k, cache_fox_v, cache_fox_logf, state_ffn_conv, page_table, attn_norm, w_in, b_forget, w_branch_a, w_branch_b, w_out, ffn_norm, w_gate, w_up, w_down, conv_w, conv_b, final_norm):
    b, s, d = x_prompt.shape
    bd, t_new, _ = x_sample.shape
    depth = attn_norm.shape[0]
    assert depth == 1 and t_new * GROUP == SUBLANES and s % Q_BLOCK == 0
    n_pages = page_table.shape[1]
    past = n_pages * PAGE
    n_pool = cache_dsa_k.shape[1]
    ff = w_gate.shape[2]
    lay = _Layout(d)

    w_perm = _prep_w_in_t(jnp.swapaxes(w_in[0], 0, 1), lay)
    bf_row = jnp.zeros((1, LANES), F32).at[0, LOGF_LANE:LOGF_LANE + N_HEADS].set(b_forget[0])
    wa, wb, wo = (w[0].astype(BF16) for w in (w_branch_a, w_branch_b, w_out))
    wg, wu, wd = (w[0].astype(BF16) for w in (w_gate, w_up, w_down))
    g_attn, g_ffn, g_fin = attn_norm[0][None, :], ffn_norm[0][None, :], final_norm[None, :]
    cw, cb = conv_w[0], conv_b[0][None, :]

    tm_p = _largest_divisor(s, (1024, 512, 256, 128))
    tab_p = _rope_tables(jnp.arange(s, dtype=I32))
    tab_s = jnp.tile(_rope_tables(past + jnp.arange(t_new, dtype=I32)), (bd, 1))
    xp2 = x_prompt.reshape(b * s, d)
    xs2 = x_sample.reshape(bd * t_new, d)
    proj_p, small_p, *kv_p = _norm_proj(xp2, g_attn, w_perm, tab_p, bf_row, lay, tm_p)
    proj_s, small_s, *kv_s = _norm_proj(xs2, g_attn, w_perm, tab_s, bf_row, lay, bd * t_new)

    oa_p = _dsa_prompt(proj_p, small_p, kv_p[0], kv_p[1], b, s, lay)
    ob_p = _fox_prompt(proj_p, small_p, kv_p[2], kv_p[3], b, s, lay)

    def cols(name, width):
        o = getattr(lay, name)
        return proj_s[:, o:o + width].astype(F32).reshape(bd, t_new, width)

    def heads_major(x):
        x = x.reshape(bd, t_new, N_KV, GROUP, HEAD_DIM).transpose(0, 2, 3, 1, 4)
        return x.reshape(bd, Q_ROWS, HEAD_DIM)

    def new_kv(x):
        x = x.reshape(bd, t_new, N_KV, HEAD_DIM).transpose(0, 2, 1, 3)
        return jnp.pad(x, ((0, 0), (0, 0), (0, LANES - t_new), (0, 0)))

    def heads_back(o):
        o = o.reshape(bd, N_KV, GROUP, t_new, HEAD_DIM).transpose(0, 3, 1, 2, 4)
        return o.reshape(bd * t_new, N_HEADS * HEAD_DIM)

    ka_s, va_s, kb_s, vb_s = (x.reshape(bd, t_new, N_KV * HEAD_DIM) for x in kv_s)
    ik_s = small_s[:, lay.small_ik:lay.small_ik + IDX_DIM].reshape(bd, t_new, IDX_DIM)
    iwf_s = small_s[:, lay.small_iwf:lay.small_iwf + LANES].reshape(bd, t_new, LANES)
    logf_s = iwf_s[..., LOGF_LANE:LOGF_LANE + N_HEADS]
    iq_s = cols("iq", 1024).reshape(bd, t_new, IDX_HEADS, IDX_DIM).transpose(0, 2, 1, 3)
    iq2 = jnp.broadcast_to(iq_s[:, :, None], (bd, IDX_HEADS, GROUP, t_new, IDX_DIM)).reshape(bd, IDX_HEADS * SUBLANES, IDX_DIM)
    iw_s = iwf_s[..., IW_LANE:IW_LANE + IDX_HEADS].transpose(0, 2, 1)
    iw2 = jnp.broadcast_to(iw_s[:, :, None], (bd, IDX_HEADS, GROUP, t_new)).reshape(bd, IDX_HEADS * SUBLANES, 1)
    ikt_new = jnp.pad(ik_s.transpose(0, 2, 1), ((0, 0), (0, 0), (0, LANES - t_new)))
    lft_new = jnp.pad(logf_s.transpose(0, 2, 1), ((0, 0), (0, 0), (0, LANES - t_new)))

    ik_cache_t = jnp.swapaxes(cache_idx_k[0], 1, 2)
    lf_cache_t = jnp.swapaxes(cache_fox_logf[0], 1, 2)
    kv_rows = lambda c: c.reshape(n_pool, PAGE * N_KV, HEAD_DIM)

    scores = _dsa_sample_keys(page_table, iq2, iw2, ikt_new, ik_cache_t, t_new)
    scores, thr = _dsa_sample_select(scores, min(TOPK_MAX, (past + t_new) // 4), _largest_divisor(bd, (8, 4, 2, 1)))
    oa_s = heads_back(_dsa_sample_attend(page_table, heads_major(cols("qa", 1024)), scores, thr,
                                         new_kv(ka_s), new_kv(va_s), kv_rows(cache_dsa_k), kv_rows(cache_dsa_v)))
    ob_s = heads_back(_fox_sample(page_table, heads_major(cols("qb", 1024)), lft_new, new_kv(kb_s), new_kv(vb_s),
                                  lf_cache_t, kv_rows(cache_fox_k), kv_rows(cache_fox_v), t_new))

    tm_m = _largest_divisor(s, (256, 128))
    x1_p, h2_p = _merge(xp2, oa_p, ob_p, proj_p, wa, wb, wo, g_ffn, lay, tm_m)
    x1_s, h2_s = _merge(xs2, oa_s, ob_s, proj_s, wa, wb, wo, g_ffn, lay, bd * t_new)

    tf = _largest_divisor(ff, (512, 256, 128))
    tm_f = _largest_divisor(s, (512, 256, 128))
    y_p, tails = _ffn_prompt(h2_p, x1_p, wg, wu, wd, cw, cb, g_fin, s, tm_f, tf)
    conv_p = tails.reshape(b, s // tm_f, SUBLANES, ff)[:, -1, SUBLANES - (CONV_W - 1):, :]

    t_major = lambda x: x.reshape(bd, t_new, -1).transpose(1, 0, 2).reshape(t_new * bd, -1)
    state_t = state_ffn_conv[0].transpose(1, 0, 2).reshape((CONV_W - 1) * bd, ff)
    y_s_t, st_t = _ffn_sample(t_major(h2_s), t_major(x1_s), state_t, wg, wu, wd, cw, cb, g_fin, bd, tf)
    y_s = y_s_t.reshape(t_new, bd, d).transpose(1, 0, 2)
    conv_s = st_t.reshape(CONV_W - 1, bd, ff).transpose(1, 0, 2)

    p_kv = [x.reshape(1, b, s, N_KV, HEAD_DIM) for x in kv_p]
    logf_at = lay.small_iwf + LOGF_LANE
    p_out = (p_kv[0], p_kv[1], small_p[:, lay.small_ik:lay.small_ik + IDX_DIM].reshape(1, b, s, IDX_DIM), p_kv[2], p_kv[3],
             small_p[:, logf_at:logf_at + N_HEADS].reshape(1, b, s, N_HEADS),
             conv_p[None])
    s_out = (ka_s.reshape(1, bd, t_new, N_KV, HEAD_DIM), va_s.reshape(1, bd, t_new, N_KV, HEAD_DIM), ik_s[None],
             kb_s.reshape(1, bd, t_new, N_KV, HEAD_DIM), vb_s.reshape(1, bd, t_new, N_KV, HEAD_DIM), logf_s[None],
             conv_s[None])
    return (y_p.reshape(b, s, d), y_s) + p_out + s_out
```

```python
import functools

import numpy as np
import jax
import jax.numpy as jnp
from jax import lax
from jax.experimental import pallas as pl
from jax.experimental.pallas import tpu as pltpu

F32 = jnp.float32
BF16 = jnp.bfloat16
I32 = jnp.int32

HEAD_DIM = 128
N_HEADS = 8
N_KV = 4
GROUP = N_HEADS // N_KV
IDX_HEADS = 16
IDX_DIM = 64
TOPK_MAX = 256
ROPE_THETA = 500000.0
ROT_DIM = HEAD_DIM // 4
IDX_ROT_DIM = IDX_DIM // 4
PAGE = 128
Q_BLOCK = 128
CONV_W = 3
RMS_EPS = 1e-6
ATT_SCALE = HEAD_DIM ** -0.5
IDX_SCALE = (IDX_HEADS * IDX_DIM) ** -0.5

LANES = 128
SUBLANES = 8
NEG = -1e30
NEG_INF = float("-inf")
INT_MIN = -2 ** 31
VMEM_LIMIT = 56 * 1024 * 1024

IW_LANE = 0
LOGF_LANE = IDX_HEADS

KV_NAMES = ("ka", "va", "kb", "vb")

NT_DIMS = (((1,), (1,)), ((), ()))


def _params(*sem):
    return pltpu.CompilerParams(dimension_semantics=sem, vmem_limit_bytes=VMEM_LIMIT)


def _nt(a, b):
    return lax.dot_general(a, b, NT_DIMS, preferred_element_type=F32)


def _rms(x, g):
    ms = jnp.mean(x * x, axis=-1, keepdims=True)
    return (x * lax.rsqrt(ms + RMS_EPS)) * g


class _Layout:
    def __init__(self, d_model):
        self.d = d_model
        self.tn = 512
        off = 0
        for name, size in (("ga", d_model), ("gb", d_model), ("qa", 1024), ("iq", 1024), ("qb", 1024)):
            assert off % size == 0, (name, off, size)
            setattr(self, name, off)
            off += size
        assert off % self.tn == 0
        self.proj_cols = off
        self.ik, self.iwf = off, off + LANES
        self.small_ik, self.small_iwf = 0, LANES
        off += self.tn
        for name in KV_NAMES:
            setattr(self, name, off)
            off += N_KV * HEAD_DIM
        assert N_KV * HEAD_DIM == self.tn
        self.nc = off

    def chunk_kinds(self):
        kinds = ["plain"] * (self.nc // LANES)
        for name, size, kind in (("qa", 1024, "qrope"), ("qb", 1024, "q"), ("iq", 1024, "rope64"),
                                 ("ik", self.tn, "s:plain"), ("ik", LANES, "s:rope64"), ("iwf", LANES, "s:iwf"),
                                 ("ka", 512, "kv0rope"), ("va", 512, "kv1"), ("kb", 512, "kv2"), ("vb", 512, "kv3")):
            start = getattr(self, name) // LANES
            for c in range(size // LANES):
                kinds[start + c] = kind
        return kinds


def _w_in_plan(lay):
    d = lay.d
    sizes = (1024, 512, 512, 1024, IDX_DIM, IDX_HEADS, 1024, 512, 512, N_HEADS, d, d)
    names = ("qa", "ka", "va", "iq", "ik", "iw", "qb", "kb", "vb", "fl", "ga", "gb")
    offs = np.concatenate([[0], np.cumsum(sizes)])
    src = {n: int(offs[k]) for k, n in enumerate(names)}
    plan = [()] * (lay.nc // LANES)
    for name, size in (("ga", d), ("gb", d), ("qa", 1024), ("iq", 1024), ("qb", 1024),
                       ("ka", 512), ("va", 512), ("kb", 512), ("vb", 512)):
        for c in range(size // LANES):
            plan[getattr(lay, name) // LANES + c] = ((src[name] + c * LANES, 0, LANES),)
    plan[lay.ik // LANES] = ((src["ik"], 0, IDX_DIM),)
    plan[lay.iwf // LANES] = ((src["iw"], IW_LANE, IDX_HEADS), (src["fl"], LOGF_LANE, N_HEADS))
    return tuple(plan), int(offs[-1])


def _prep_w_body(w_ref, o_ref, *, plan):
    cols = w_ref.shape[1]
    for c, pieces in enumerate(plan):
        parts, pos = [], 0
        for first, at, height in pieces:
            if at > pos:
                parts.append(jnp.zeros((at - pos, cols), F32))
            parts.append(w_ref[first:first + height, :])
            pos = at + height
        if pos < LANES:
            parts.append(jnp.zeros((LANES - pos, cols), F32))
        chunk = parts[0] if len(parts) == 1 else jnp.concatenate(parts, axis=0)
        o_ref[c * LANES:(c + 1) * LANES, :] = chunk.astype(BF16)


def _prep_w_in_t(w_in_t, lay):
    n_src, d = w_in_t.shape
    plan, n_cols = _w_in_plan(lay)
    assert n_cols == n_src and all(f % SUBLANES == 0 and a % SUBLANES == 0 for p in plan for f, a, _ in p)
    tc = _largest_divisor(d, (256, 128))
    return pl.pallas_call(
        functools.partial(_prep_w_body, plan=plan),
        grid=(d // tc,),
        in_specs=[pl.BlockSpec((n_src, tc), lambda i: (0, i))],
        out_specs=pl.BlockSpec((lay.nc, tc), lambda i: (0, i)),
        out_shape=jax.ShapeDtypeStruct((lay.nc, d), BF16),
        compiler_params=_params("arbitrary"),
        name="prep_w_in",
    )(w_in_t)


def _rope_tables(pos):
    def one(rot_dim, period):
        half = rot_dim // 2
        inv_freq = jnp.power(ROPE_THETA, -jnp.arange(half, dtype=F32) * (2.0 / rot_dim))
        ang = pos.astype(F32)[:, None] * inv_freq[None, :]
        cos, sin = jnp.cos(ang), jnp.sin(ang)
        n = pos.shape[0]
        c = jnp.concatenate([cos, cos, jnp.ones((n, period - rot_dim), F32)], axis=1)
        sa = jnp.concatenate([-sin, jnp.zeros((n, period - half), F32)], axis=1)
        sb = jnp.concatenate([jnp.zeros((n, half), F32), sin, jnp.zeros((n, period - rot_dim), F32)], axis=1)
        rep = LANES // period
        return [jnp.tile(t, (1, rep)) for t in (c, sa, sb)]
    return jnp.concatenate(one(ROT_DIM, HEAD_DIM) + one(IDX_ROT_DIM, IDX_DIM), axis=1)


def _proj_body(x_ref, g_ref, w_ref, tab_ref, bf_ref, o_ref, small_ref, *rest, tile_kinds):
    kv_refs, h_ref = rest[:len(KV_NAMES)], rest[len(KV_NAMES)]
    j = pl.program_id(1)
    tm = x_ref.shape[0]

    @pl.when(j == 0)
    def _():
        h_ref[...] = _rms(x_ref[...], g_ref[...]).astype(BF16)

    acc = _nt(h_ref[...], w_ref[...])

    def rope(a, base, half):
        c = tab_ref[:, base:base + LANES]
        sa = tab_ref[:, base + LANES:base + 2 * LANES]
        sb = tab_ref[:, base + 2 * LANES:base + 3 * LANES]
        return a * c + pltpu.roll(a, LANES - half, 1) * sa + pltpu.roll(a, half, 1) * sb

    def indexer_weight_and_log_forget(a):
        z = a + bf_ref[...]
        ls = jnp.minimum(z, 0.0) - jnp.log1p(jnp.exp(-jnp.abs(z)))
        lane = lax.broadcasted_iota(I32, a.shape, 1)
        return jnp.where(lane < LOGF_LANE, a * IDX_SCALE, jnp.where(lane < LOGF_LANE + N_HEADS, ls, a))

    def emit(kinds):
        for c, kind in enumerate(kinds):
            a = acc[:, c * LANES:(c + 1) * LANES]
            to_small = kind.startswith("s:")
            kind = kind[2:] if to_small else kind
            if kind == "qrope":
                a = rope(a, 0, ROT_DIM // 2) * ATT_SCALE
            elif kind == "q":
                a = a * ATT_SCALE
            elif kind == "rope64":
                a = rope(a, 3 * LANES, IDX_ROT_DIM // 2)
            elif kind == "iwf":
                a = indexer_weight_and_log_forget(a)
            if kind.startswith("kv"):
                if kind.endswith("rope"):
                    a = rope(a, 0, ROT_DIM // 2)
                kv_refs[int(kind[2])][pl.ds(c, tm, stride=N_KV), :] = a
            elif to_small:
                small_ref[:, c * LANES:(c + 1) * LANES] = a
            else:
                o_ref[:, c * LANES:(c + 1) * LANES] = a.astype(BF16)

    groups = {}
    for t, kinds in enumerate(tile_kinds):
        groups.setdefault(kinds, []).append(t)
    for kinds, tiles in groups.items():
        cond = functools.reduce(jnp.logical_or, [j == t for t in tiles])
        if all(k == "plain" for k in kinds):
            @pl.when(cond)
            def _():
                o_ref[...] = acc.astype(BF16)
        else:
            pl.when(cond)(functools.partial(emit, kinds))


def _norm_proj(x2d, gamma, w_perm, tab, bf_row, lay, tm):
    n, d = x2d.shape
    tn = lay.tn
    kinds = lay.chunk_kinds()
    per = tn // LANES
    tile_kinds = tuple(tuple(kinds[t * per:(t + 1) * per]) for t in range(lay.nc // tn))
    tab_blocks = tab.shape[0] // tm
    last_proj_tile = lay.proj_cols // tn - 1
    kv_spec = pl.BlockSpec((tm * N_KV, HEAD_DIM), lambda i, j: (i, 0))
    return pl.pallas_call(
        functools.partial(_proj_body, tile_kinds=tile_kinds),
        grid=(n // tm, lay.nc // tn),
        in_specs=[
            pl.BlockSpec((tm, d), lambda i, j: (i, 0), pipeline_mode=pl.Buffered(1)),
            pl.BlockSpec((1, d), lambda i, j: (0, 0)),
            pl.BlockSpec((tn, d), lambda i, j: (j, 0)),
            pl.BlockSpec((tm, 6 * LANES), lambda i, j: (i % tab_blocks, 0)),
            pl.BlockSpec((1, LANES), lambda i, j: (0, 0)),
        ],
        out_specs=[pl.BlockSpec((tm, tn), lambda i, j: (i, jnp.minimum(j, last_proj_tile))),
                   pl.BlockSpec((tm, tn), lambda i, j: (i, 0))] + [kv_spec] * len(KV_NAMES),
        out_shape=[jax.ShapeDtypeStruct((n, lay.proj_cols), BF16), jax.ShapeDtypeStruct((n, tn), F32)]
        + [jax.ShapeDtypeStruct((n * N_KV, HEAD_DIM), F32)] * len(KV_NAMES),
        scratch_shapes=[pltpu.VMEM((tm, d), BF16)],
        compiler_params=_params("arbitrary", "arbitrary"),
        name="norm_proj",
    )(x2d, gamma, w_perm, tab, bf_row)


def _code_to_float(code):
    bits = jnp.where(code < 0, code ^ jnp.int32(0x7FFFFFFF), code)
    return pltpu.bitcast(bits, F32)


def _kth_largest(read_scores, rows, k):
    def body(it, code):
        cand = code + jnp.left_shift(jnp.int32(1), 31 - it)
        cnt = jnp.sum(jnp.where(read_scores() >= _code_to_float(cand), 1.0, 0.0), axis=1, keepdims=True)
        return jnp.where(cnt >= k, cand, code)

    code = lax.fori_loop(0, 32, body, jnp.full((rows, 1), INT_MIN, I32))
    return _code_to_float(code), code == INT_MIN


def _count(mask):
    return jnp.sum(jnp.where(mask, 1.0, 0.0), axis=1, keepdims=True)


def _total(mask):
    ones = jnp.where(mask, 1.0, 0.0)
    return jnp.sum(jnp.sum(ones, axis=0, keepdims=True), axis=1, keepdims=True)[0, 0]


def _earlier_in_chunk():
    r = lax.broadcasted_iota(I32, (LANES, LANES), 0)
    c = lax.broadcasted_iota(I32, (LANES, LANES), 1)
    return jnp.where(r < c, 1.0, 0.0).astype(BF16)


def _tied_keys_to_keep(eq, seen, need, earlier):
    ones = jnp.where(eq, 1.0, 0.0)
    rank = seen + jnp.dot(ones.astype(BF16), earlier, preferred_element_type=F32)
    return eq & (rank < need), seen + jnp.sum(ones, axis=1, keepdims=True)


KEY_BUCKET = 512


def _key_limits(s):
    step = min(KEY_BUCKET, s)
    assert s % step == 0
    return tuple(range(step, s + 1, step))


def _for_causal_limit(i, limits, block_fn):
    q_end = (i + 1) * Q_BLOCK
    prev = 0
    for lim in limits:
        pl.when((q_end > prev) & (q_end <= lim))(functools.partial(block_fn, lim))
        prev = lim


def _load_kv_heads(src_ref, dst_ref):
    tokens = dst_ref.shape[0]
    for kh in range(N_KV):
        dst_ref[:, kh * HEAD_DIM:(kh + 1) * HEAD_DIM] = src_ref[pl.ds(kh, tokens, stride=N_KV), :].astype(BF16)


def _attend_heads(q_ref, kb_ref, vb_ref, o_ref, n, logit_bias):
    for kh in range(N_KV):
        kk = kb_ref[0:n, kh * HEAD_DIM:(kh + 1) * HEAD_DIM]
        vv = vb_ref[0:n, kh * HEAD_DIM:(kh + 1) * HEAD_DIM]
        for g in range(GROUP):
            h = kh * GROUP + g
            q = q_ref[:, h * HEAD_DIM:(h + 1) * HEAD_DIM]
            lg = _nt(q, kk) + logit_bias(h)
            m = jnp.max(lg, axis=1, keepdims=True)
            e = jnp.exp(lg - m)
            l = jnp.sum(e, axis=1, keepdims=True)
            o = jnp.dot(e.astype(BF16), vv, preferred_element_type=F32)
            o_ref[:, h * HEAD_DIM:(h + 1) * HEAD_DIM] = o / l


def _dsa_prompt_body(iq_ref, iwf_ref, ik_ref, q_ref, k_ref, v_ref, o_ref,
                     ikb_ref, kb_ref, vb_ref, sc_ref, bias_ref, *, topk, limits):
    i = pl.program_id(1)

    @pl.when(i == 0)
    def _():
        ikb_ref[...] = ik_ref[:, :IDX_DIM].astype(BF16)
        _load_kv_heads(k_ref, kb_ref)
        _load_kv_heads(v_ref, vb_ref)

    def block(n):
        ikb = ikb_ref[0:n, :]
        for h in range(IDX_HEADS):
            qh = iq_ref[:, h * IDX_DIM:(h + 1) * IDX_DIM]
            s = jnp.maximum(_nt(qh, ikb), 0.0) * iwf_ref[:, IW_LANE + h:IW_LANE + h + 1]
            if h == 0:
                sc_ref[:, 0:n] = s
            else:
                sc_ref[:, 0:n] += s
        row = i * Q_BLOCK + lax.broadcasted_iota(I32, (Q_BLOCK, n), 0)
        col = lax.broadcasted_iota(I32, (Q_BLOCK, n), 1)
        causal = col <= row
        sc_ref[:, 0:n] = jnp.where(causal, sc_ref[:, 0:n], NEG_INF)
        thr, none = _kth_largest(lambda: sc_ref[:, 0:n], Q_BLOCK, topk)
        chosen = ((sc_ref[:, 0:n] >= thr) | none) & causal
        bias_ref[:, 0:n] = jnp.where(chosen, 0.0, NEG)

        first = i * Q_BLOCK
        short = jnp.clip(topk - first, 0, Q_BLOCK)
        expected = short * first + (short * (short + 1)) // 2 + (Q_BLOCK - short) * topk

        @pl.when(_total(chosen) > expected.astype(F32))
        def _():
            need = topk - _count((sc_ref[:, 0:n] > thr) & causal)
            earlier = _earlier_in_chunk()
            seen = jnp.zeros((Q_BLOCK, 1), F32)
            for c in range(n // LANES):
                lanes = slice(c * LANES, (c + 1) * LANES)
                sc = sc_ref[:, lanes]
                ok = (c * LANES + lax.broadcasted_iota(I32, (Q_BLOCK, LANES), 1)
                      <= first + lax.broadcasted_iota(I32, (Q_BLOCK, LANES), 0))
                keep, seen = _tied_keys_to_keep((sc == thr) & ok, seen, need, earlier)
                bias_ref[:, lanes] = jnp.where((((sc > thr) | none) & ok) | keep, 0.0, NEG)
        _attend_heads(q_ref, kb_ref, vb_ref, o_ref, n, lambda h: bias_ref[:, 0:n])

    _for_causal_limit(i, limits, block)


def _dsa_prompt(proj, small, k4, v4, b, s, lay):
    nb = s // Q_BLOCK
    topk = min(TOPK_MAX, s // 4)
    row = lambda bb, i: bb * nb + i
    return pl.pallas_call(
        functools.partial(_dsa_prompt_body, topk=topk, limits=_key_limits(s)),
        grid=(b, nb),
        in_specs=[
            pl.BlockSpec((Q_BLOCK, 1024), lambda bb, i: (row(bb, i), lay.iq // 1024)),
            pl.BlockSpec((Q_BLOCK, LANES), lambda bb, i: (row(bb, i), lay.small_iwf // LANES)),
            pl.BlockSpec((s, LANES), lambda bb, i: (bb, lay.small_ik // LANES)),
            pl.BlockSpec((Q_BLOCK, 1024), lambda bb, i: (row(bb, i), lay.qa // 1024)),
            pl.BlockSpec((s * N_KV, HEAD_DIM), lambda bb, i: (bb, 0)),
            pl.BlockSpec((s * N_KV, HEAD_DIM), lambda bb, i: (bb, 0)),
        ],
        out_specs=pl.BlockSpec((Q_BLOCK, 1024), lambda bb, i: (row(bb, i), 0)),
        out_shape=jax.ShapeDtypeStruct((b * s, 1024), F32),
        scratch_shapes=[pltpu.VMEM((s, IDX_DIM), BF16), pltpu.VMEM((s, 512), BF16), pltpu.VMEM((s, 512), BF16),
                        pltpu.VMEM((Q_BLOCK, s), F32), pltpu.VMEM((Q_BLOCK, s), F32)],
        compiler_params=_params("arbitrary", "arbitrary"),
        name="dsa_prompt",
    )(proj, small, small, proj, k4, v4)


CUM_BLOCK = 256


def _fox_prompt_body(q_ref, k_ref, v_ref, lf_ref, o_ref, kb_ref, vb_ref, c_ref, ct_ref, bias_ref, *, limits):
    i = pl.program_id(1)
    s_len = kb_ref.shape[0]

    @pl.when(i == 0)
    def _():
        _load_kv_heads(k_ref, kb_ref)
        _load_kv_heads(v_ref, vb_ref)
        r = lax.broadcasted_iota(I32, (CUM_BLOCK, CUM_BLOCK), 0)
        c = lax.broadcasted_iota(I32, (CUM_BLOCK, CUM_BLOCK), 1)
        tri = jnp.where(c <= r, 1.0, 0.0).astype(F32)
        carry = jnp.zeros((1, LANES), F32)
        for blk in range(s_len // CUM_BLOCK):
            xb = lf_ref[blk * CUM_BLOCK:(blk + 1) * CUM_BLOCK, :]
            cb = jnp.dot(tri, xb, precision=lax.Precision.HIGHEST, preferred_element_type=F32) + carry
            c_ref[blk * CUM_BLOCK:(blk + 1) * CUM_BLOCK, :] = cb
            carry = cb[CUM_BLOCK - 1:CUM_BLOCK, :]
        ct_ref[...] = c_ref[...].T

    start = pl.multiple_of(i * Q_BLOCK, Q_BLOCK)

    def block(n):
        row = i * Q_BLOCK + lax.broadcasted_iota(I32, (Q_BLOCK, n), 0)
        col = lax.broadcasted_iota(I32, (Q_BLOCK, n), 1)
        bias_ref[:, 0:n] = jnp.where(col <= row, 0.0, NEG)

        def logit_bias(h):
            cq = c_ref[pl.ds(start, Q_BLOCK), LOGF_LANE + h:LOGF_LANE + h + 1]
            ck = ct_ref[LOGF_LANE + h:LOGF_LANE + h + 1, 0:n]
            return (cq - ck) + bias_ref[:, 0:n]

        _attend_heads(q_ref, kb_ref, vb_ref, o_ref, n, logit_bias)

    _for_causal_limit(i, limits, block)


def _fox_prompt(proj, small, k4, v4, b, s, lay):
    nb = s // Q_BLOCK
    assert s % CUM_BLOCK == 0
    row = lambda bb, i: bb * nb + i
    return pl.pallas_call(
        functools.partial(_fox_prompt_body, limits=_key_limits(s)),
        grid=(b, nb),
        in_specs=[
            pl.BlockSpec((Q_BLOCK, 1024), lambda bb, i: (row(bb, i), lay.qb // 1024)),
            pl.BlockSpec((s * N_KV, HEAD_DIM), lambda bb, i: (bb, 0)),
            pl.BlockSpec((s * N_KV, HEAD_DIM), lambda bb, i: (bb, 0)),
            pl.BlockSpec((s, LANES), lambda bb, i: (bb, lay.small_iwf // LANES)),
        ],
        out_specs=pl.BlockSpec((Q_BLOCK, 1024), lambda bb, i: (row(bb, i), 0)),
        out_shape=jax.ShapeDtypeStruct((b * s, 1024), F32),
        scratch_shapes=[pltpu.VMEM((s, 512), BF16), pltpu.VMEM((s, 512), BF16),
                        pltpu.VMEM((s, LANES), F32), pltpu.VMEM((LANES, s), F32), pltpu.VMEM((Q_BLOCK, s), F32)],
        compiler_params=_params("arbitrary", "arbitrary"),
        name="fox_prompt",
    )(proj, k4, v4, small)


Q_ROWS = N_KV * SUBLANES


def _pages_per_step(n_pages, most=16):
    for pp in (32, 16, 8, 4, 2, 1):
        if pp <= most and n_pages % pp == 0:
            return pp


def _page_specs(block, pp, page_of):
    def spec(r):
        return pl.BlockSpec((None,) + block, lambda bb, c, pt: (pt[bb, page_of(c, r)],) + (0,) * len(block))
    return [spec(r) for r in range(pp)]


def _per_seq(shape):
    return pl.BlockSpec((None,) + shape, lambda bb, c, pt: (bb,) + (0,) * len(shape))


def _shared(shape):
    return pl.BlockSpec(shape, lambda bb, c, pt: (0,) * len(shape))


def _kv_rows(pages, kh):
    return jnp.concatenate([p[pl.ds(kh, PAGE, stride=N_KV), :] for p in pages], axis=0).astype(BF16)


def _softmax_update(q_ref, keys_of, values_of, bias, sel, m_ref, l_ref, acc_ref):
    lg = jnp.concatenate([_nt(_q_rows(q_ref, kh), keys_of(kh)) for kh in range(N_KV)], axis=0)
    if bias is not None:
        lg = lg + bias
    if sel is not None:
        lg = jnp.where(sel, lg, NEG)
    m_old = m_ref[...]
    m_new = jnp.maximum(m_old, jnp.max(lg, axis=1, keepdims=True))
    corr = jnp.exp(m_old - m_new)
    e = jnp.exp(lg - m_new)
    if sel is not None:
        e = jnp.where(sel, e, 0.0)
    l_ref[...] = l_ref[...] * corr + jnp.sum(e, axis=1, keepdims=True)
    pv = jnp.concatenate([jnp.dot(e[kh * SUBLANES:(kh + 1) * SUBLANES].astype(BF16), values_of(kh),
                                  preferred_element_type=F32) for kh in range(N_KV)], axis=0)
    acc_ref[...] = acc_ref[...] * corr + pv
    m_ref[...] = m_new


def _softmax_init(m_ref, l_ref, acc_ref):
    m_ref[...] = jnp.full(m_ref.shape, NEG, F32)
    l_ref[...] = jnp.zeros(l_ref.shape, F32)
    acc_ref[...] = jnp.zeros(acc_ref.shape, F32)


def _softmax_scratch():
    return [pltpu.VMEM((Q_ROWS, 1), F32), pltpu.VMEM((Q_ROWS, 1), F32), pltpu.VMEM((Q_ROWS, HEAD_DIM), F32)]


def _q_rows(q_ref, kh):
    return q_ref[kh * SUBLANES:(kh + 1) * SUBLANES, :].astype(BF16)


def _dsa_sample_keys_body(pt_ref, iq_ref, iw_ref, ikn_ref, *rest, pp, n_new):
    pages, sc_ref = rest[:pp], rest[pp]
    c = pl.program_id(1)
    past = sc_ref.shape[1] - LANES
    q = iq_ref[...].astype(BF16)
    w = iw_ref[...]

    def scores_of(ikt):
        s = jnp.maximum(jnp.dot(q, ikt.astype(BF16), preferred_element_type=F32), 0.0) * w
        acc = s[0:SUBLANES]
        for h in range(1, IDX_HEADS):
            acc = acc + s[h * SUBLANES:(h + 1) * SUBLANES]
        return acc

    sc_ref[:, pl.ds(pl.multiple_of(c * (pp * PAGE), pp * PAGE), pp * PAGE)] = scores_of(
        jnp.concatenate([p[...] for p in pages], axis=1))

    @pl.when(c == 0)
    def _():
        row = lax.broadcasted_iota(I32, (SUBLANES, LANES), 0)
        lane = lax.broadcasted_iota(I32, (SUBLANES, LANES), 1)
        sc_ref[:, past:past + LANES] = jnp.where(lane <= row % n_new, scores_of(ikn_ref[...]), NEG_INF)


def _dsa_sample_keys(page_table, iq2, iw2, ikt_new, ik_cache_t, n_new):
    bd, n_pages = page_table.shape
    pp = _pages_per_step(n_pages, 32)
    past = n_pages * PAGE
    return pl.pallas_call(
        functools.partial(_dsa_sample_keys_body, pp=pp, n_new=n_new),
        grid_spec=pltpu.PrefetchScalarGridSpec(
            num_scalar_prefetch=1,
            grid=(bd, n_pages // pp),
            in_specs=[_per_seq((IDX_HEADS * SUBLANES, IDX_DIM)), _per_seq((IDX_HEADS * SUBLANES, 1)),
                      _per_seq((IDX_DIM, LANES))]
            + _page_specs((IDX_DIM, PAGE), pp, lambda c, r: c * pp + r),
            out_specs=pl.BlockSpec((SUBLANES, past + LANES), lambda bb, c, pt: (bb, 0)),
        ),
        out_shape=jax.ShapeDtypeStruct((bd * SUBLANES, past + LANES), F32),
        compiler_params=_params("arbitrary", "arbitrary"),
        name="dsa_sample_keys",
    )(page_table, iq2, iw2, ikt_new, *([ik_cache_t] * pp))


def _dsa_sample_select_body(sc_ref, sel_ref, thr_ref, *, topk):
    rows, n = sc_ref.shape
    thr, _ = _kth_largest(lambda: sc_ref[...], rows, topk)
    thr_ref[...] = jnp.broadcast_to(thr, thr_ref.shape)
    sel_ref[...] = sc_ref[...]

    @pl.when(jnp.max(_count(sc_ref[...] >= thr)) > topk)
    def _():
        need = topk - _count(sc_ref[...] > thr)
        earlier = _earlier_in_chunk()

        def chunk(c, seen):
            lanes = pl.ds(pl.multiple_of(c * LANES, LANES), LANES)
            sc = sc_ref[:, lanes]
            eq = sc == thr
            keep, seen = _tied_keys_to_keep(eq, seen, need, earlier)
            sel_ref[:, lanes] = jnp.where(eq & jnp.logical_not(keep), NEG_INF, sc)
            return seen

        lax.fori_loop(0, n // LANES, chunk, jnp.zeros((rows, 1), F32))


def _dsa_sample_select(scores, topk, seqs_per_step):
    rows_total, n = scores.shape
    rows = seqs_per_step * SUBLANES
    return pl.pallas_call(
        functools.partial(_dsa_sample_select_body, topk=topk),
        grid=(rows_total // rows,),
        in_specs=[pl.BlockSpec((rows, n), lambda i: (i, 0))],
        out_specs=[pl.BlockSpec((rows, n), lambda i: (i, 0)), pl.BlockSpec((rows, LANES), lambda i: (i, 0))],
        out_shape=[jax.ShapeDtypeStruct((rows_total, n), F32), jax.ShapeDtypeStruct((rows_total, LANES), F32)],
        compiler_params=_params("arbitrary"),
        name="dsa_sample_select",
    )(scores)


PAGE_SLOTS = 4
PAGES_AHEAD = 3


def _chunk_pages(n_pages):
    for pp in (8, 4, 2, 1):
        if n_pages % pp == 0 and (n_pages // pp) % PAGE_SLOTS == 0:
            return pp
    raise ValueError(f"need a multiple of {PAGE_SLOTS} page chunks, got {n_pages} pages")


def _walk_page_chunks(pt_ref, streams, pp, n_chunks, page_of, compute):
    b = pl.program_id(0)
    assert n_chunks % PAGE_SLOTS == 0 and PAGES_AHEAD < PAGE_SLOTS and PAGES_AHEAD <= n_chunks

    def copies(seq, c, slot, to_wait=False):
        out = []
        for r in range(pp):
            page = 0 if to_wait else pt_ref[seq, page_of(c, r)]
            for hbm, buf, sem in streams:
                out.append(pltpu.make_async_copy(hbm.at[page], buf.at[slot, r], sem.at[slot]))
        return out

    @pl.when(b == 0)
    def _():
        for c in range(PAGES_AHEAD):
            for d in copies(b, c, c % PAGE_SLOTS):
                d.start()

    for c in range(n_chunks):
        for d in copies(b, c, c % PAGE_SLOTS, to_wait=True):
            d.wait()
        ahead = c + PAGES_AHEAD
        if ahead < n_chunks:
            for d in copies(b, ahead, ahead % PAGE_SLOTS):
                d.start()
        else:
            @pl.when(b + 1 < pl.num_programs(0))
            def _(ahead=ahead):
                for d in copies(b + 1, ahead - n_chunks, ahead % PAGE_SLOTS):
                    d.start()
        compute(c, c % PAGE_SLOTS)


def _dsa_sample_attend_body(pt_ref, q_ref, sc_ref, thr_ref, knew_ref, vnew_ref, k_hbm, v_hbm, o_ref,
                            kbuf, vbuf, ksem, vsem, m_ref, l_ref, acc_ref, *, pp, n_chunks):
    past = n_chunks * pp * PAGE
    thr = jnp.concatenate([thr_ref[:, 0:1]] * N_KV, axis=0)

    def selected(scores):
        scores = jnp.concatenate([scores] * N_KV, axis=0)
        return (scores >= thr) & (scores > NEG_INF)

    _softmax_init(m_ref, l_ref, acc_ref)
    _softmax_update(q_ref, lambda kh: knew_ref[kh].astype(BF16), lambda kh: vnew_ref[kh].astype(BF16),
                    None, selected(sc_ref[:, past:past + LANES]), m_ref, l_ref, acc_ref)

    def compute(c, slot):
        k_pages = [kbuf.at[slot, r] for r in range(pp)]
        v_pages = [vbuf.at[slot, r] for r in range(pp)]
        sel = selected(sc_ref[:, c * pp * PAGE:(c + 1) * pp * PAGE])
        _softmax_update(q_ref, functools.partial(_kv_rows, k_pages), functools.partial(_kv_rows, v_pages),
                        None, sel, m_ref, l_ref, acc_ref)

    _walk_page_chunks(pt_ref, [(k_hbm, kbuf, ksem), (v_hbm, vbuf, vsem)], pp, n_chunks,
                      lambda c, r: c * pp + r, compute)
    o_ref[...] = acc_ref[...] / l_ref[...]


def _per_seq1(shape):
    return pl.BlockSpec((None,) + shape, lambda bb, pt: (bb,) + (0,) * len(shape))


def _kv_page_scratch(pp):
    page = (PAGE * N_KV, HEAD_DIM)
    return [pltpu.VMEM((PAGE_SLOTS, pp) + page, F32), pltpu.VMEM((PAGE_SLOTS, pp) + page, F32),
            pltpu.SemaphoreType.DMA((PAGE_SLOTS,)), pltpu.SemaphoreType.DMA((PAGE_SLOTS,))]


def _dsa_sample_attend(page_table, q_s, scores, thr, k_new, v_new, k_cache, v_cache):
    bd, n_pages = page_table.shape
    pp = _chunk_pages(n_pages)
    n_cols = scores.shape[1]
    hbm = pl.BlockSpec(memory_space=pl.ANY)
    return pl.pallas_call(
        functools.partial(_dsa_sample_attend_body, pp=pp, n_chunks=n_pages // pp),
        grid_spec=pltpu.PrefetchScalarGridSpec(
            num_scalar_prefetch=1,
            grid=(bd,),
            in_specs=[_per_seq1((Q_ROWS, HEAD_DIM)),
                      pl.BlockSpec((SUBLANES, n_cols), lambda bb, pt: (bb, 0)),
                      pl.BlockSpec((SUBLANES, LANES), lambda bb, pt: (bb, 0)),
                      _per_seq1((N_KV, LANES, HEAD_DIM)), _per_seq1((N_KV, LANES, HEAD_DIM)), hbm, hbm],
            out_specs=_per_seq1((Q_ROWS, HEAD_DIM)),
            scratch_shapes=_kv_page_scratch(pp) + _softmax_scratch(),
        ),
        out_shape=jax.ShapeDtypeStruct((bd, Q_ROWS, HEAD_DIM), F32),
        compiler_params=_params("arbitrary"),
        name="dsa_sample_attend",
    )(page_table, q_s, scores, thr, k_new, v_new, k_cache, v_cache)


def _fox_sample_body(pt_ref, q_ref, lfn_ref, knew_ref, vnew_ref, rep_ref, later_ref, lf_hbm, k_hbm, v_hbm, o_ref,
                     lfbuf, kbuf, vbuf, lfsem, ksem, vsem, m_ref, l_ref, acc_ref, cq_ref, carry_ref,
                     *, pp, n_chunks, n_new):
    hi = lax.Precision.HIGHEST
    n_pages = pp * n_chunks

    _softmax_init(m_ref, l_ref, acc_ref)
    carry_ref[...] = jnp.zeros(carry_ref.shape, F32)
    r_io = lax.broadcasted_iota(I32, (LANES, LANES), 0)
    c_io = lax.broadcasted_iota(I32, (LANES, LANES), 1)
    incl = jnp.where(r_io <= c_io, 1.0, 0.0).astype(F32)
    cum = jnp.dot(lfn_ref[...], incl, precision=hi, preferred_element_type=F32)
    cg = jnp.dot(rep_ref[0:Q_ROWS, 0:N_HEADS], cum, precision=hi, preferred_element_type=F32)
    row = lax.broadcasted_iota(I32, (Q_ROWS, LANES), 0)
    lane = lax.broadcasted_iota(I32, (Q_ROWS, LANES), 1)
    own = lane == row % n_new
    cq_ref[...] = jnp.sum(jnp.where(own, cg, 0.0), axis=1, keepdims=True)
    _softmax_update(q_ref, lambda kh: knew_ref[kh].astype(BF16), lambda kh: vnew_ref[kh].astype(BF16),
                    cq_ref[...] - cg, lane <= row % n_new, m_ref, l_ref, acc_ref)

    def compute(c, slot):
        k_pages = [kbuf.at[slot, r] for r in range(pp)]
        v_pages = [vbuf.at[slot, r] for r in range(pp)]
        lf_all = jnp.concatenate([lfbuf[slot, r] for r in range(pp)], axis=0)
        lf_rows = jnp.dot(rep_ref[...], lf_all, precision=hi, preferred_element_type=F32)
        within = jnp.dot(lf_rows, later_ref[...], precision=hi, preferred_element_type=F32)
        total = within[:, 0:1] + lf_rows[:, 0:1]
        run = carry_ref[...]
        biases = []
        for r in range(pp):
            biases.append(within[r * Q_ROWS:(r + 1) * Q_ROWS] + (run + cq_ref[...]))
            run = run + total[r * Q_ROWS:(r + 1) * Q_ROWS]
        carry_ref[...] = run
        _softmax_update(q_ref, functools.partial(_kv_rows, k_pages), functools.partial(_kv_rows, v_pages),
                        jnp.concatenate(biases, axis=1), None, m_ref, l_ref, acc_ref)

    _walk_page_chunks(pt_ref, [(lf_hbm, lfbuf, lfsem), (k_hbm, kbuf, ksem), (v_hbm, vbuf, vsem)], pp, n_chunks,
                      lambda c, r: n_pages - 1 - (c * pp + r), compute)
    o_ref[...] = acc_ref[...] / l_ref[...]


def _fox_sample(page_table, q_s, lft_new, k_new, v_new, lf_cache_t, k_cache, v_cache, n_new):
    bd, n_pages = page_table.shape
    pp = _chunk_pages(n_pages)
    row_head = np.arange(Q_ROWS) // n_new
    rep_one = (row_head[:, None] == np.arange(N_HEADS)[None, :]).astype(np.float32)
    rep = jnp.asarray(np.kron(np.eye(pp, dtype=np.float32), rep_one))
    later = jnp.asarray((np.arange(PAGE)[:, None] > np.arange(PAGE)[None, :]).astype(np.float32))
    hbm = pl.BlockSpec(memory_space=pl.ANY)
    shared = lambda shape: pl.BlockSpec(shape, lambda bb, pt: (0,) * len(shape))
    return pl.pallas_call(
        functools.partial(_fox_sample_body, pp=pp, n_chunks=n_pages // pp, n_new=n_new),
        grid_spec=pltpu.PrefetchScalarGridSpec(
            num_scalar_prefetch=1,
            grid=(bd,),
            in_specs=[_per_seq1((Q_ROWS, HEAD_DIM)), _per_seq1((N_HEADS, LANES)),
                      _per_seq1((N_KV, LANES, HEAD_DIM)), _per_seq1((N_KV, LANES, HEAD_DIM)),
                      shared((pp * Q_ROWS, pp * N_HEADS)), shared((PAGE, PAGE)), hbm, hbm, hbm],
            out_specs=_per_seq1((Q_ROWS, HEAD_DIM)),
            scratch_shapes=[pltpu.VMEM((PAGE_SLOTS, pp, N_HEADS, PAGE), F32)] + _kv_page_scratch(pp)[:2]
            + [pltpu.SemaphoreType.DMA((PAGE_SLOTS,))] * 3
            + _softmax_scratch() + [pltpu.VMEM((Q_ROWS, 1), F32), pltpu.VMEM((Q_ROWS, 1), F32)],
        ),
        out_shape=jax.ShapeDtypeStruct((bd, Q_ROWS, HEAD_DIM), F32),
        compiler_params=_params("arbitrary"),
        name="fox_sample",
    )(page_table, q_s, lft_new, k_new, v_new, rep, later, lf_cache_t, k_cache, v_cache)


def _merge_body(x_ref, oa_ref, ob_ref, ga_ref, gb_ref, wa_ref, wb_ref, wo_ref, gn_ref, x1_ref, h2_ref):
    a = jnp.dot(oa_ref[...].astype(BF16), wa_ref[...], preferred_element_type=F32)
    b = jnp.dot(ob_ref[...].astype(BF16), wb_ref[...], preferred_element_type=F32)
    merged = jax.nn.sigmoid(ga_ref[...].astype(F32)) * a + jax.nn.sigmoid(gb_ref[...].astype(F32)) * b
    x1 = x_ref[...] + jnp.dot(merged.astype(BF16), wo_ref[...], preferred_element_type=F32)
    x1_ref[...] = x1
    h2_ref[...] = _rms(x1, gn_ref[...]).astype(BF16)


def _merge(x2d, o_a, o_b, proj, wa, wb, wo, ffn_norm, lay, tm):
    n, d = x2d.shape
    const = lambda shape: pl.BlockSpec(shape, lambda i: (0, 0), pipeline_mode=pl.Buffered(1))
    return pl.pallas_call(
        _merge_body,
        grid=(n // tm,),
        in_specs=[
            pl.BlockSpec((tm, d), lambda i: (i, 0)),
            pl.BlockSpec((tm, 1024), lambda i: (i, 0)),
            pl.BlockSpec((tm, 1024), lambda i: (i, 0)),
            pl.BlockSpec((tm, d), lambda i: (i, lay.ga // d)),
            pl.BlockSpec((tm, d), lambda i: (i, lay.gb // d)),
            const((1024, d)), const((1024, d)), const((d, d)), const((1, d)),
        ],
        out_specs=[pl.BlockSpec((tm, d), lambda i: (i, 0)), pl.BlockSpec((tm, d), lambda i: (i, 0))],
        out_shape=[jax.ShapeDtypeStruct((n, d), F32), jax.ShapeDtypeStruct((n, d), BF16)],
        compiler_params=_params("arbitrary"),
        name="merge",
    )(x2d, o_a, o_b, proj, proj, wa, wb, wo, ffn_norm)


FFN_ROW_CHUNKS = 2


def _ffn_zero_acc(f, acc_ref):
    @pl.when(f == 0)
    def _():
        acc_ref[...] = jnp.zeros(acc_ref.shape, F32)


def _ffn_finish(f, n_f, x1_ref, fn_ref, y_ref, acc_ref):
    @pl.when(f == n_f - 1)
    def _():
        y_ref[...] = _rms(x1_ref[...] + acc_ref[...], fn_ref[...])


def _ffn_rows(h, first, rows, n_rows, taps, wg_ref, wu_ref, wd_ref, cw_ref, cb_ref, ext_ref, acc_ref):
    gp = jnp.dot(h, wg_ref[...], preferred_element_type=F32)
    up = jnp.dot(h, wu_ref[...], preferred_element_type=F32)
    ext_ref[first + rows:first + rows + n_rows, :] = gp
    conv = cb_ref[...]
    for j, back in enumerate(taps):
        src = gp if back == 0 else ext_ref[first + rows - back:first + rows - back + n_rows, :]
        conv = conv + cw_ref[j:j + 1, :] * src
    act = (conv * jax.nn.sigmoid(conv)) * up
    acc_ref[rows:rows + n_rows, :] += jnp.dot(act.astype(BF16), wd_ref[...], preferred_element_type=F32)


def _ffn_prompt_body(h_ref, halo_ref, wg_ref, wu_ref, wd_ref, cw_ref, cb_ref, x1_ref, fn_ref,
                     y_ref, tail_ref, acc_ref, ext_ref, *, tiles_per_seq):
    i = pl.program_id(0)
    f = pl.program_id(1)
    tm = h_ref.shape[0]
    _ffn_zero_acc(f, acc_ref)
    halo = jnp.dot(halo_ref[...], wg_ref[...], preferred_element_type=F32)
    ext_ref[0:SUBLANES, :] = jnp.where(i % tiles_per_seq == 0, 0.0, halo)
    taps = tuple(CONV_W - 1 - j for j in range(CONV_W))
    rc = tm // FFN_ROW_CHUNKS
    for c in range(FFN_ROW_CHUNKS):
        _ffn_rows(h_ref[c * rc:(c + 1) * rc, :], SUBLANES, c * rc, rc, taps,
                  wg_ref, wu_ref, wd_ref, cw_ref, cb_ref, ext_ref, acc_ref)
    tail_ref[...] = ext_ref[tm:tm + SUBLANES, :]
    _ffn_finish(f, pl.num_programs(1), x1_ref, fn_ref, y_ref, acc_ref)


def _ffn_prompt(h2, x1, wg, wu, wd, conv_w, conv_b, final_norm, s, tm, tf):
    n, d = x1.shape
    ff = wg.shape[1]
    assert s % tm == 0 and tm % SUBLANES == 0
    hb = tm // SUBLANES
    return pl.pallas_call(
        functools.partial(_ffn_prompt_body, tiles_per_seq=s // tm),
        grid=(n // tm, ff // tf),
        in_specs=[
            pl.BlockSpec((tm, d), lambda i, f: (i, 0)),
            pl.BlockSpec((SUBLANES, d), lambda i, f: (jnp.maximum(i * hb - 1, 0), 0)),
            pl.BlockSpec((d, tf), lambda i, f: (0, f)),
            pl.BlockSpec((d, tf), lambda i, f: (0, f)),
            pl.BlockSpec((tf, d), lambda i, f: (f, 0)),
            pl.BlockSpec((CONV_W, tf), lambda i, f: (0, f)),
            pl.BlockSpec((1, tf), lambda i, f: (0, f)),
            pl.BlockSpec((tm, d), lambda i, f: (i, 0)),
            pl.BlockSpec((1, d), lambda i, f: (0, 0)),
        ],
        out_specs=[pl.BlockSpec((tm, d), lambda i, f: (i, 0)),
                   pl.BlockSpec((SUBLANES, tf), lambda i, f: (i, f))],
        out_shape=[jax.ShapeDtypeStruct((n, d), F32), jax.ShapeDtypeStruct((n // tm * SUBLANES, ff), F32)],
        scratch_shapes=[pltpu.VMEM((tm, d), F32), pltpu.VMEM((tm + SUBLANES, tf), F32)],
        compiler_params=_params("arbitrary", "arbitrary"),
        name="ffn_prompt",
    )(h2, h2, wg, wu, wd, conv_w, conv_b, x1, final_norm)


def _ffn_sample_body(h_ref, st_ref, wg_ref, wu_ref, wd_ref, cw_ref, cb_ref, x1_ref, fn_ref,
                     y_ref, new_st_ref, acc_ref, ext_ref, *, bd):
    f = pl.program_id(0)
    n = h_ref.shape[0]
    keep = (CONV_W - 1) * bd
    _ffn_zero_acc(f, acc_ref)
    ext_ref[0:keep, :] = st_ref[...]
    taps = tuple((CONV_W - 1 - j) * bd for j in range(CONV_W))
    _ffn_rows(h_ref[...], keep, 0, n, taps, wg_ref, wu_ref, wd_ref, cw_ref, cb_ref, ext_ref, acc_ref)
    new_st_ref[...] = ext_ref[n:n + keep, :]
    _ffn_finish(f, pl.num_programs(0), x1_ref, fn_ref, y_ref, acc_ref)


def _ffn_sample(h2, x1, state, wg, wu, wd, conv_w, conv_b, final_norm, bd, tf):
    n, d = x1.shape
    ff = wg.shape[1]
    keep = (CONV_W - 1) * bd
    assert bd % SUBLANES == 0 and n >= keep
    return pl.pallas_call(
        functools.partial(_ffn_sample_body, bd=bd),
        grid=(ff // tf,),
        in_specs=[
            pl.BlockSpec((n, d), lambda f: (0, 0)),
            pl.BlockSpec((keep, tf), lambda f: (0, f)),
            pl.BlockSpec((d, tf), lambda f: (0, f)),
            pl.BlockSpec((d, tf), lambda f: (0, f)),
            pl.BlockSpec((tf, d), lambda f: (f, 0)),
            pl.BlockSpec((CONV_W, tf), lambda f: (0, f)),
            pl.BlockSpec((1, tf), lambda f: (0, f)),
            pl.BlockSpec((n, d), lambda f: (0, 0)),
            pl.BlockSpec((1, d), lambda f: (0, 0)),
        ],
        out_specs=[pl.BlockSpec((n, d), lambda f: (0, 0)), pl.BlockSpec((keep, tf), lambda f: (0, f))],
        out_shape=[jax.ShapeDtypeStruct((n, d), F32), jax.ShapeDtypeStruct((keep, ff), F32)],
        scratch_shapes=[pltpu.VMEM((n, d), F32), pltpu.VMEM((keep + n, tf), F32)],
        compiler_params=_params("arbitrary"),
        name="ffn_sample",
    )(h2, state, wg, wu, wd, conv_w, conv_b, x1, final_norm)


def _largest_divisor(n, candidates):
    for c in candidates:
        if n % c == 0:
            return c
    raise ValueError(f"no tile for {n} among {candidates}")


def kernel(x_prompt, x_sample, cache_dsa_k, cache_dsa_v, cache_idx_k, cache_fox_k, cache_fox_v, cache_fox_logf, state_ffn_conv, page_table, attn_norm, w_in, b_forget, w_branch_a, w_branch_b, w_out, ffn_norm, w_gate, w_up, w_down, conv_w, conv_b, final_norm):
    b, s, d = x_prompt.shape
    bd, t_new, _ = x_sample.shape
    depth = attn_norm.shape[0]
    assert depth == 1 and t_new * GROUP == SUBLANES and s % Q_BLOCK == 0
    n_pages = page_table.shape[1]
    past = n_pages * PAGE
    n_pool = cache_dsa_k.shape[1]
    ff = w_gate.shape[2]
    lay = _Layout(d)

    w_perm = _prep_w_in_t(jnp.swapaxes(w_in[0], 0, 1), lay)
    bf_row = jnp.zeros((1, LANES), F32).at[0, LOGF_LANE:LOGF_LANE + N_HEADS].set(b_forget[0])
    wa, wb, wo = (w[0].astype(BF16) for w in (w_branch_a, w_branch_b, w_out))
    wg, wu, wd = (w[0].astype(BF16) for w in (w_gate, w_up, w_down))
    g_attn, g_ffn, g_fin = attn_norm[0][None, :], ffn_norm[0][None, :], final_norm[None, :]
    cw, cb = conv_w[0], conv_b[0][None, :]

    tm_p = _largest_divisor(s, (1024, 512, 256, 128))
    tab_p = _rope_tables(jnp.arange(s, dtype=I32))
    tab_s = jnp.tile(_rope_tables(past + jnp.arange(t_new, dtype=I32)), (bd, 1))
    xp2 = x_prompt.reshape(b * s, d)
    xs2 = x_sample.reshape(bd * t_new, d)
    proj_p, small_p, *kv_p = _norm_proj(xp2, g_attn, w_perm, tab_p, bf_row, lay, tm_p)
    proj_s, small_s, *kv_s = _norm_proj(xs2, g_attn, w_perm, tab_s, bf_row, lay, bd * t_new)

    oa_p = _dsa_prompt(proj_p, small_p, kv_p[0], kv_p[1], b, s, lay)
    ob_p = _fox_prompt(proj_p, small_p, kv_p[2], kv_p[3], b, s, lay)

    def cols(name, width):
        o = getattr(lay, name)
        return proj_s[:, o:o + width].astype(F32).reshape(bd, t_new, width)

    def heads_major(x):
        x = x.reshape(bd, t_new, N_KV, GROUP, HEAD_DIM).transpose(0, 2, 3, 1, 4)
        return x.reshape(bd, Q_ROWS, HEAD_DIM)

    def new_kv(x):
        x = x.reshape(bd, t_new, N_KV, HEAD_DIM).transpose(0, 2, 1, 3)
        return jnp.pad(x, ((0, 0), (0, 0), (0, LANES - t_new), (0, 0)))

    def heads_back(o):
        o = o.reshape(bd, N_KV, GROUP, t_new, HEAD_DIM).transpose(0, 3, 1, 2, 4)
        return o.reshape(bd * t_new, N_HEADS * HEAD_DIM)

    ka_s, va_s, kb_s, vb_s = (x.reshape(bd, t_new, N_KV * HEAD_DIM) for x in kv_s)
    ik_s = small_s[:, lay.small_ik:lay.small_ik + IDX_DIM].reshape(bd, t_new, IDX_DIM)
    iwf_s = small_s[:, lay.small_iwf:lay.small_iwf + LANES].reshape(bd, t_new, LANES)
    logf_s = iwf_s[..., LOGF_LANE:LOGF_LANE + N_HEADS]
    iq_s = cols("iq", 1024).reshape(bd, t_new, IDX_HEADS, IDX_DIM).transpose(0, 2, 1, 3)
    iq2 = jnp.broadcast_to(iq_s[:, :, None], (bd, IDX_HEADS, GROUP, t_new, IDX_DIM)).reshape(bd, IDX_HEADS * SUBLANES, IDX_DIM)
    iw_s = iwf_s[..., IW_LANE:IW_LANE + IDX_HEADS].transpose(0, 2, 1)
    iw2 = jnp.broadcast_to(iw_s[:, :, None], (bd, IDX_HEADS, GROUP, t_new)).reshape(bd, IDX_HEADS * SUBLANES, 1)
    ikt_new = jnp.pad(ik_s.transpose(0, 2, 1), ((0, 0), (0, 0), (0, LANES - t_new)))
    lft_new = jnp.pad(logf_s.transpose(0, 2, 1), ((0, 0), (0, 0), (0, LANES - t_new)))

    ik_cache_t = jnp.swapaxes(cache_idx_k[0], 1, 2)
    lf_cache_t = jnp.swapaxes(cache_fox_logf[0], 1, 2)
    kv_rows = lambda c: c.reshape(n_pool, PAGE * N_KV, HEAD_DIM)

    scores = _dsa_sample_keys(page_table, iq2, iw2, ikt_new, ik_cache_t, t_new)
    scores, thr = _dsa_sample_select(scores, min(TOPK_MAX, (past + t_new) // 4), _largest_divisor(bd, (8, 4, 2, 1)))
    oa_s = heads_back(_dsa_sample_attend(page_table, heads_major(cols("qa", 1024)), scores, thr,
                                         new_kv(ka_s), new_kv(va_s), kv_rows(cache_dsa_k), kv_rows(cache_dsa_v)))
    ob_s = heads_back(_fox_sample(page_table, heads_major(cols("qb", 1024)), lft_new, new_kv(kb_s), new_kv(vb_s),
                                  lf_cache_t, kv_rows(cache_fox_k), kv_rows(cache_fox_v), t_new))

    tm_m = _largest_divisor(s, (256, 128))
    x1_p, h2_p = _merge(xp2, oa_p, ob_p, proj_p, wa, wb, wo, g_ffn, lay, tm_m)
    x1_s, h2_s = _merge(xs2, oa_s, ob_s, proj_s, wa, wb, wo, g_ffn, lay, bd * t_new)

    tf = _largest_divisor(ff, (512, 256, 128))
    tm_f = _largest_divisor(s, (512, 256, 128))
    y_p, tails = _ffn_prompt(h2_p, x1_p, wg, wu, wd, cw, cb, g_fin, s, tm_f, tf)
    conv_p = tails.reshape(b, s // tm_f, SUBLANES, ff)[:, -1, SUBLANES - (CONV_W - 1):, :]

    t_major = lambda x: x.reshape(bd, t_new, -1).transpose(1, 0, 2).reshape(t_new * bd, -1)
    state_t = state_ffn_conv[0].transpose(1, 0, 2).reshape((CONV_W - 1) * bd, ff)
    y_s_t, st_t = _ffn_sample(t_major(h2_s), t_major(x1_s), state_t, wg, wu, wd, cw, cb, g_fin, bd, tf)
    y_s = y_s_t.reshape(t_new, bd, d).transpose(1, 0, 2)
    conv_s = st_t.reshape(CONV_W - 1, bd, ff).transpose(1, 0, 2)

    p_kv = [x.reshape(1, b, s, N_KV, HEAD_DIM) for x in kv_p]
    logf_at = lay.small_iwf + LOGF_LANE
    p_out = (p_kv[0], p_kv[1], small_p[:, lay.small_ik:lay.small_ik + IDX_DIM].reshape(1, b, s, IDX_DIM), p_kv[2], p_kv[3],
             small_p[:, logf_at:logf_at + N_HEADS].reshape(1, b, s, N_HEADS),
             conv_p[None])
    s_out = (ka_s.reshape(1, bd, t_new, N_KV, HEAD_DIM), va_s.reshape(1, bd, t_new, N_KV, HEAD_DIM), ik_s[None],
             kb_s.reshape(1, bd, t_new, N_KV, HEAD_DIM), vb_s.reshape(1, bd, t_new, N_KV, HEAD_DIM), logf_s[None],
             conv_s[None])
    return (y_p.reshape(b, s, d), y_s) + p_out + s_out
```

```python
import functools

import numpy as np
import jax
import jax.numpy as jnp
from jax import lax
from jax.experimental import pallas as pl
from jax.experimental.pallas import tpu as pltpu

F32 = jnp.float32
BF16 = jnp.bfloat16
I32 = jnp.int32

HEAD_DIM = 128
N_HEADS = 8
N_KV = 4
GROUP = N_HEADS // N_KV
IDX_HEADS = 16
IDX_DIM = 64
TOPK_MAX = 256
ROPE_THETA = 500000.0
ROT_DIM = HEAD_DIM // 4
IDX_ROT_DIM = IDX_DIM // 4
PAGE = 128
Q_BLOCK = 128
CONV_W = 3
RMS_EPS = 1e-6
ATT_SCALE = HEAD_DIM ** -0.5
IDX_SCALE = (IDX_HEADS * IDX_DIM) ** -0.5

LANES = 128
SUBLANES = 8
NEG = -1e30
NEG_INF = float("-inf")
INT_MIN = -2 ** 31
VMEM_LIMIT = 56 * 1024 * 1024

IW_LANE = 0
LOGF_LANE = IDX_HEADS

KV_NAMES = ("ka", "va", "kb", "vb")

NT_DIMS = (((1,), (1,)), ((), ()))


def _params(*sem):
    return pltpu.CompilerParams(dimension_semantics=sem, vmem_limit_bytes=VMEM_LIMIT)


def _nt(a, b):
    return lax.dot_general(a, b, NT_DIMS, preferred_element_type=F32)


def _rms(x, g):
    ms = jnp.mean(x * x, axis=-1, keepdims=True)
    return (x * lax.rsqrt(ms + RMS_EPS)) * g


class _Layout:
    def __init__(self, d_model):
        self.d = d_model
        self.tn = 512
        off = 0
        for name, size in (("ga", d_model), ("gb", d_model), ("qa", 1024), ("iq", 1024), ("qb", 1024)):
            assert off % size == 0, (name, off, size)
            setattr(self, name, off)
            off += size
        assert off % self.tn == 0
        self.proj_cols = off
        self.ik, self.iwf = off, off + LANES
        self.small_ik, self.small_iwf = 0, LANES
        off += self.tn
        for name in KV_NAMES:
            setattr(self, name, off)
            off += N_KV * HEAD_DIM
        assert N_KV * HEAD_DIM == self.tn
        self.nc = off

    def chunk_kinds(self):
        kinds = ["plain"] * (self.nc // LANES)
        for name, size, kind in (("qa", 1024, "qrope"), ("qb", 1024, "q"), ("iq", 1024, "rope64"),
                                 ("ik", self.tn, "s:plain"), ("ik", LANES, "s:rope64"), ("iwf", LANES, "s:iwf"),
                                 ("ka", 512, "kv0rope"), ("va", 512, "kv1"), ("kb", 512, "kv2"), ("vb", 512, "kv3")):
            start = getattr(self, name) // LANES
            for c in range(size // LANES):
                kinds[start + c] = kind
        return kinds


def _w_in_plan(lay):
    d = lay.d
    sizes = (1024, 512, 512, 1024, IDX_DIM, IDX_HEADS, 1024, 512, 512, N_HEADS, d, d)
    names = ("qa", "ka", "va", "iq", "ik", "iw", "qb", "kb", "vb", "fl", "ga", "gb")
    offs = np.concatenate([[0], np.cumsum(sizes)])
    src = {n: int(offs[k]) for k, n in enumerate(names)}
    plan = [()] * (lay.nc // LANES)
    for name, size in (("ga", d), ("gb", d), ("qa", 1024), ("iq", 1024), ("qb", 1024),
                       ("ka", 512), ("va", 512), ("kb", 512), ("vb", 512)):
        for c in range(size // LANES):
            plan[getattr(lay, name) // LANES + c] = ((src[name] + c * LANES, 0, LANES),)
    plan[lay.ik // LANES] = ((src["ik"], 0, IDX_DIM),)
    plan[lay.iwf // LANES] = ((src["iw"], IW_LANE, IDX_HEADS), (src["fl"], LOGF_LANE, N_HEADS))
    return tuple(plan), int(offs[-1])


def _prep_w_body(w_ref, o_ref, *, plan):
    cols = w_ref.shape[1]
    for c, pieces in enumerate(plan):
        parts, pos = [], 0
        for first, at, height in pieces:
            if at > pos:
                parts.append(jnp.zeros((at - pos, cols), F32))
            parts.append(w_ref[first:first + height, :])
            pos = at + height
        if pos < LANES:
            parts.append(jnp.zeros((LANES - pos, cols), F32))
        chunk = parts[0] if len(parts) == 1 else jnp.concatenate(parts, axis=0)
        o_ref[c * LANES:(c + 1) * LANES, :] = chunk.astype(BF16)


def _prep_w_in_t(w_in_t, lay):
    n_src, d = w_in_t.shape
    plan, n_cols = _w_in_plan(lay)
    assert n_cols == n_src and all(f % SUBLANES == 0 and a % SUBLANES == 0 for p in plan for f, a, _ in p)
    tc = _largest_divisor(d, (256, 128))
    return pl.pallas_call(
        functools.partial(_prep_w_body, plan=plan),
        grid=(d // tc,),
        in_specs=[pl.BlockSpec((n_src, tc), lambda i: (0, i))],
        out_specs=pl.BlockSpec((lay.nc, tc), lambda i: (0, i)),
        out_shape=jax.ShapeDtypeStruct((lay.nc, d), BF16),
        compiler_params=_params("arbitrary"),
        name="prep_w_in",
    )(w_in_t)


def _rope_tables(pos):
    def one(rot_dim, period):
        half = rot_dim // 2
        inv_freq = jnp.power(ROPE_THETA, -jnp.arange(half, dtype=F32) * (2.0 / rot_dim))
        ang = pos.astype(F32)[:, None] * inv_freq[None, :]
        cos, sin = jnp.cos(ang), jnp.sin(ang)
        n = pos.shape[0]
        c = jnp.concatenate([cos, cos, jnp.ones((n, period - rot_dim), F32)], axis=1)
        sa = jnp.concatenate([-sin, jnp.zeros((n, period - half), F32)], axis=1)
        sb = jnp.concatenate([jnp.zeros((n, half), F32), sin, jnp.zeros((n, period - rot_dim), F32)], axis=1)
        rep = LANES // period
        return [jnp.tile(t, (1, rep)) for t in (c, sa, sb)]
    return jnp.concatenate(one(ROT_DIM, HEAD_DIM) + one(IDX_ROT_DIM, IDX_DIM), axis=1)


def _proj_body(x_ref, g_ref, w_ref, tab_ref, bf_ref, o_ref, small_ref, *rest, tile_kinds):
    kv_refs, h_ref = rest[:len(KV_NAMES)], rest[len(KV_NAMES)]
    j = pl.program_id(1)
    tm = x_ref.shape[0]

    @pl.when(j == 0)
    def _():
        h_ref[...] = _rms(x_ref[...], g_ref[...]).astype(BF16)

    acc = _nt(h_ref[...], w_ref[...])

    def rope(a, base, half):
        c = tab_ref[:, base:base + LANES]
        sa = tab_ref[:, base + LANES:base + 2 * LANES]
        sb = tab_ref[:, base + 2 * LANES:base + 3 * LANES]
        return a * c + pltpu.roll(a, LANES - half, 1) * sa + pltpu.roll(a, half, 1) * sb

    def indexer_weight_and_log_forget(a):
        z = a + bf_ref[...]
        ls = jnp.minimum(z, 0.0) - jnp.log1p(jnp.exp(-jnp.abs(z)))
        lane = lax.broadcasted_iota(I32, a.shape, 1)
        return jnp.where(lane < LOGF_LANE, a * IDX_SCALE, jnp.where(lane < LOGF_LANE + N_HEADS, ls, a))

    def emit(kinds):
        for c, kind in enumerate(kinds):
            a = acc[:, c * LANES:(c + 1) * LANES]
            to_small = kind.startswith("s:")
            kind = kind[2:] if to_small else kind
            if kind == "qrope":
                a = rope(a, 0, ROT_DIM // 2) * ATT_SCALE
            elif kind == "q":
                a = a * ATT_SCALE
            elif kind == "rope64":
                a = rope(a, 3 * LANES, IDX_ROT_DIM // 2)
            elif kind == "iwf":
                a = indexer_weight_and_log_forget(a)
            if kind.startswith("kv"):
                if kind.endswith("rope"):
                    a = rope(a, 0, ROT_DIM // 2)
                kv_refs[int(kind[2])][pl.ds(c, tm, stride=N_KV), :] = a
            elif to_small:
                small_ref[:, c * LANES:(c + 1) * LANES] = a
            else:
                o_ref[:, c * LANES:(c + 1) * LANES] = a.astype(BF16)

    groups = {}
    for t, kinds in enumerate(tile_kinds):
        groups.setdefault(kinds, []).append(t)
    for kinds, tiles in groups.items():
        cond = functools.reduce(jnp.logical_or, [j == t for t in tiles])
        if all(k == "plain" for k in kinds):
            @pl.when(cond)
            def _():
                o_ref[...] = acc.astype(BF16)
        else:
            pl.when(cond)(functools.partial(emit, kinds))


def _norm_proj(x2d, gamma, w_perm, tab, bf_row, lay, tm):
    n, d = x2d.shape
    tn = lay.tn
    kinds = lay.chunk_kinds()
    per = tn // LANES
    tile_kinds = tuple(tuple(kinds[t * per:(t + 1) * per]) for t in range(lay.nc // tn))
    tab_blocks = tab.shape[0] // tm
    last_proj_tile = lay.proj_cols // tn - 1
    kv_spec = pl.BlockSpec((tm * N_KV, HEAD_DIM), lambda i, j: (i, 0))
    return pl.pallas_call(
        functools.partial(_proj_body, tile_kinds=tile_kinds),
        grid=(n // tm, lay.nc // tn),
        in_specs=[
            pl.BlockSpec((tm, d), lambda i, j: (i, 0), pipeline_mode=pl.Buffered(1)),
            pl.BlockSpec((1, d), lambda i, j: (0, 0)),
            pl.BlockSpec((tn, d), lambda i, j: (j, 0)),
            pl.BlockSpec((tm, 6 * LANES), lambda i, j: (i % tab_blocks, 0)),
            pl.BlockSpec((1, LANES), lambda i, j: (0, 0)),
        ],
        out_specs=[pl.BlockSpec((tm, tn), lambda i, j: (i, jnp.minimum(j, last_proj_tile))),
                   pl.BlockSpec((tm, tn), lambda i, j: (i, 0))] + [kv_spec] * len(KV_NAMES),
        out_shape=[jax.ShapeDtypeStruct((n, lay.proj_cols), BF16), jax.ShapeDtypeStruct((n, tn), F32)]
        + [jax.ShapeDtypeStruct((n * N_KV, HEAD_DIM), F32)] * len(KV_NAMES),
        scratch_shapes=[pltpu.VMEM((tm, d), BF16)],
        compiler_params=_params("arbitrary", "arbitrary"),
        name="norm_proj",
    )(x2d, gamma, w_perm, tab, bf_row)


def _code_to_float(code):
    bits = jnp.where(code < 0, code ^ jnp.int32(0x7FFFFFFF), code)
    return pltpu.bitcast(bits, F32)


def _kth_largest(read_scores, rows, k):
    def body(it, code):
        cand = code + jnp.left_shift(jnp.int32(1), 31 - it)
        cnt = jnp.sum(jnp.where(read_scores() >= _code_to_float(cand), 1.0, 0.0), axis=1, keepdims=True)
        return jnp.where(cnt >= k, cand, code)

    code = lax.fori_loop(0, 32, body, jnp.full((rows, 1), INT_MIN, I32))
    return _code_to_float(code), code == INT_MIN


def _count(mask):
    return jnp.sum(jnp.where(mask, 1.0, 0.0), axis=1, keepdims=True)


def _total(mask):
    ones = jnp.where(mask, 1.0, 0.0)
    return jnp.sum(jnp.sum(ones, axis=0, keepdims=True), axis=1, keepdims=True)[0, 0]


def _earlier_in_chunk():
    r = lax.broadcasted_iota(I32, (LANES, LANES), 0)
    c = lax.broadcasted_iota(I32, (LANES, LANES), 1)
    return jnp.where(r < c, 1.0, 0.0).astype(BF16)


def _tied_keys_to_keep(eq, seen, need, earlier):
    ones = jnp.where(eq, 1.0, 0.0)
    rank = seen + jnp.dot(ones.astype(BF16), earlier, preferred_element_type=F32)
    return eq & (rank < need), seen + jnp.sum(ones, axis=1, keepdims=True)


KEY_BUCKET = 512


def _key_limits(s):
    step = min(KEY_BUCKET, s)
    assert s % step == 0
    return tuple(range(step, s + 1, step))


def _for_causal_limit(i, limits, block_fn):
    q_end = (i + 1) * Q_BLOCK
    prev = 0
    for lim in limits:
        pl.when((q_end > prev) & (q_end <= lim))(functools.partial(block_fn, lim))
        prev = lim


def _load_kv_heads(src_ref, dst_ref):
    tokens = dst_ref.shape[0]
    for kh in range(N_KV):
        dst_ref[:, kh * HEAD_DIM:(kh + 1) * HEAD_DIM] = src_ref[pl.ds(kh, tokens, stride=N_KV), :].astype(BF16)


def _attend_heads(q_ref, kb_ref, vb_ref, o_ref, n, logit_bias):
    for kh in range(N_KV):
        kk = kb_ref[0:n, kh * HEAD_DIM:(kh + 1) * HEAD_DIM]
        vv = vb_ref[0:n, kh * HEAD_DIM:(kh + 1) * HEAD_DIM]
        for g in range(GROUP):
            h = kh * GROUP + g
            q = q_ref[:, h * HEAD_DIM:(h + 1) * HEAD_DIM]
            lg = _nt(q, kk) + logit_bias(h)
            m = jnp.max(lg, axis=1, keepdims=True)
            e = jnp.exp(lg - m)
            l = jnp.sum(e, axis=1, keepdims=True)
            o = jnp.dot(e.astype(BF16), vv, preferred_element_type=F32)
            o_ref[:, h * HEAD_DIM:(h + 1) * HEAD_DIM] = o / l


def _dsa_prompt_body(iq_ref, iwf_ref, ik_ref, q_ref, k_ref, v_ref, o_ref,
                     ikb_ref, kb_ref, vb_ref, sc_ref, bias_ref, *, topk, limits):
    i = pl.program_id(1)

    @pl.when(i == 0)
    def _():
        ikb_ref[...] = ik_ref[:, :IDX_DIM].astype(BF16)
        _load_kv_heads(k_ref, kb_ref)
        _load_kv_heads(v_ref, vb_ref)

    def block(n):
        ikb = ikb_ref[0:n, :]
        for h in range(IDX_HEADS):
            qh = iq_ref[:, h * IDX_DIM:(h + 1) * IDX_DIM]
            s = jnp.maximum(_nt(qh, ikb), 0.0) * iwf_ref[:, IW_LANE + h:IW_LANE + h + 1]
            if h == 0:
                sc_ref[:, 0:n] = s
            else:
                sc_ref[:, 0:n] += s
        row = i * Q_BLOCK + lax.broadcasted_iota(I32, (Q_BLOCK, n), 0)
        col = lax.broadcasted_iota(I32, (Q_BLOCK, n), 1)
        causal = col <= row
        sc_ref[:, 0:n] = jnp.where(causal, sc_ref[:, 0:n], NEG_INF)
        thr, none = _kth_largest(lambda: sc_ref[:, 0:n], Q_BLOCK, topk)
        chosen = ((sc_ref[:, 0:n] >= thr) | none) & causal
        bias_ref[:, 0:n] = jnp.where(chosen, 0.0, NEG)

        first = i * Q_BLOCK
        short = jnp.clip(topk - first, 0, Q_BLOCK)
        expected = short * first + (short * (short + 1)) // 2 + (Q_BLOCK - short) * topk

        @pl.when(_total(chosen) > expected.astype(F32))
        def _():
            need = topk - _count((sc_ref[:, 0:n] > thr) & causal)
            earlier = _earlier_in_chunk()
            seen = jnp.zeros((Q_BLOCK, 1), F32)
            for c in range(n // LANES):
                lanes = slice(c * LANES, (c + 1) * LANES)
                sc = sc_ref[:, lanes]
                ok = (c * LANES + lax.broadcasted_iota(I32, (Q_BLOCK, LANES), 1)
                      <= first + lax.broadcasted_iota(I32, (Q_BLOCK, LANES), 0))
                keep, seen = _tied_keys_to_keep((sc == thr) & ok, seen, need, earlier)
                bias_ref[:, lanes] = jnp.where((((sc > thr) | none) & ok) | keep, 0.0, NEG)
        _attend_heads(q_ref, kb_ref, vb_ref, o_ref, n, lambda h: bias_ref[:, 0:n])

    _for_causal_limit(i, limits, block)


def _dsa_prompt(proj, small, k4, v4, b, s, lay):
    nb = s // Q_BLOCK
    topk = min(TOPK_MAX, s // 4)
    row = lambda bb, i: bb * nb + i
    return pl.pallas_call(
        functools.partial(_dsa_prompt_body, topk=topk, limits=_key_limits(s)),
        grid=(b, nb),
        in_specs=[
            pl.BlockSpec((Q_BLOCK, 1024), lambda bb, i: (row(bb, i), lay.iq // 1024)),
            pl.BlockSpec((Q_BLOCK, LANES), lambda bb, i: (row(bb, i), lay.small_iwf // LANES)),
            pl.BlockSpec((s, LANES), lambda bb, i: (bb, lay.small_ik // LANES)),
            pl.BlockSpec((Q_BLOCK, 1024), lambda bb, i: (row(bb, i), lay.qa // 1024)),
            pl.BlockSpec((s * N_KV, HEAD_DIM), lambda bb, i: (bb, 0)),
            pl.BlockSpec((s * N_KV, HEAD_DIM), lambda bb, i: (bb, 0)),
        ],
        out_specs=pl.BlockSpec((Q_BLOCK, 1024), lambda bb, i: (row(bb, i), 0)),
        out_shape=jax.ShapeDtypeStruct((b * s, 1024), F32),
        scratch_shapes=[pltpu.VMEM((s, IDX_DIM), BF16), pltpu.VMEM((s, 512), BF16), pltpu.VMEM((s, 512), BF16),
                        pltpu.VMEM((Q_BLOCK, s), F32), pltpu.VMEM((Q_BLOCK, s), F32)],
        compiler_params=_params("arbitrary", "arbitrary"),
        name="dsa_prompt",
    )(proj, small, small, proj, k4, v4)


CUM_BLOCK = 256


def _fox_prompt_body(q_ref, k_ref, v_ref, lf_ref, o_ref, kb_ref, vb_ref, c_ref, ct_ref, bias_ref, *, limits):
    i = pl.program_id(1)
    s_len = kb_ref.shape[0]

    @pl.when(i == 0)
    def _():
        _load_kv_heads(k_ref, kb_ref)
        _load_kv_heads(v_ref, vb_ref)
        r = lax.broadcasted_iota(I32, (CUM_BLOCK, CUM_BLOCK), 0)
        c = lax.broadcasted_iota(I32, (CUM_BLOCK, CUM_BLOCK), 1)
        tri = jnp.where(c <= r, 1.0, 0.0).astype(F32)
        carry = jnp.zeros((1, LANES), F32)
        for blk in range(s_len // CUM_BLOCK):
            xb = lf_ref[blk * CUM_BLOCK:(blk + 1) * CUM_BLOCK, :]
            cb = jnp.dot(tri, xb, precision=lax.Precision.HIGHEST, preferred_element_type=F32) + carry
            c_ref[blk * CUM_BLOCK:(blk + 1) * CUM_BLOCK, :] = cb
            carry = cb[CUM_BLOCK - 1:CUM_BLOCK, :]
        ct_ref[...] = c_ref[...].T

    start = pl.multiple_of(i * Q_BLOCK, Q_BLOCK)

    def block(n):
        row = i * Q_BLOCK + lax.broadcasted_iota(I32, (Q_BLOCK, n), 0)
        col = lax.broadcasted_iota(I32, (Q_BLOCK, n), 1)
        bias_ref[:, 0:n] = jnp.where(col <= row, 0.0, NEG)

        def logit_bias(h):
            cq = c_ref[pl.ds(start, Q_BLOCK), LOGF_LANE + h:LOGF_LANE + h + 1]
            ck = ct_ref[LOGF_LANE + h:LOGF_LANE + h + 1, 0:n]
            return (cq - ck) + bias_ref[:, 0:n]

        _attend_heads(q_ref, kb_ref, vb_ref, o_ref, n, logit_bias)

    _for_causal_limit(i, limits, block)


def _fox_prompt(proj, small, k4, v4, b, s, lay):
    nb = s // Q_BLOCK
    assert s % CUM_BLOCK == 0
    row = lambda bb, i: bb * nb + i
    return pl.pallas_call(
        functools.partial(_fox_prompt_body, limits=_key_limits(s)),
        grid=(b, nb),
        in_specs=[
            pl.BlockSpec((Q_BLOCK, 1024), lambda bb, i: (row(bb, i), lay.qb // 1024)),
            pl.BlockSpec((s * N_KV, HEAD_DIM), lambda bb, i: (bb, 0)),
            pl.BlockSpec((s * N_KV, HEAD_DIM), lambda bb, i: (bb, 0)),
            pl.BlockSpec((s, LANES), lambda bb, i: (bb, lay.small_iwf // LANES)),
        ],
        out_specs=pl.BlockSpec((Q_BLOCK, 1024), lambda bb, i: (row(bb, i), 0)),
        out_shape=jax.ShapeDtypeStruct((b * s, 1024), F32),
        scratch_shapes=[pltpu.VMEM((s, 512), BF16), pltpu.VMEM((s, 512), BF16),
                        pltpu.VMEM((s, LANES), F32), pltpu.VMEM((LANES, s), F32), pltpu.VMEM((Q_BLOCK, s), F32)],
        compiler_params=_params("arbitrary", "arbitrary"),
        name="fox_prompt",
    )(proj, k4, v4, small)


Q_ROWS = N_KV * SUBLANES


def _kv_rows(pages, kh):
    return jnp.concatenate([p[pl.ds(kh, PAGE, stride=N_KV), :] for p in pages], axis=0).astype(BF16)


def _softmax_update(q_ref, keys_of, values_of, bias, sel, m_ref, l_ref, acc_ref):
    lg = jnp.concatenate([_nt(_q_rows(q_ref, kh), keys_of(kh)) for kh in range(N_KV)], axis=0)
    if bias is not None:
        lg = lg + bias
    if sel is not None:
        lg = jnp.where(sel, lg, NEG)
    m_old = m_ref[...]
    m_new = jnp.maximum(m_old, jnp.max(lg, axis=1, keepdims=True))
    corr = jnp.exp(m_old - m_new)
    e = jnp.exp(lg - m_new)
    if sel is not None:
        e = jnp.where(sel, e, 0.0)
    l_ref[...] = l_ref[...] * corr + jnp.sum(e, axis=1, keepdims=True)
    pv = jnp.concatenate([jnp.dot(e[kh * SUBLANES:(kh + 1) * SUBLANES].astype(BF16), values_of(kh),
                                  preferred_element_type=F32) for kh in range(N_KV)], axis=0)
    acc_ref[...] = acc_ref[...] * corr + pv
    m_ref[...] = m_new


def _softmax_init(m_ref, l_ref, acc_ref):
    m_ref[...] = jnp.full(m_ref.shape, NEG, F32)
    l_ref[...] = jnp.zeros(l_ref.shape, F32)
    acc_ref[...] = jnp.zeros(acc_ref.shape, F32)


def _softmax_scratch():
    return [pltpu.VMEM((Q_ROWS, 1), F32), pltpu.VMEM((Q_ROWS, 1), F32), pltpu.VMEM((Q_ROWS, HEAD_DIM), F32)]


def _q_rows(q_ref, kh):
    return q_ref[kh * SUBLANES:(kh + 1) * SUBLANES, :].astype(BF16)


def _dsa_sample_keys_body(pt_ref, iq_ref, iw_ref, ikn_ref, ik_hbm, sc_ref, ikbuf, iksem, *, pp, n_chunks, n_new):
    past = n_chunks * pp * PAGE
    q = iq_ref[...].astype(BF16)
    w = iw_ref[...]

    def scores_of(ikt):
        s = jnp.maximum(jnp.dot(q, ikt.astype(BF16), preferred_element_type=F32), 0.0) * w
        acc = s[0:SUBLANES]
        for h in range(1, IDX_HEADS):
            acc = acc + s[h * SUBLANES:(h + 1) * SUBLANES]
        return acc

    row = lax.broadcasted_iota(I32, (SUBLANES, LANES), 0)
    lane = lax.broadcasted_iota(I32, (SUBLANES, LANES), 1)
    sc_ref[:, past:past + LANES] = jnp.where(lane <= row % n_new, scores_of(ikn_ref[...]), NEG_INF)

    def compute(c, slot):
        ikt = jnp.concatenate([ikbuf[slot, r] for r in range(pp)], axis=1)
        sc_ref[:, c * pp * PAGE:(c + 1) * pp * PAGE] = scores_of(ikt)

    _walk_page_chunks(pt_ref, [(ik_hbm, ikbuf, iksem)], pp, n_chunks, lambda c, r: c * pp + r, compute)


def _dsa_sample_keys(page_table, iq2, iw2, ikt_new, ik_cache_t, n_new):
    bd, n_pages = page_table.shape
    pp = _chunk_pages(n_pages, 16)
    past = n_pages * PAGE
    return pl.pallas_call(
        functools.partial(_dsa_sample_keys_body, pp=pp, n_chunks=n_pages // pp, n_new=n_new),
        grid_spec=pltpu.PrefetchScalarGridSpec(
            num_scalar_prefetch=1,
            grid=(bd,),
            in_specs=[_per_seq1((IDX_HEADS * SUBLANES, IDX_DIM)), _per_seq1((IDX_HEADS * SUBLANES, 1)),
                      _per_seq1((IDX_DIM, LANES)), pl.BlockSpec(memory_space=pl.ANY)],
            out_specs=pl.BlockSpec((SUBLANES, past + LANES), lambda bb, pt: (bb, 0)),
            scratch_shapes=[pltpu.VMEM((PAGE_SLOTS, pp, IDX_DIM, PAGE), F32), pltpu.SemaphoreType.DMA((PAGE_SLOTS,))],
        ),
        out_shape=jax.ShapeDtypeStruct((bd * SUBLANES, past + LANES), F32),
        compiler_params=_params("arbitrary"),
        name="dsa_sample_keys",
    )(page_table, iq2, iw2, ikt_new, ik_cache_t)


def _dsa_sample_select_body(sc_ref, sel_ref, thr_ref, *, topk):
    rows, n = sc_ref.shape
    thr, _ = _kth_largest(lambda: sc_ref[...], rows, topk)
    thr_ref[...] = jnp.broadcast_to(thr, thr_ref.shape)
    sel_ref[...] = sc_ref[...]

    @pl.when(jnp.max(_count(sc_ref[...] >= thr)) > topk)
    def _():
        need = topk - _count(sc_ref[...] > thr)
        earlier = _earlier_in_chunk()

        def chunk(c, seen):
            lanes = pl.ds(pl.multiple_of(c * LANES, LANES), LANES)
            sc = sc_ref[:, lanes]
            eq = sc == thr
            keep, seen = _tied_keys_to_keep(eq, seen, need, earlier)
            sel_ref[:, lanes] = jnp.where(eq & jnp.logical_not(keep), NEG_INF, sc)
            return seen

        lax.fori_loop(0, n // LANES, chunk, jnp.zeros((rows, 1), F32))


def _dsa_sample_select(scores, topk, seqs_per_step):
    rows_total, n = scores.shape
    rows = seqs_per_step * SUBLANES
    return pl.pallas_call(
        functools.partial(_dsa_sample_select_body, topk=topk),
        grid=(rows_total // rows,),
        in_specs=[pl.BlockSpec((rows, n), lambda i: (i, 0))],
        out_specs=[pl.BlockSpec((rows, n), lambda i: (i, 0)), pl.BlockSpec((rows, LANES), lambda i: (i, 0))],
        out_shape=[jax.ShapeDtypeStruct((rows_total, n), F32), jax.ShapeDtypeStruct((rows_total, LANES), F32)],
        compiler_params=_params("arbitrary"),
        name="dsa_sample_select",
    )(scores)


PAGE_SLOTS = 4
PAGES_AHEAD = 3


def _chunk_pages(n_pages, most=8):
    for pp in (16, 8, 4, 2, 1):
        if pp <= most and n_pages % pp == 0 and (n_pages // pp) % PAGE_SLOTS == 0:
            return pp
    raise ValueError(f"need a multiple of {PAGE_SLOTS} page chunks, got {n_pages} pages")


def _walk_page_chunks(pt_ref, streams, pp, n_chunks, page_of, compute):
    b = pl.program_id(0)
    assert n_chunks % PAGE_SLOTS == 0 and PAGES_AHEAD < PAGE_SLOTS and PAGES_AHEAD <= n_chunks

    def copies(seq, c, slot, to_wait=False):
        out = []
        for r in range(pp):
            page = 0 if to_wait else pt_ref[seq, page_of(c, r)]
            for hbm, buf, sem in streams:
                out.append(pltpu.make_async_copy(hbm.at[page], buf.at[slot, r], sem.at[slot]))
        return out

    @pl.when(b == 0)
    def _():
        for c in range(PAGES_AHEAD):
            for d in copies(b, c, c % PAGE_SLOTS):
                d.start()

    for c in range(n_chunks):
        for d in copies(b, c, c % PAGE_SLOTS, to_wait=True):
            d.wait()
        ahead = c + PAGES_AHEAD
        if ahead < n_chunks:
            for d in copies(b, ahead, ahead % PAGE_SLOTS):
                d.start()
        else:
            @pl.when(b + 1 < pl.num_programs(0))
            def _(ahead=ahead):
                for d in copies(b + 1, ahead - n_chunks, ahead % PAGE_SLOTS):
                    d.start()
        compute(c, c % PAGE_SLOTS)


def _dsa_sample_attend_body(pt_ref, q_ref, sc_ref, thr_ref, knew_ref, vnew_ref, k_hbm, v_hbm, o_ref,
                            kbuf, vbuf, ksem, vsem, m_ref, l_ref, acc_ref, *, pp, n_chunks):
    past = n_chunks * pp * PAGE
    thr = jnp.concatenate([thr_ref[:, 0:1]] * N_KV, axis=0)

    def selected(scores):
        scores = jnp.concatenate([scores] * N_KV, axis=0)
        return (scores >= thr) & (scores > NEG_INF)

    _softmax_init(m_ref, l_ref, acc_ref)
    _softmax_update(q_ref, lambda kh: knew_ref[kh].astype(BF16), lambda kh: vnew_ref[kh].astype(BF16),
                    None, selected(sc_ref[:, past:past + LANES]), m_ref, l_ref, acc_ref)

    def compute(c, slot):
        k_pages = [kbuf.at[slot, r] for r in range(pp)]
        v_pages = [vbuf.at[slot, r] for r in range(pp)]
        sel = selected(sc_ref[:, c * pp * PAGE:(c + 1) * pp * PAGE])
        _softmax_update(q_ref, functools.partial(_kv_rows, k_pages), functools.partial(_kv_rows, v_pages),
                        None, sel, m_ref, l_ref, acc_ref)

    _walk_page_chunks(pt_ref, [(k_hbm, kbuf, ksem), (v_hbm, vbuf, vsem)], pp, n_chunks,
                      lambda c, r: c * pp + r, compute)
    o_ref[...] = acc_ref[...] / l_ref[...]


def _per_seq1(shape):
    return pl.BlockSpec((None,) + shape, lambda bb, pt: (bb,) + (0,) * len(shape))


def _kv_page_scratch(pp):
    page = (PAGE * N_KV, HEAD_DIM)
    return [pltpu.VMEM((PAGE_SLOTS, pp) + page, F32), pltpu.VMEM((PAGE_SLOTS, pp) + page, F32),
            pltpu.SemaphoreType.DMA((PAGE_SLOTS,)), pltpu.SemaphoreType.DMA((PAGE_SLOTS,))]


def _dsa_sample_attend(page_table, q_s, scores, thr, k_new, v_new, k_cache, v_cache):
    bd, n_pages = page_table.shape
    pp = _chunk_pages(n_pages)
    n_cols = scores.shape[1]
    hbm = pl.BlockSpec(memory_space=pl.ANY)
    return pl.pallas_call(
        functools.partial(_dsa_sample_attend_body, pp=pp, n_chunks=n_pages // pp),
        grid_spec=pltpu.PrefetchScalarGridSpec(
            num_scalar_prefetch=1,
            grid=(bd,),
            in_specs=[_per_seq1((Q_ROWS, HEAD_DIM)),
                      pl.BlockSpec((SUBLANES, n_cols), lambda bb, pt: (bb, 0)),
                      pl.BlockSpec((SUBLANES, LANES), lambda bb, pt: (bb, 0)),
                      _per_seq1((N_KV, LANES, HEAD_DIM)), _per_seq1((N_KV, LANES, HEAD_DIM)), hbm, hbm],
            out_specs=_per_seq1((Q_ROWS, HEAD_DIM)),
            scratch_shapes=_kv_page_scratch(pp) + _softmax_scratch(),
        ),
        out_shape=jax.ShapeDtypeStruct((bd, Q_ROWS, HEAD_DIM), F32),
        compiler_params=_params("arbitrary"),
        name="dsa_sample_attend",
    )(page_table, q_s, scores, thr, k_new, v_new, k_cache, v_cache)


def _fox_sample_body(pt_ref, q_ref, lfn_ref, knew_ref, vnew_ref, rep_ref, later_ref, lf_hbm, k_hbm, v_hbm, o_ref,
                     lfbuf, kbuf, vbuf, lfsem, ksem, vsem, m_ref, l_ref, acc_ref, cq_ref, carry_ref,
                     *, pp, n_chunks, n_new):
    hi = lax.Precision.HIGHEST
    n_pages = pp * n_chunks

    _softmax_init(m_ref, l_ref, acc_ref)
    carry_ref[...] = jnp.zeros(carry_ref.shape, F32)
    r_io = lax.broadcasted_iota(I32, (LANES, LANES), 0)
    c_io = lax.broadcasted_iota(I32, (LANES, LANES), 1)
    incl = jnp.where(r_io <= c_io, 1.0, 0.0).astype(F32)
    cum = jnp.dot(lfn_ref[...], incl, precision=hi, preferred_element_type=F32)
    cg = jnp.dot(rep_ref[0:Q_ROWS, 0:N_HEADS], cum, precision=hi, preferred_element_type=F32)
    row = lax.broadcasted_iota(I32, (Q_ROWS, LANES), 0)
    lane = lax.broadcasted_iota(I32, (Q_ROWS, LANES), 1)
    own = lane == row % n_new
    cq_ref[...] = jnp.sum(jnp.where(own, cg, 0.0), axis=1, keepdims=True)
    _softmax_update(q_ref, lambda kh: knew_ref[kh].astype(BF16), lambda kh: vnew_ref[kh].astype(BF16),
                    cq_ref[...] - cg, lane <= row % n_new, m_ref, l_ref, acc_ref)

    def compute(c, slot):
        k_pages = [kbuf.at[slot, r] for r in range(pp)]
        v_pages = [vbuf.at[slot, r] for r in range(pp)]
        lf_all = jnp.concatenate([lfbuf[slot, r] for r in range(pp)], axis=0)
        lf_rows = jnp.dot(rep_ref[...], lf_all, precision=hi, preferred_element_type=F32)
        within = jnp.dot(lf_rows, later_ref[...], precision=hi, preferred_element_type=F32)
        total = within[:, 0:1] + lf_rows[:, 0:1]
        run = carry_ref[...]
        biases = []
        for r in range(pp):
            biases.append(within[r * Q_ROWS:(r + 1) * Q_ROWS] + (run + cq_ref[...]))
            run = run + total[r * Q_ROWS:(r + 1) * Q_ROWS]
        carry_ref[...] = run
        _softmax_update(q_ref, functools.partial(_kv_rows, k_pages), functools.partial(_kv_rows, v_pages),
                        jnp.concatenate(biases, axis=1), None, m_ref, l_ref, acc_ref)

    _walk_page_chunks(pt_ref, [(lf_hbm, lfbuf, lfsem), (k_hbm, kbuf, ksem), (v_hbm, vbuf, vsem)], pp, n_chunks,
                      lambda c, r: n_pages - 1 - (c * pp + r), compute)
    o_ref[...] = acc_ref[...] / l_ref[...]


def _fox_sample(page_table, q_s, lft_new, k_new, v_new, lf_cache_t, k_cache, v_cache, n_new):
    bd, n_pages = page_table.shape
    pp = _chunk_pages(n_pages)
    row_head = np.arange(Q_ROWS) // n_new
    rep_one = (row_head[:, None] == np.arange(N_HEADS)[None, :]).astype(np.float32)
    rep = jnp.asarray(np.kron(np.eye(pp, dtype=np.float32), rep_one))
    later = jnp.asarray((np.arange(PAGE)[:, None] > np.arange(PAGE)[None, :]).astype(np.float32))
    hbm = pl.BlockSpec(memory_space=pl.ANY)
    shared = lambda shape: pl.BlockSpec(shape, lambda bb, pt: (0,) * len(shape))
    return pl.pallas_call(
        functools.partial(_fox_sample_body, pp=pp, n_chunks=n_pages // pp, n_new=n_new),
        grid_spec=pltpu.PrefetchScalarGridSpec(
            num_scalar_prefetch=1,
            grid=(bd,),
            in_specs=[_per_seq1((Q_ROWS, HEAD_DIM)), _per_seq1((N_HEADS, LANES)),
                      _per_seq1((N_KV, LANES, HEAD_DIM)), _per_seq1((N_KV, LANES, HEAD_DIM)),
                      shared((pp * Q_ROWS, pp * N_HEADS)), shared((PAGE, PAGE)), hbm, hbm, hbm],
            out_specs=_per_seq1((Q_ROWS, HEAD_DIM)),
            scratch_shapes=[pltpu.VMEM((PAGE_SLOTS, pp, N_HEADS, PAGE), F32)] + _kv_page_scratch(pp)[:2]
            + [pltpu.SemaphoreType.DMA((PAGE_SLOTS,))] * 3
            + _softmax_scratch() + [pltpu.VMEM((Q_ROWS, 1), F32), pltpu.VMEM((Q_ROWS, 1), F32)],
        ),
        out_shape=jax.ShapeDtypeStruct((bd, Q_ROWS, HEAD_DIM), F32),
        compiler_params=_params("arbitrary"),
        name="fox_sample",
    )(page_table, q_s, lft_new, k_new, v_new, rep, later, lf_cache_t, k_cache, v_cache)


def _merge_body(x_ref, oa_ref, ob_ref, ga_ref, gb_ref, wa_ref, wb_ref, wo_ref, gn_ref, x1_ref, h2_ref):
    a = jnp.dot(oa_ref[...].astype(BF16), wa_ref[...], preferred_element_type=F32)
    b = jnp.dot(ob_ref[...].astype(BF16), wb_ref[...], preferred_element_type=F32)
    merged = jax.nn.sigmoid(ga_ref[...].astype(F32)) * a + jax.nn.sigmoid(gb_ref[...].astype(F32)) * b
    x1 = x_ref[...] + jnp.dot(merged.astype(BF16), wo_ref[...], preferred_element_type=F32)
    x1_ref[...] = x1
    h2_ref[...] = _rms(x1, gn_ref[...]).astype(BF16)


def _merge(x2d, o_a, o_b, proj, wa, wb, wo, ffn_norm, lay, tm):
    n, d = x2d.shape
    const = lambda shape: pl.BlockSpec(shape, lambda i: (0, 0), pipeline_mode=pl.Buffered(1))
    return pl.pallas_call(
        _merge_body,
        grid=(n // tm,),
        in_specs=[
            pl.BlockSpec((tm, d), lambda i: (i, 0)),
            pl.BlockSpec((tm, 1024), lambda i: (i, 0)),
            pl.BlockSpec((tm, 1024), lambda i: (i, 0)),
            pl.BlockSpec((tm, d), lambda i: (i, lay.ga // d)),
            pl.BlockSpec((tm, d), lambda i: (i, lay.gb // d)),
            const((1024, d)), const((1024, d)), const((d, d)), const((1, d)),
        ],
        out_specs=[pl.BlockSpec((tm, d), lambda i: (i, 0)), pl.BlockSpec((tm, d), lambda i: (i, 0))],
        out_shape=[jax.ShapeDtypeStruct((n, d), F32), jax.ShapeDtypeStruct((n, d), BF16)],
        compiler_params=_params("arbitrary"),
        name="merge",
    )(x2d, o_a, o_b, proj, proj, wa, wb, wo, ffn_norm)


FFN_ROW_CHUNKS = 2


def _ffn_zero_acc(f, acc_ref):
    @pl.when(f == 0)
    def _():
        acc_ref[...] = jnp.zeros(acc_ref.shape, F32)


def _ffn_finish(f, n_f, x1_ref, fn_ref, y_ref, acc_ref):
    @pl.when(f == n_f - 1)
    def _():
        y_ref[...] = _rms(x1_ref[...] + acc_ref[...], fn_ref[...])


def _ffn_rows(h, first, rows, n_rows, taps, wg_ref, wu_ref, wd_ref, cw_ref, cb_ref, ext_ref, acc_ref):
    gp = jnp.dot(h, wg_ref[...], preferred_element_type=F32)
    up = jnp.dot(h, wu_ref[...], preferred_element_type=F32)
    ext_ref[first + rows:first + rows + n_rows, :] = gp
    conv = cb_ref[...]
    for j, back in enumerate(taps):
        src = gp if back == 0 else ext_ref[first + rows - back:first + rows - back + n_rows, :]
        conv = conv + cw_ref[j:j + 1, :] * src
    act = (conv * jax.nn.sigmoid(conv)) * up
    acc_ref[rows:rows + n_rows, :] += jnp.dot(act.astype(BF16), wd_ref[...], preferred_element_type=F32)


def _ffn_prompt_body(h_ref, halo_ref, wg_ref, wu_ref, wd_ref, cw_ref, cb_ref, x1_ref, fn_ref,
                     y_ref, tail_ref, acc_ref, ext_ref, *, tiles_per_seq):
    i = pl.program_id(0)
    f = pl.program_id(1)
    tm = h_ref.shape[0]
    _ffn_zero_acc(f, acc_ref)
    halo = jnp.dot(halo_ref[...], wg_ref[...], preferred_element_type=F32)
    ext_ref[0:SUBLANES, :] = jnp.where(i % tiles_per_seq == 0, 0.0, halo)
    taps = tuple(CONV_W - 1 - j for j in range(CONV_W))
    rc = tm // FFN_ROW_CHUNKS
    for c in range(FFN_ROW_CHUNKS):
        _ffn_rows(h_ref[c * rc:(c + 1) * rc, :], SUBLANES, c * rc, rc, taps,
                  wg_ref, wu_ref, wd_ref, cw_ref, cb_ref, ext_ref, acc_ref)
    tail_ref[...] = ext_ref[tm:tm + SUBLANES, :]
    _ffn_finish(f, pl.num_programs(1), x1_ref, fn_ref, y_ref, acc_ref)


def _ffn_prompt(h2, x1, wg, wu, wd, conv_w, conv_b, final_norm, s, tm, tf):
    n, d = x1.shape
    ff = wg.shape[1]
    assert s % tm == 0 and tm % SUBLANES == 0
    hb = tm // SUBLANES
    return pl.pallas_call(
        functools.partial(_ffn_prompt_body, tiles_per_seq=s // tm),
        grid=(n // tm, ff // tf),
        in_specs=[
            pl.BlockSpec((tm, d), lambda i, f: (i, 0)),
            pl.BlockSpec((SUBLANES, d), lambda i, f: (jnp.maximum(i * hb - 1, 0), 0)),
            pl.BlockSpec((d, tf), lambda i, f: (0, f)),
            pl.BlockSpec((d, tf), lambda i, f: (0, f)),
            pl.BlockSpec((tf, d), lambda i, f: (f, 0)),
            pl.BlockSpec((CONV_W, tf), lambda i, f: (0, f)),
            pl.BlockSpec((1, tf), lambda i, f: (0, f)),
            pl.BlockSpec((tm, d), lambda i, f: (i, 0)),
            pl.BlockSpec((1, d), lambda i, f: (0, 0)),
        ],
        out_specs=[pl.BlockSpec((tm, d), lambda i, f: (i, 0)),
                   pl.BlockSpec((SUBLANES, tf), lambda i, f: (i, f))],
        out_shape=[jax.ShapeDtypeStruct((n, d), F32), jax.ShapeDtypeStruct((n // tm * SUBLANES, ff), F32)],
        scratch_shapes=[pltpu.VMEM((tm, d), F32), pltpu.VMEM((tm + SUBLANES, tf), F32)],
        compiler_params=_params("arbitrary", "arbitrary"),
        name="ffn_prompt",
    )(h2, h2, wg, wu, wd, conv_w, conv_b, x1, final_norm)


def _ffn_sample_body(h_ref, st_ref, wg_ref, wu_ref, wd_ref, cw_ref, cb_ref, x1_ref, fn_ref,
                     y_ref, new_st_ref, acc_ref, ext_ref, *, bd):
    f = pl.program_id(0)
    n = h_ref.shape[0]
    keep = (CONV_W - 1) * bd
    _ffn_zero_acc(f, acc_ref)
    ext_ref[0:keep, :] = st_ref[...]
    taps = tuple((CONV_W - 1 - j) * bd for j in range(CONV_W))
    _ffn_rows(h_ref[...], keep, 0, n, taps, wg_ref, wu_ref, wd_ref, cw_ref, cb_ref, ext_ref, acc_ref)
    new_st_ref[...] = ext_ref[n:n + keep, :]
    _ffn_finish(f, pl.num_programs(0), x1_ref, fn_ref, y_ref, acc_ref)


def _ffn_sample(h2, x1, state, wg, wu, wd, conv_w, conv_b, final_norm, bd, tf):
    n, d = x1.shape
    ff = wg.shape[1]
    keep = (CONV_W - 1) * bd
    assert bd % SUBLANES == 0 and n >= keep
    return pl.pallas_call(
        functools.partial(_ffn_sample_body, bd=bd),
        grid=(ff // tf,),
        in_specs=[
            pl.BlockSpec((n, d), lambda f: (0, 0)),
            pl.BlockSpec((keep, tf), lambda f: (0, f)),
            pl.BlockSpec((d, tf), lambda f: (0, f)),
            pl.BlockSpec((d, tf), lambda f: (0, f)),
            pl.BlockSpec((tf, d), lambda f: (f, 0)),
            pl.BlockSpec((CONV_W, tf), lambda f: (0, f)),
            pl.BlockSpec((1, tf), lambda f: (0, f)),
            pl.BlockSpec((n, d), lambda f: (0, 0)),
            pl.BlockSpec((1, d), lambda f: (0, 0)),
        ],
        out_specs=[pl.BlockSpec((n, d), lambda f: (0, 0)), pl.BlockSpec((keep, tf), lambda f: (0, f))],
        out_shape=[jax.ShapeDtypeStruct((n, d), F32), jax.ShapeDtypeStruct((keep, ff), F32)],
        scratch_shapes=[pltpu.VMEM((n, d), F32), pltpu.VMEM((keep + n, tf), F32)],
        compiler_params=_params("arbitrary"),
        name="ffn_sample",
    )(h2, state, wg, wu, wd, conv_w, conv_b, x1, final_norm)


def _largest_divisor(n, candidates):
    for c in candidates:
        if n % c == 0:
            return c
    raise ValueError(f"no tile for {n} among {candidates}")


def kernel(x_prompt, x_sample, cache_dsa_k, cache_dsa_v, cache_idx_k, cache_fox_k, cache_fox_v, cache_fox_logf, state_ffn_conv, page_table, attn_norm, w_in, b_forget, w_branch_a, w_branch_b, w_out, ffn_norm, w_gate, w_up, w_down, conv_w, conv_b, final_norm):
    b, s, d = x_prompt.shape
    bd, t_new, _ = x_sample.shape
    depth = attn_norm.shape[0]
    assert depth == 1 and t_new * GROUP == SUBLANES and s % Q_BLOCK == 0
    n_pages = page_table.shape[1]
    past = n_pages * PAGE
    n_pool = cache_dsa_k.shape[1]
    ff = w_gate.shape[2]
    lay = _Layout(d)

    w_perm = _prep_w_in_t(jnp.swapaxes(w_in[0], 0, 1), lay)
    bf_row = jnp.zeros((1, LANES), F32).at[0, LOGF_LANE:LOGF_LANE + N_HEADS].set(b_forget[0])
    wa, wb, wo = (w[0].astype(BF16) for w in (w_branch_a, w_branch_b, w_out))
    wg, wu, wd = (w[0].astype(BF16) for w in (w_gate, w_up, w_down))
    g_attn, g_ffn, g_fin = attn_norm[0][None, :], ffn_norm[0][None, :], final_norm[None, :]
    cw, cb = conv_w[0], conv_b[0][None, :]

    tm_p = _largest_divisor(s, (1024, 512, 256, 128))
    tab_p = _rope_tables(jnp.arange(s, dtype=I32))
    tab_s = jnp.tile(_rope_tables(past + jnp.arange(t_new, dtype=I32)), (bd, 1))
    xp2 = x_prompt.reshape(b * s, d)
    xs2 = x_sample.reshape(bd * t_new, d)
    proj_p, small_p, *kv_p = _norm_proj(xp2, g_attn, w_perm, tab_p, bf_row, lay, tm_p)
    proj_s, small_s, *kv_s = _norm_proj(xs2, g_attn, w_perm, tab_s, bf_row, lay, bd * t_new)

    oa_p = _dsa_prompt(proj_p, small_p, kv_p[0], kv_p[1], b, s, lay)
    ob_p = _fox_prompt(proj_p, small_p, kv_p[2], kv_p[3], b, s, lay)

    def cols(name, width):
        o = getattr(lay, name)
        return proj_s[:, o:o + width].astype(F32).reshape(bd, t_new, width)

    def heads_major(x):
        x = x.reshape(bd, t_new, N_KV, GROUP, HEAD_DIM).transpose(0, 2, 3, 1, 4)
        return x.reshape(bd, Q_ROWS, HEAD_DIM)

    def new_kv(x):
        x = x.reshape(bd, t_new, N_KV, HEAD_DIM).transpose(0, 2, 1, 3)
        return jnp.pad(x, ((0, 0), (0, 0), (0, LANES - t_new), (0, 0)))

    def heads_back(o):
        o = o.reshape(bd, N_KV, GROUP, t_new, HEAD_DIM).transpose(0, 3, 1, 2, 4)
        return o.reshape(bd * t_new, N_HEADS * HEAD_DIM)

    ka_s, va_s, kb_s, vb_s = (x.reshape(bd, t_new, N_KV * HEAD_DIM) for x in kv_s)
    ik_s = small_s[:, lay.small_ik:lay.small_ik + IDX_DIM].reshape(bd, t_new, IDX_DIM)
    iwf_s = small_s[:, lay.small_iwf:lay.small_iwf + LANES].reshape(bd, t_new, LANES)
    logf_s = iwf_s[..., LOGF_LANE:LOGF_LANE + N_HEADS]
    iq_s = cols("iq", 1024).reshape(bd, t_new, IDX_HEADS, IDX_DIM).transpose(0, 2, 1, 3)
    iq2 = jnp.broadcast_to(iq_s[:, :, None], (bd, IDX_HEADS, GROUP, t_new, IDX_DIM)).reshape(bd, IDX_HEADS * SUBLANES, IDX_DIM)
    iw_s = iwf_s[..., IW_LANE:IW_LANE + IDX_HEADS].transpose(0, 2, 1)
    iw2 = jnp.broadcast_to(iw_s[:, :, None], (bd, IDX_HEADS, GROUP, t_new)).reshape(bd, IDX_HEADS * SUBLANES, 1)
    ikt_new = jnp.pad(ik_s.transpose(0, 2, 1), ((0, 0), (0, 0), (0, LANES - t_new)))
    lft_new = jnp.pad(logf_s.transpose(0, 2, 1), ((0, 0), (0, 0), (0, LANES - t_new)))

    ik_cache_t = jnp.swapaxes(cache_idx_k[0], 1, 2)
    lf_cache_t = jnp.swapaxes(cache_fox_logf[0], 1, 2)
    kv_rows = lambda c: c.reshape(n_pool, PAGE * N_KV, HEAD_DIM)

    scores = _dsa_sample_keys(page_table, iq2, iw2, ikt_new, ik_cache_t, t_new)
    scores, thr = _dsa_sample_select(scores, min(TOPK_MAX, (past + t_new) // 4), _largest_divisor(bd, (8, 4, 2, 1)))
    oa_s = heads_back(_dsa_sample_attend(page_table, heads_major(cols("qa", 1024)), scores, thr,
                                         new_kv(ka_s), new_kv(va_s), kv_rows(cache_dsa_k), kv_rows(cache_dsa_v)))
    ob_s = heads_back(_fox_sample(page_table, heads_major(cols("qb", 1024)), lft_new, new_kv(kb_s), new_kv(vb_s),
                                  lf_cache_t, kv_rows(cache_fox_k), kv_rows(cache_fox_v), t_new))

    tm_m = _largest_divisor(s, (256, 128))
    x1_p, h2_p = _merge(xp2, oa_p, ob_p, proj_p, wa, wb, wo, g_ffn, lay, tm_m)
    x1_s, h2_s = _merge(xs2, oa_s, ob_s, proj_s, wa, wb, wo, g_ffn, lay, bd * t_new)

    tf = _largest_divisor(ff, (512, 256, 128))
    tm_f = _largest_divisor(s, (512, 256, 128))
    y_p, tails = _ffn_prompt(h2_p, x1_p, wg, wu, wd, cw, cb, g_fin, s, tm_f, tf)
    conv_p = tails.reshape(b, s // tm_f, SUBLANES, ff)[:, -1, SUBLANES - (CONV_W - 1):, :]

    t_major = lambda x: x.reshape(bd, t_new, -1).transpose(1, 0, 2).reshape(t_new * bd, -1)
    state_t = state_ffn_conv[0].transpose(1, 0, 2).reshape((CONV_W - 1) * bd, ff)
    y_s_t, st_t = _ffn_sample(t_major(h2_s), t_major(x1_s), state_t, wg, wu, wd, cw, cb, g_fin, bd, tf)
    y_s = y_s_t.reshape(t_new, bd, d).transpose(1, 0, 2)
    conv_s = st_t.reshape(CONV_W - 1, bd, ff).transpose(1, 0, 2)

    p_kv = [x.reshape(1, b, s, N_KV, HEAD_DIM) for x in kv_p]
    logf_at = lay.small_iwf + LOGF_LANE
    p_out = (p_kv[0], p_kv[1], small_p[:, lay.small_ik:lay.small_ik + IDX_DIM].reshape(1, b, s, IDX_DIM), p_kv[2], p_kv[3],
             small_p[:, logf_at:logf_at + N_HEADS].reshape(1, b, s, N_HEADS),
             conv_p[None])
    s_out = (ka_s.reshape(1, bd, t_new, N_KV, HEAD_DIM), va_s.reshape(1, bd, t_new, N_KV, HEAD_DIM), ik_s[None],
             kb_s.reshape(1, bd, t_new, N_KV, HEAD_DIM), vb_s.reshape(1, bd, t_new, N_KV, HEAD_DIM), logf_s[None],
             conv_s[None])
    return (y_p.reshape(b, s, d), y_s) + p_out + s_out
```

```python
import functools

import numpy as np
import jax
import jax.numpy as jnp
from jax import lax
from jax.experimental import pallas as pl
from jax.experimental.pallas import tpu as pltpu

F32 = jnp.float32
BF16 = jnp.bfloat16
I32 = jnp.int32

HEAD_DIM = 128
N_HEADS = 8
N_KV = 4
GROUP = N_HEADS // N_KV
IDX_HEADS = 16
IDX_DIM = 64
TOPK_MAX = 256
ROPE_THETA = 500000.0
ROT_DIM = HEAD_DIM // 4
IDX_ROT_DIM = IDX_DIM // 4
PAGE = 128
Q_BLOCK = 128
CONV_W = 3
RMS_EPS = 1e-6
ATT_SCALE = HEAD_DIM ** -0.5
IDX_SCALE = (IDX_HEADS * IDX_DIM) ** -0.5

LANES = 128
SUBLANES = 8
NEG = -1e30
NEG_INF = float("-inf")
INT_MIN = -2 ** 31
VMEM_LIMIT = 56 * 1024 * 1024

IW_LANE = 0
LOGF_LANE = IDX_HEADS

KV_NAMES = ("ka", "va", "kb", "vb")

NT_DIMS = (((1,), (1,)), ((), ()))


def _params(*sem):
    return pltpu.CompilerParams(dimension_semantics=sem, vmem_limit_bytes=VMEM_LIMIT)


def _nt(a, b):
    return lax.dot_general(a, b, NT_DIMS, preferred_element_type=F32)


def _rms(x, g):
    ms = jnp.mean(x * x, axis=-1, keepdims=True)
    return (x * lax.rsqrt(ms + RMS_EPS)) * g


class _Layout:
    def __init__(self, d_model):
        self.d = d_model
        self.tn = 512
        off = 0
        for name, size in (("ga", d_model), ("gb", d_model), ("qa", 1024), ("iq", 1024), ("qb", 1024)):
            assert off % size == 0, (name, off, size)
            setattr(self, name, off)
            off += size
        assert off % self.tn == 0
        self.proj_cols = off
        self.ik, self.iwf = off, off + LANES
        self.small_ik, self.small_iwf = 0, LANES
        off += self.tn
        for name in KV_NAMES:
            setattr(self, name, off)
            off += N_KV * HEAD_DIM
        assert N_KV * HEAD_DIM == self.tn
        self.nc = off

    def chunk_kinds(self):
        kinds = ["plain"] * (self.nc // LANES)
        for name, size, kind in (("qa", 1024, "qrope"), ("qb", 1024, "q"), ("iq", 1024, "rope64"),
                                 ("ik", self.tn, "s:plain"), ("ik", LANES, "s:rope64"), ("iwf", LANES, "s:iwf"),
                                 ("ka", 512, "kv0rope"), ("va", 512, "kv1"), ("kb", 512, "kv2"), ("vb", 512, "kv3")):
            start = getattr(self, name) // LANES
            for c in range(size // LANES):
                kinds[start + c] = kind
        return kinds


def _w_in_plan(lay):
    d = lay.d
    sizes = (1024, 512, 512, 1024, IDX_DIM, IDX_HEADS, 1024, 512, 512, N_HEADS, d, d)
    names = ("qa", "ka", "va", "iq", "ik", "iw", "qb", "kb", "vb", "fl", "ga", "gb")
    offs = np.concatenate([[0], np.cumsum(sizes)])
    src = {n: int(offs[k]) for k, n in enumerate(names)}
    plan = [()] * (lay.nc // LANES)
    for name, size in (("ga", d), ("gb", d), ("qa", 1024), ("iq", 1024), ("qb", 1024),
                       ("ka", 512), ("va", 512), ("kb", 512), ("vb", 512)):
        for c in range(size // LANES):
            plan[getattr(lay, name) // LANES + c] = ((src[name] + c * LANES, 0, LANES),)
    plan[lay.ik // LANES] = ((src["ik"], 0, IDX_DIM),)
    plan[lay.iwf // LANES] = ((src["iw"], IW_LANE, IDX_HEADS), (src["fl"], LOGF_LANE, N_HEADS))
    return tuple(plan), int(offs[-1])


def _prep_w_body(w_ref, o_ref, *, plan):
    cols = w_ref.shape[1]
    for c, pieces in enumerate(plan):
        parts, pos = [], 0
        for first, at, height in pieces:
            if at > pos:
                parts.append(jnp.zeros((at - pos, cols), F32))
            parts.append(w_ref[first:first + height, :])
            pos = at + height
        if pos < LANES:
            parts.append(jnp.zeros((LANES - pos, cols), F32))
        chunk = parts[0] if len(parts) == 1 else jnp.concatenate(parts, axis=0)
        o_ref[c * LANES:(c + 1) * LANES, :] = chunk.astype(BF16)


def _prep_w_in_t(w_in_t, lay):
    n_src, d = w_in_t.shape
    plan, n_cols = _w_in_plan(lay)
    assert n_cols == n_src and all(f % SUBLANES == 0 and a % SUBLANES == 0 for p in plan for f, a, _ in p)
    tc = _largest_divisor(d, (256, 128))
    return pl.pallas_call(
        functools.partial(_prep_w_body, plan=plan),
        grid=(d // tc,),
        in_specs=[pl.BlockSpec((n_src, tc), lambda i: (0, i))],
        out_specs=pl.BlockSpec((lay.nc, tc), lambda i: (0, i)),
        out_shape=jax.ShapeDtypeStruct((lay.nc, d), BF16),
        compiler_params=_params("arbitrary"),
        name="prep_w_in",
    )(w_in_t)


def _rope_tables(pos):
    def one(rot_dim, period):
        half = rot_dim // 2
        inv_freq = jnp.power(ROPE_THETA, -jnp.arange(half, dtype=F32) * (2.0 / rot_dim))
        ang = pos.astype(F32)[:, None] * inv_freq[None, :]
        cos, sin = jnp.cos(ang), jnp.sin(ang)
        n = pos.shape[0]
        c = jnp.concatenate([cos, cos, jnp.ones((n, period - rot_dim), F32)], axis=1)
        sa = jnp.concatenate([-sin, jnp.zeros((n, period - half), F32)], axis=1)
        sb = jnp.concatenate([jnp.zeros((n, half), F32), sin, jnp.zeros((n, period - rot_dim), F32)], axis=1)
        rep = LANES // period
        return [jnp.tile(t, (1, rep)) for t in (c, sa, sb)]
    return jnp.concatenate(one(ROT_DIM, HEAD_DIM) + one(IDX_ROT_DIM, IDX_DIM), axis=1)


def _proj_body(x_ref, g_ref, w_ref, tab_ref, bf_ref, o_ref, small_ref, *rest, tile_kinds):
    kv_refs, h_ref = rest[:len(KV_NAMES)], rest[len(KV_NAMES)]
    j = pl.program_id(1)
    tm = x_ref.shape[0]

    @pl.when(j == 0)
    def _():
        h_ref[...] = _rms(x_ref[...], g_ref[...]).astype(BF16)

    acc = _nt(h_ref[...], w_ref[...])

    def rope(a, base, half):
        c = tab_ref[:, base:base + LANES]
        sa = tab_ref[:, base + LANES:base + 2 * LANES]
        sb = tab_ref[:, base + 2 * LANES:base + 3 * LANES]
        return a * c + pltpu.roll(a, LANES - half, 1) * sa + pltpu.roll(a, half, 1) * sb

    def indexer_weight_and_log_forget(a):
        z = a + bf_ref[...]
        ls = jnp.minimum(z, 0.0) - jnp.log1p(jnp.exp(-jnp.abs(z)))
        lane = lax.broadcasted_iota(I32, a.shape, 1)
        return jnp.where(lane < LOGF_LANE, a * IDX_SCALE, jnp.where(lane < LOGF_LANE + N_HEADS, ls, a))

    def emit(kinds):
        for c, kind in enumerate(kinds):
            a = acc[:, c * LANES:(c + 1) * LANES]
            to_small = kind.startswith("s:")
            kind = kind[2:] if to_small else kind
            if kind == "qrope":
                a = rope(a, 0, ROT_DIM // 2) * ATT_SCALE
            elif kind == "q":
                a = a * ATT_SCALE
            elif kind == "rope64":
                a = rope(a, 3 * LANES, IDX_ROT_DIM // 2)
            elif kind == "iwf":
                a = indexer_weight_and_log_forget(a)
            if kind.startswith("kv"):
                if kind.endswith("rope"):
                    a = rope(a, 0, ROT_DIM // 2)
                kv_refs[int(kind[2])][pl.ds(c, tm, stride=N_KV), :] = a
            elif to_small:
                small_ref[:, c * LANES:(c + 1) * LANES] = a
            else:
                o_ref[:, c * LANES:(c + 1) * LANES] = a.astype(BF16)

    groups = {}
    for t, kinds in enumerate(tile_kinds):
        groups.setdefault(kinds, []).append(t)
    for kinds, tiles in groups.items():
        cond = functools.reduce(jnp.logical_or, [j == t for t in tiles])
        if all(k == "plain" for k in kinds):
            @pl.when(cond)
            def _():
                o_ref[...] = acc.astype(BF16)
        else:
            pl.when(cond)(functools.partial(emit, kinds))


def _norm_proj(x2d, gamma, w_perm, tab, bf_row, lay, tm):
    n, d = x2d.shape
    tn = lay.tn
    kinds = lay.chunk_kinds()
    per = tn // LANES
    tile_kinds = tuple(tuple(kinds[t * per:(t + 1) * per]) for t in range(lay.nc // tn))
    tab_blocks = tab.shape[0] // tm
    last_proj_tile = lay.proj_cols // tn - 1
    kv_spec = pl.BlockSpec((tm * N_KV, HEAD_DIM), lambda i, j: (i, 0))
    return pl.pallas_call(
        functools.partial(_proj_body, tile_kinds=tile_kinds),
        grid=(n // tm, lay.nc // tn),
        in_specs=[
            pl.BlockSpec((tm, d), lambda i, j: (i, 0), pipeline_mode=pl.Buffered(1)),
            pl.BlockSpec((1, d), lambda i, j: (0, 0)),
            pl.BlockSpec((tn, d), lambda i, j: (j, 0)),
            pl.BlockSpec((tm, 6 * LANES), lambda i, j: (i % tab_blocks, 0)),
            pl.BlockSpec((1, LANES), lambda i, j: (0, 0)),
        ],
        out_specs=[pl.BlockSpec((tm, tn), lambda i, j: (i, jnp.minimum(j, last_proj_tile))),
                   pl.BlockSpec((tm, tn), lambda i, j: (i, 0))] + [kv_spec] * len(KV_NAMES),
        out_shape=[jax.ShapeDtypeStruct((n, lay.proj_cols), BF16), jax.ShapeDtypeStruct((n, tn), F32)]
        + [jax.ShapeDtypeStruct((n * N_KV, HEAD_DIM), F32)] * len(KV_NAMES),
        scratch_shapes=[pltpu.VMEM((tm, d), BF16)],
        compiler_params=_params("arbitrary", "arbitrary"),
        name="norm_proj",
    )(x2d, gamma, w_perm, tab, bf_row)


def _code_to_float(code):
    bits = jnp.where(code < 0, code ^ jnp.int32(0x7FFFFFFF), code)
    return pltpu.bitcast(bits, F32)


def _kth_largest(read_scores, rows, k):
    def body(it, code):
        cand = code + jnp.left_shift(jnp.int32(1), 31 - it)
        cnt = jnp.sum(jnp.where(read_scores() >= _code_to_float(cand), 1.0, 0.0), axis=1, keepdims=True)
        return jnp.where(cnt >= k, cand, code)

    code = lax.fori_loop(0, 32, body, jnp.full((rows, 1), INT_MIN, I32))
    return _code_to_float(code), code == INT_MIN


def _count(mask):
    return jnp.sum(jnp.where(mask, 1.0, 0.0), axis=1, keepdims=True)


def _total(mask):
    ones = jnp.where(mask, 1.0, 0.0)
    return jnp.sum(jnp.sum(ones, axis=0, keepdims=True), axis=1, keepdims=True)[0, 0]


def _earlier_in_chunk():
    r = lax.broadcasted_iota(I32, (LANES, LANES), 0)
    c = lax.broadcasted_iota(I32, (LANES, LANES), 1)
    return jnp.where(r < c, 1.0, 0.0).astype(BF16)


def _tied_keys_to_keep(eq, seen, need, earlier):
    ones = jnp.where(eq, 1.0, 0.0)
    rank = seen + jnp.dot(ones.astype(BF16), earlier, preferred_element_type=F32)
    return eq & (rank < need), seen + jnp.sum(ones, axis=1, keepdims=True)


KEY_BUCKET = 256


def _key_limits(s):
    step = min(KEY_BUCKET, s)
    assert s % step == 0
    return tuple(range(step, s + 1, step))


def _for_causal_limit(i, limits, block_fn):
    q_end = (i + 1) * Q_BLOCK
    prev = 0
    for lim in limits:
        pl.when((q_end > prev) & (q_end <= lim))(functools.partial(block_fn, lim))
        prev = lim


def _load_kv_heads(src_ref, dst_ref):
    tokens = dst_ref.shape[0]
    for kh in range(N_KV):
        dst_ref[:, kh * HEAD_DIM:(kh + 1) * HEAD_DIM] = src_ref[pl.ds(kh, tokens, stride=N_KV), :].astype(BF16)


def _attend_heads(q_ref, kb_ref, vb_ref, o_ref, n, logit_bias):
    for kh in range(N_KV):
        kk = kb_ref[0:n, kh * HEAD_DIM:(kh + 1) * HEAD_DIM]
        vv = vb_ref[0:n, kh * HEAD_DIM:(kh + 1) * HEAD_DIM]
        for g in range(GROUP):
            h = kh * GROUP + g
            q = q_ref[:, h * HEAD_DIM:(h + 1) * HEAD_DIM]
            lg = _nt(q, kk) + logit_bias(h)
            m = jnp.max(lg, axis=1, keepdims=True)
            e = jnp.exp(lg - m)
            l = jnp.sum(e, axis=1, keepdims=True)
            o = jnp.dot(e.astype(BF16), vv, preferred_element_type=F32)
            o_ref[:, h * HEAD_DIM:(h + 1) * HEAD_DIM] = o / l


def _dsa_prompt_body(iq_ref, iwf_ref, ik_ref, q_ref, k_ref, v_ref, o_ref,
                     ikb_ref, kb_ref, vb_ref, sc_ref, bias_ref, *, topk, limits):
    i = pl.program_id(1)

    @pl.when(i == 0)
    def _():
        ikb_ref[...] = ik_ref[:, :IDX_DIM].astype(BF16)
        _load_kv_heads(k_ref, kb_ref)
        _load_kv_heads(v_ref, vb_ref)

    def block(n):
        ikb = ikb_ref[0:n, :]
        for h in range(IDX_HEADS):
            qh = iq_ref[:, h * IDX_DIM:(h + 1) * IDX_DIM]
            s = jnp.maximum(_nt(qh, ikb), 0.0) * iwf_ref[:, IW_LANE + h:IW_LANE + h + 1]
            if h == 0:
                sc_ref[:, 0:n] = s
            else:
                sc_ref[:, 0:n] += s
        row = i * Q_BLOCK + lax.broadcasted_iota(I32, (Q_BLOCK, n), 0)
        col = lax.broadcasted_iota(I32, (Q_BLOCK, n), 1)
        causal = col <= row
        sc_ref[:, 0:n] = jnp.where(causal, sc_ref[:, 0:n], NEG_INF)
        thr, none = _kth_largest(lambda: sc_ref[:, 0:n], Q_BLOCK, topk)
        chosen = ((sc_ref[:, 0:n] >= thr) | none) & causal
        bias_ref[:, 0:n] = jnp.where(chosen, 0.0, NEG)

        first = i * Q_BLOCK
        short = jnp.clip(topk - first, 0, Q_BLOCK)
        expected = short * first + (short * (short + 1)) // 2 + (Q_BLOCK - short) * topk

        @pl.when(_total(chosen) > expected.astype(F32))
        def _():
            need = topk - _count((sc_ref[:, 0:n] > thr) & causal)
            earlier = _earlier_in_chunk()
            seen = jnp.zeros((Q_BLOCK, 1), F32)
            for c in range(n // LANES):
                lanes = slice(c * LANES, (c + 1) * LANES)
                sc = sc_ref[:, lanes]
                ok = (c * LANES + lax.broadcasted_iota(I32, (Q_BLOCK, LANES), 1)
                      <= first + lax.broadcasted_iota(I32, (Q_BLOCK, LANES), 0))
                keep, seen = _tied_keys_to_keep((sc == thr) & ok, seen, need, earlier)
                bias_ref[:, lanes] = jnp.where((((sc > thr) | none) & ok) | keep, 0.0, NEG)
        _attend_heads(q_ref, kb_ref, vb_ref, o_ref, n, lambda h: bias_ref[:, 0:n])

    _for_causal_limit(i, limits, block)


def _dsa_prompt(proj, small, k4, v4, b, s, lay):
    nb = s // Q_BLOCK
    topk = min(TOPK_MAX, s // 4)
    row = lambda bb, i: bb * nb + i
    return pl.pallas_call(
        functools.partial(_dsa_prompt_body, topk=topk, limits=_key_limits(s)),
        grid=(b, nb),
        in_specs=[
            pl.BlockSpec((Q_BLOCK, 1024), lambda bb, i: (row(bb, i), lay.iq // 1024)),
            pl.BlockSpec((Q_BLOCK, LANES), lambda bb, i: (row(bb, i), lay.small_iwf // LANES)),
            pl.BlockSpec((s, LANES), lambda bb, i: (bb, lay.small_ik // LANES)),
            pl.BlockSpec((Q_BLOCK, 1024), lambda bb, i: (row(bb, i), lay.qa // 1024)),
            pl.BlockSpec((s * N_KV, HEAD_DIM), lambda bb, i: (bb, 0)),
            pl.BlockSpec((s * N_KV, HEAD_DIM), lambda bb, i: (bb, 0)),
        ],
        out_specs=pl.BlockSpec((Q_BLOCK, 1024), lambda bb, i: (row(bb, i), 0)),
        out_shape=jax.ShapeDtypeStruct((b * s, 1024), F32),
        scratch_shapes=[pltpu.VMEM((s, IDX_DIM), BF16), pltpu.VMEM((s, 512), BF16), pltpu.VMEM((s, 512), BF16),
                        pltpu.VMEM((Q_BLOCK, s), F32), pltpu.VMEM((Q_BLOCK, s), F32)],
        compiler_params=_params("arbitrary", "arbitrary"),
        name="dsa_prompt",
    )(proj, small, small, proj, k4, v4)


CUM_BLOCK = 256


def _fox_prompt_body(q_ref, k_ref, v_ref, lf_ref, o_ref, kb_ref, vb_ref, c_ref, ct_ref, bias_ref, *, limits):
    i = pl.program_id(1)
    s_len = kb_ref.shape[0]

    @pl.when(i == 0)
    def _():
        _load_kv_heads(k_ref, kb_ref)
        _load_kv_heads(v_ref, vb_ref)
        r = lax.broadcasted_iota(I32, (CUM_BLOCK, CUM_BLOCK), 0)
        c = lax.broadcasted_iota(I32, (CUM_BLOCK, CUM_BLOCK), 1)
        tri = jnp.where(c <= r, 1.0, 0.0).astype(F32)
        carry = jnp.zeros((1, LANES), F32)
        for blk in range(s_len // CUM_BLOCK):
            xb = lf_ref[blk * CUM_BLOCK:(blk + 1) * CUM_BLOCK, :]
            cb = jnp.dot(tri, xb, precision=lax.Precision.HIGHEST, preferred_element_type=F32) + carry
            c_ref[blk * CUM_BLOCK:(blk + 1) * CUM_BLOCK, :] = cb
            carry = cb[CUM_BLOCK - 1:CUM_BLOCK, :]
        ct_ref[...] = c_ref[...].T

    start = pl.multiple_of(i * Q_BLOCK, Q_BLOCK)

    def block(n):
        row = i * Q_BLOCK + lax.broadcasted_iota(I32, (Q_BLOCK, n), 0)
        col = lax.broadcasted_iota(I32, (Q_BLOCK, n), 1)
        bias_ref[:, 0:n] = jnp.where(col <= row, 0.0, NEG)

        def logit_bias(h):
            cq = c_ref[pl.ds(start, Q_BLOCK), LOGF_LANE + h:LOGF_LANE + h + 1]
            ck = ct_ref[LOGF_LANE + h:LOGF_LANE + h + 1, 0:n]
            return (cq - ck) + bias_ref[:, 0:n]

        _attend_heads(q_ref, kb_ref, vb_ref, o_ref, n, logit_bias)

    _for_causal_limit(i, limits, block)


def _fox_prompt(proj, small, k4, v4, b, s, lay):
    nb = s // Q_BLOCK
    assert s % CUM_BLOCK == 0
    row = lambda bb, i: bb * nb + i
    return pl.pallas_call(
        functools.partial(_fox_prompt_body, limits=_key_limits(s)),
        grid=(b, nb),
        in_specs=[
            pl.BlockSpec((Q_BLOCK, 1024), lambda bb, i: (row(bb, i), lay.qb // 1024)),
            pl.BlockSpec((s * N_KV, HEAD_DIM), lambda bb, i: (bb, 0)),
            pl.BlockSpec((s * N_KV, HEAD_DIM), lambda bb, i: (bb, 0)),
            pl.BlockSpec((s, LANES), lambda bb, i: (bb, lay.small_iwf // LANES)),
        ],
        out_specs=pl.BlockSpec((Q_BLOCK, 1024), lambda bb, i: (row(bb, i), 0)),
        out_shape=jax.ShapeDtypeStruct((b * s, 1024), F32),
        scratch_shapes=[pltpu.VMEM((s, 512), BF16), pltpu.VMEM((s, 512), BF16),
                        pltpu.VMEM((s, LANES), F32), pltpu.VMEM((LANES, s), F32), pltpu.VMEM((Q_BLOCK, s), F32)],
        compiler_params=_params("arbitrary", "arbitrary"),
        name="fox_prompt",
    )(proj, k4, v4, small)


Q_ROWS = N_KV * SUBLANES


def _kv_rows(pages, kh):
    return jnp.concatenate([p[pl.ds(kh, PAGE, stride=N_KV), :] for p in pages], axis=0).astype(BF16)


def _softmax_update(q_ref, keys_of, values_of, bias, sel, m_ref, l_ref, acc_ref):
    lg = jnp.concatenate([_nt(_q_rows(q_ref, kh), keys_of(kh)) for kh in range(N_KV)], axis=0)
    if bias is not None:
        lg = lg + bias
    if sel is not None:
        lg = jnp.where(sel, lg, NEG)
    m_old = m_ref[...]
    m_new = jnp.maximum(m_old, jnp.max(lg, axis=1, keepdims=True))
    corr = jnp.exp(m_old - m_new)
    e = jnp.exp(lg - m_new)
    if sel is not None:
        e = jnp.where(sel, e, 0.0)
    l_ref[...] = l_ref[...] * corr + jnp.sum(e, axis=1, keepdims=True)
    pv = jnp.concatenate([jnp.dot(e[kh * SUBLANES:(kh + 1) * SUBLANES].astype(BF16), values_of(kh),
                                  preferred_element_type=F32) for kh in range(N_KV)], axis=0)
    acc_ref[...] = acc_ref[...] * corr + pv
    m_ref[...] = m_new


def _softmax_init(m_ref, l_ref, acc_ref):
    m_ref[...] = jnp.full(m_ref.shape, NEG, F32)
    l_ref[...] = jnp.zeros(l_ref.shape, F32)
    acc_ref[...] = jnp.zeros(acc_ref.shape, F32)


def _softmax_scratch():
    return [pltpu.VMEM((Q_ROWS, 1), F32), pltpu.VMEM((Q_ROWS, 1), F32), pltpu.VMEM((Q_ROWS, HEAD_DIM), F32)]


def _q_rows(q_ref, kh):
    return q_ref[kh * SUBLANES:(kh + 1) * SUBLANES, :].astype(BF16)


def _dsa_sample_keys_body(pt_ref, iq_ref, iw_ref, ikn_ref, ik_hbm, sc_ref, ikbuf, iksem, *, pp, n_chunks, n_new):
    past = n_chunks * pp * PAGE
    q = iq_ref[...].astype(BF16)
    w = iw_ref[...]

    def scores_of(ikt):
        s = jnp.maximum(jnp.dot(q, ikt.astype(BF16), preferred_element_type=F32), 0.0) * w
        acc = s[0:SUBLANES]
        for h in range(1, IDX_HEADS):
            acc = acc + s[h * SUBLANES:(h + 1) * SUBLANES]
        return acc

    row = lax.broadcasted_iota(I32, (SUBLANES, LANES), 0)
    lane = lax.broadcasted_iota(I32, (SUBLANES, LANES), 1)
    sc_ref[:, past:past + LANES] = jnp.where(lane <= row % n_new, scores_of(ikn_ref[...]), NEG_INF)

    def compute(c, slot):
        ikt = jnp.concatenate([ikbuf[slot, r] for r in range(pp)], axis=1)
        sc_ref[:, c * pp * PAGE:(c + 1) * pp * PAGE] = scores_of(ikt)

    _walk_page_chunks(pt_ref, [(ik_hbm, ikbuf, iksem)], pp, n_chunks, lambda c, r: c * pp + r, compute)


def _dsa_sample_keys(page_table, iq2, iw2, ikt_new, ik_cache_t, n_new):
    bd, n_pages = page_table.shape
    pp = _chunk_pages(n_pages, 16)
    past = n_pages * PAGE
    return pl.pallas_call(
        functools.partial(_dsa_sample_keys_body, pp=pp, n_chunks=n_pages // pp, n_new=n_new),
        grid_spec=pltpu.PrefetchScalarGridSpec(
            num_scalar_prefetch=1,
            grid=(bd,),
            in_specs=[_per_seq1((IDX_HEADS * SUBLANES, IDX_DIM)), _per_seq1((IDX_HEADS * SUBLANES, 1)),
                      _per_seq1((IDX_DIM, LANES)), pl.BlockSpec(memory_space=pl.ANY)],
            out_specs=pl.BlockSpec((SUBLANES, past + LANES), lambda bb, pt: (bb, 0)),
            scratch_shapes=[pltpu.VMEM((PAGE_SLOTS, pp, IDX_DIM, PAGE), F32), pltpu.SemaphoreType.DMA((PAGE_SLOTS,))],
        ),
        out_shape=jax.ShapeDtypeStruct((bd * SUBLANES, past + LANES), F32),
        compiler_params=_params("arbitrary"),
        name="dsa_sample_keys",
    )(page_table, iq2, iw2, ikt_new, ik_cache_t)


def _dsa_sample_select_body(sc_ref, sel_ref, thr_ref, *, topk):
    rows, n = sc_ref.shape
    thr, _ = _kth_largest(lambda: sc_ref[...], rows, topk)
    thr_ref[...] = jnp.broadcast_to(thr, thr_ref.shape)
    sel_ref[...] = sc_ref[...]

    @pl.when(jnp.max(_count(sc_ref[...] >= thr)) > topk)
    def _():
        need = topk - _count(sc_ref[...] > thr)
        earlier = _earlier_in_chunk()

        def chunk(c, seen):
            lanes = pl.ds(pl.multiple_of(c * LANES, LANES), LANES)
            sc = sc_ref[:, lanes]
            eq = sc == thr
            keep, seen = _tied_keys_to_keep(eq, seen, need, earlier)
            sel_ref[:, lanes] = jnp.where(eq & jnp.logical_not(keep), NEG_INF, sc)
            return seen

        lax.fori_loop(0, n // LANES, chunk, jnp.zeros((rows, 1), F32))


def _dsa_sample_select(scores, topk, seqs_per_step):
    rows_total, n = scores.shape
    rows = seqs_per_step * SUBLANES
    return pl.pallas_call(
        functools.partial(_dsa_sample_select_body, topk=topk),
        grid=(rows_total // rows,),
        in_specs=[pl.BlockSpec((rows, n), lambda i: (i, 0))],
        out_specs=[pl.BlockSpec((rows, n), lambda i: (i, 0)), pl.BlockSpec((rows, LANES), lambda i: (i, 0))],
        out_shape=[jax.ShapeDtypeStruct((rows_total, n), F32), jax.ShapeDtypeStruct((rows_total, LANES), F32)],
        compiler_params=_params("arbitrary"),
        name="dsa_sample_select",
    )(scores)


PAGE_SLOTS = 4
PAGES_AHEAD = 3


def _chunk_pages(n_pages, most=8):
    for pp in (16, 8, 4, 2, 1):
        if pp <= most and n_pages % pp == 0 and (n_pages // pp) % PAGE_SLOTS == 0:
            return pp
    raise ValueError(f"need a multiple of {PAGE_SLOTS} page chunks, got {n_pages} pages")


def _walk_page_chunks(pt_ref, streams, pp, n_chunks, page_of, compute):
    b = pl.program_id(0)
    assert n_chunks % PAGE_SLOTS == 0 and PAGES_AHEAD < PAGE_SLOTS and PAGES_AHEAD <= n_chunks

    def copies(seq, c, slot, to_wait=False):
        out = []
        for r in range(pp):
            page = 0 if to_wait else pt_ref[seq, page_of(c, r)]
            for hbm, buf, sem in streams:
                out.append(pltpu.make_async_copy(hbm.at[page], buf.at[slot, r], sem.at[slot]))
        return out

    @pl.when(b == 0)
    def _():
        for c in range(PAGES_AHEAD):
            for d in copies(b, c, c % PAGE_SLOTS):
                d.start()

    for c in range(n_chunks):
        for d in copies(b, c, c % PAGE_SLOTS, to_wait=True):
            d.wait()
        ahead = c + PAGES_AHEAD
        if ahead < n_chunks:
            for d in copies(b, ahead, ahead % PAGE_SLOTS):
                d.start()
        else:
            @pl.when(b + 1 < pl.num_programs(0))
            def _(ahead=ahead):
                for d in copies(b + 1, ahead - n_chunks, ahead % PAGE_SLOTS):
                    d.start()
        compute(c, c % PAGE_SLOTS)


def _dsa_sample_attend_body(pt_ref, q_ref, sc_ref, thr_ref, knew_ref, vnew_ref, k_hbm, v_hbm, o_ref,
                            kbuf, vbuf, ksem, vsem, m_ref, l_ref, acc_ref, *, pp, n_chunks):
    past = n_chunks * pp * PAGE
    thr = jnp.concatenate([thr_ref[:, 0:1]] * N_KV, axis=0)

    def selected(scores):
        scores = jnp.concatenate([scores] * N_KV, axis=0)
        return (scores >= thr) & (scores > NEG_INF)

    _softmax_init(m_ref, l_ref, acc_ref)
    _softmax_update(q_ref, lambda kh: knew_ref[kh].astype(BF16), lambda kh: vnew_ref[kh].astype(BF16),
                    None, selected(sc_ref[:, past:past + LANES]), m_ref, l_ref, acc_ref)

    def compute(c, slot):
        k_pages = [kbuf.at[slot, r] for r in range(pp)]
        v_pages = [vbuf.at[slot, r] for r in range(pp)]
        sel = selected(sc_ref[:, c * pp * PAGE:(c + 1) * pp * PAGE])
        _softmax_update(q_ref, functools.partial(_kv_rows, k_pages), functools.partial(_kv_rows, v_pages),
                        None, sel, m_ref, l_ref, acc_ref)

    _walk_page_chunks(pt_ref, [(k_hbm, kbuf, ksem), (v_hbm, vbuf, vsem)], pp, n_chunks,
                      lambda c, r: c * pp + r, compute)
    o_ref[...] = acc_ref[...] / l_ref[...]


def _per_seq1(shape):
    return pl.BlockSpec((None,) + shape, lambda bb, pt: (bb,) + (0,) * len(shape))


def _kv_page_scratch(pp):
    page = (PAGE * N_KV, HEAD_DIM)
    return [pltpu.VMEM((PAGE_SLOTS, pp) + page, F32), pltpu.VMEM((PAGE_SLOTS, pp) + page, F32),
            pltpu.SemaphoreType.DMA((PAGE_SLOTS,)), pltpu.SemaphoreType.DMA((PAGE_SLOTS,))]


def _dsa_sample_attend(page_table, q_s, scores, thr, k_new, v_new, k_cache, v_cache):
    bd, n_pages = page_table.shape
    pp = _chunk_pages(n_pages)
    n_cols = scores.shape[1]
    hbm = pl.BlockSpec(memory_space=pl.ANY)
    return pl.pallas_call(
        functools.partial(_dsa_sample_attend_body, pp=pp, n_chunks=n_pages // pp),
        grid_spec=pltpu.PrefetchScalarGridSpec(
            num_scalar_prefetch=1,
            grid=(bd,),
            in_specs=[_per_seq1((Q_ROWS, HEAD_DIM)),
                      pl.BlockSpec((SUBLANES, n_cols), lambda bb, pt: (bb, 0)),
                      pl.BlockSpec((SUBLANES, LANES), lambda bb, pt: (bb, 0)),
                      _per_seq1((N_KV, LANES, HEAD_DIM)), _per_seq1((N_KV, LANES, HEAD_DIM)), hbm, hbm],
            out_specs=_per_seq1((Q_ROWS, HEAD_DIM)),
            scratch_shapes=_kv_page_scratch(pp) + _softmax_scratch(),
        ),
        out_shape=jax.ShapeDtypeStruct((bd, Q_ROWS, HEAD_DIM), F32),
        compiler_params=_params("arbitrary"),
        name="dsa_sample_attend",
    )(page_table, q_s, scores, thr, k_new, v_new, k_cache, v_cache)


def _fox_sample_body(pt_ref, q_ref, lfn_ref, knew_ref, vnew_ref, rep_ref, later_ref, lf_hbm, k_hbm, v_hbm, o_ref,
                     lfbuf, kbuf, vbuf, lfsem, ksem, vsem, m_ref, l_ref, acc_ref, cq_ref, carry_ref,
                     *, pp, n_chunks, n_new):
    hi = lax.Precision.HIGHEST
    n_pages = pp * n_chunks

    _softmax_init(m_ref, l_ref, acc_ref)
    carry_ref[...] = jnp.zeros(carry_ref.shape, F32)
    r_io = lax.broadcasted_iota(I32, (LANES, LANES), 0)
    c_io = lax.broadcasted_iota(I32, (LANES, LANES), 1)
    incl = jnp.where(r_io <= c_io, 1.0, 0.0).astype(F32)
    cum = jnp.dot(lfn_ref[...], incl, precision=hi, preferred_element_type=F32)
    cg = jnp.dot(rep_ref[0:Q_ROWS, 0:N_HEADS], cum, precision=hi, preferred_element_type=F32)
    row = lax.broadcasted_iota(I32, (Q_ROWS, LANES), 0)
    lane = lax.broadcasted_iota(I32, (Q_ROWS, LANES), 1)
    own = lane == row % n_new
    cq_ref[...] = jnp.sum(jnp.where(own, cg, 0.0), axis=1, keepdims=True)
    _softmax_update(q_ref, lambda kh: knew_ref[kh].astype(BF16), lambda kh: vnew_ref[kh].astype(BF16),
                    cq_ref[...] - cg, lane <= row % n_new, m_ref, l_ref, acc_ref)

    def compute(c, slot):
        k_pages = [kbuf.at[slot, r] for r in range(pp)]
        v_pages = [vbuf.at[slot, r] for r in range(pp)]
        lf_all = jnp.concatenate([lfbuf[slot, r] for r in range(pp)], axis=0)
        lf_rows = jnp.dot(rep_ref[...], lf_all, precision=hi, preferred_element_type=F32)
        within = jnp.dot(lf_rows, later_ref[...], precision=hi, preferred_element_type=F32)
        total = within[:, 0:1] + lf_rows[:, 0:1]
        run = carry_ref[...]
        biases = []
        for r in range(pp):
            biases.append(within[r * Q_ROWS:(r + 1) * Q_ROWS] + (run + cq_ref[...]))
            run = run + total[r * Q_ROWS:(r + 1) * Q_ROWS]
        carry_ref[...] = run
        _softmax_update(q_ref, functools.partial(_kv_rows, k_pages), functools.partial(_kv_rows, v_pages),
                        jnp.concatenate(biases, axis=1), None, m_ref, l_ref, acc_ref)

    _walk_page_chunks(pt_ref, [(lf_hbm, lfbuf, lfsem), (k_hbm, kbuf, ksem), (v_hbm, vbuf, vsem)], pp, n_chunks,
                      lambda c, r: n_pages - 1 - (c * pp + r), compute)
    o_ref[...] = acc_ref[...] / l_ref[...]


def _fox_sample(page_table, q_s, lft_new, k_new, v_new, lf_cache_t, k_cache, v_cache, n_new):
    bd, n_pages = page_table.shape
    pp = _chunk_pages(n_pages)
    row_head = np.arange(Q_ROWS) // n_new
    rep_one = (row_head[:, None] == np.arange(N_HEADS)[None, :]).astype(np.float32)
    rep = jnp.asarray(np.kron(np.eye(pp, dtype=np.float32), rep_one))
    later = jnp.asarray((np.arange(PAGE)[:, None] > np.arange(PAGE)[None, :]).astype(np.float32))
    hbm = pl.BlockSpec(memory_space=pl.ANY)
    shared = lambda shape: pl.BlockSpec(shape, lambda bb, pt: (0,) * len(shape))
    return pl.pallas_call(
        functools.partial(_fox_sample_body, pp=pp, n_chunks=n_pages // pp, n_new=n_new),
        grid_spec=pltpu.PrefetchScalarGridSpec(
            num_scalar_prefetch=1,
            grid=(bd,),
            in_specs=[_per_seq1((Q_ROWS, HEAD_DIM)), _per_seq1((N_HEADS, LANES)),
                      _per_seq1((N_KV, LANES, HEAD_DIM)), _per_seq1((N_KV, LANES, HEAD_DIM)),
                      shared((pp * Q_ROWS, pp * N_HEADS)), shared((PAGE, PAGE)), hbm, hbm, hbm],
            out_specs=_per_seq1((Q_ROWS, HEAD_DIM)),
            scratch_shapes=[pltpu.VMEM((PAGE_SLOTS, pp, N_HEADS, PAGE), F32)] + _kv_page_scratch(pp)[:2]
            + [pltpu.SemaphoreType.DMA((PAGE_SLOTS,))] * 3
            + _softmax_scratch() + [pltpu.VMEM((Q_ROWS, 1), F32), pltpu.VMEM((Q_ROWS, 1), F32)],
        ),
        out_shape=jax.ShapeDtypeStruct((bd, Q_ROWS, HEAD_DIM), F32),
        compiler_params=_params("arbitrary"),
        name="fox_sample",
    )(page_table, q_s, lft_new, k_new, v_new, rep, later, lf_cache_t, k_cache, v_cache)


def _merge_body(x_ref, oa_ref, ob_ref, ga_ref, gb_ref, wa_ref, wb_ref, wo_ref, gn_ref, x1_ref, h2_ref):
    a = jnp.dot(oa_ref[...].astype(BF16), wa_ref[...], preferred_element_type=F32)
    b = jnp.dot(ob_ref[...].astype(BF16), wb_ref[...], preferred_element_type=F32)
    merged = jax.nn.sigmoid(ga_ref[...].astype(F32)) * a + jax.nn.sigmoid(gb_ref[...].astype(F32)) * b
    x1 = x_ref[...] + jnp.dot(merged.astype(BF16), wo_ref[...], preferred_element_type=F32)
    x1_ref[...] = x1
    h2_ref[...] = _rms(x1, gn_ref[...]).astype(BF16)


def _merge(x2d, o_a, o_b, proj, wa, wb, wo, ffn_norm, lay, tm):
    n, d = x2d.shape
    const = lambda shape: pl.BlockSpec(shape, lambda i: (0, 0), pipeline_mode=pl.Buffered(1))
    return pl.pallas_call(
        _merge_body,
        grid=(n // tm,),
        in_specs=[
            pl.BlockSpec((tm, d), lambda i: (i, 0)),
            pl.BlockSpec((tm, 1024), lambda i: (i, 0)),
            pl.BlockSpec((tm, 1024), lambda i: (i, 0)),
            pl.BlockSpec((tm, d), lambda i: (i, lay.ga // d)),
            pl.BlockSpec((tm, d), lambda i: (i, lay.gb // d)),
            const((1024, d)), const((1024, d)), const((d, d)), const((1, d)),
        ],
        out_specs=[pl.BlockSpec((tm, d), lambda i: (i, 0)), pl.BlockSpec((tm, d), lambda i: (i, 0))],
        out_shape=[jax.ShapeDtypeStruct((n, d), F32), jax.ShapeDtypeStruct((n, d), BF16)],
        compiler_params=_params("arbitrary"),
        name="merge",
    )(x2d, o_a, o_b, proj, proj, wa, wb, wo, ffn_norm)


FFN_ROW_CHUNKS = 2


def _ffn_zero_acc(f, acc_ref):
    @pl.when(f == 0)
    def _():
        acc_ref[...] = jnp.zeros(acc_ref.shape, F32)


def _ffn_finish(f, n_f, x1_ref, fn_ref, y_ref, acc_ref):
    @pl.when(f == n_f - 1)
    def _():
        y_ref[...] = _rms(x1_ref[...] + acc_ref[...], fn_ref[...])


def _ffn_rows(h, first, rows, n_rows, taps, wg_ref, wu_ref, wd_ref, cw_ref, cb_ref, ext_ref, acc_ref):
    gp = jnp.dot(h, wg_ref[...], preferred_element_type=F32)
    up = jnp.dot(h, wu_ref[...], preferred_element_type=F32)
    ext_ref[first + rows:first + rows + n_rows, :] = gp
    conv = cb_ref[...]
    for j, back in enumerate(taps):
        src = gp if back == 0 else ext_ref[first + rows - back:first + rows - back + n_rows, :]
        conv = conv + cw_ref[j:j + 1, :] * src
    act = (conv * jax.nn.sigmoid(conv)) * up
    acc_ref[rows:rows + n_rows, :] += jnp.dot(act.astype(BF16), wd_ref[...], preferred_element_type=F32)


def _ffn_prompt_body(h_ref, halo_ref, wg_ref, wu_ref, wd_ref, cw_ref, cb_ref, x1_ref, fn_ref,
                     y_ref, tail_ref, acc_ref, ext_ref, *, tiles_per_seq):
    i = pl.program_id(0)
    f = pl.program_id(1)
    tm = h_ref.shape[0]
    _ffn_zero_acc(f, acc_ref)
    halo = jnp.dot(halo_ref[...], wg_ref[...], preferred_element_type=F32)
    ext_ref[0:SUBLANES, :] = jnp.where(i % tiles_per_seq == 0, 0.0, halo)
    taps = tuple(CONV_W - 1 - j for j in range(CONV_W))
    rc = tm // FFN_ROW_CHUNKS
    for c in range(FFN_ROW_CHUNKS):
        _ffn_rows(h_ref[c * rc:(c + 1) * rc, :], SUBLANES, c * rc, rc, taps,
                  wg_ref, wu_ref, wd_ref, cw_ref, cb_ref, ext_ref, acc_ref)
    tail_ref[...] = ext_ref[tm:tm + SUBLANES, :]
    _ffn_finish(f, pl.num_programs(1), x1_ref, fn_ref, y_ref, acc_ref)


def _ffn_prompt(h2, x1, wg, wu, wd, conv_w, conv_b, final_norm, s, tm, tf):
    n, d = x1.shape
    ff = wg.shape[1]
    assert s % tm == 0 and tm % SUBLANES == 0
    hb = tm // SUBLANES
    return pl.pallas_call(
        functools.partial(_ffn_prompt_body, tiles_per_seq=s // tm),
        grid=(n // tm, ff // tf),
        in_specs=[
            pl.BlockSpec((tm, d), lambda i, f: (i, 0)),
            pl.BlockSpec((SUBLANES, d), lambda i, f: (jnp.maximum(i * hb - 1, 0), 0)),
            pl.BlockSpec((d, tf), lambda i, f: (0, f)),
            pl.BlockSpec((d, tf), lambda i, f: (0, f)),
            pl.BlockSpec((tf, d), lambda i, f: (f, 0)),
            pl.BlockSpec((CONV_W, tf), lambda i, f: (0, f)),
            pl.BlockSpec((1, tf), lambda i, f: (0, f)),
            pl.BlockSpec((tm, d), lambda i, f: (i, 0)),
            pl.BlockSpec((1, d), lambda i, f: (0, 0)),
        ],
        out_specs=[pl.BlockSpec((tm, d), lambda i, f: (i, 0)),
                   pl.BlockSpec((SUBLANES, tf), lambda i, f: (i, f))],
        out_shape=[jax.ShapeDtypeStruct((n, d), F32), jax.ShapeDtypeStruct((n // tm * SUBLANES, ff), F32)],
        scratch_shapes=[pltpu.VMEM((tm, d), F32), pltpu.VMEM((tm + SUBLANES, tf), F32)],
        compiler_params=_params("arbitrary", "arbitrary"),
        name="ffn_prompt",
    )(h2, h2, wg, wu, wd, conv_w, conv_b, x1, final_norm)


def _ffn_sample_body(h_ref, st_ref, wg_ref, wu_ref, wd_ref, cw_ref, cb_ref, x1_ref, fn_ref,
                     y_ref, new_st_ref, acc_ref, ext_ref, *, bd):
    f = pl.program_id(0)
    n = h_ref.shape[0]
    keep = (CONV_W - 1) * bd
    _ffn_zero_acc(f, acc_ref)
    ext_ref[0:keep, :] = st_ref[...]
    taps = tuple((CONV_W - 1 - j) * bd for j in range(CONV_W))
    _ffn_rows(h_ref[...], keep, 0, n, taps, wg_ref, wu_ref, wd_ref, cw_ref, cb_ref, ext_ref, acc_ref)
    new_st_ref[...] = ext_ref[n:n + keep, :]
    _ffn_finish(f, pl.num_programs(0), x1_ref, fn_ref, y_ref, acc_ref)


def _ffn_sample(h2, x1, state, wg, wu, wd, conv_w, conv_b, final_norm, bd, tf):
    n, d = x1.shape
    ff = wg.shape[1]
    keep = (CONV_W - 1) * bd
    assert bd % SUBLANES == 0 and n >= keep
    return pl.pallas_call(
        functools.partial(_ffn_sample_body, bd=bd),
        grid=(ff // tf,),
        in_specs=[
            pl.BlockSpec((n, d), lambda f: (0, 0)),
            pl.BlockSpec((keep, tf), lambda f: (0, f)),
            pl.BlockSpec((d, tf), lambda f: (0, f)),
            pl.BlockSpec((d, tf), lambda f: (0, f)),
            pl.BlockSpec((tf, d), lambda f: (f, 0)),
            pl.BlockSpec((CONV_W, tf), lambda f: (0, f)),
            pl.BlockSpec((1, tf), lambda f: (0, f)),
            pl.BlockSpec((n, d), lambda f: (0, 0)),
            pl.BlockSpec((1, d), lambda f: (0, 0)),
        ],
        out_specs=[pl.BlockSpec((n, d), lambda f: (0, 0)), pl.BlockSpec((keep, tf), lambda f: (0, f))],
        out_shape=[jax.ShapeDtypeStruct((n, d), F32), jax.ShapeDtypeStruct((keep, ff), F32)],
        scratch_shapes=[pltpu.VMEM((n, d), F32), pltpu.VMEM((keep + n, tf), F32)],
        compiler_params=_params("arbitrary"),
        name="ffn_sample",
    )(h2, state, wg, wu, wd, conv_w, conv_b, x1, final_norm)


def _largest_divisor(n, candidates):
    for c in candidates:
        if n % c == 0:
            return c
    raise ValueError(f"no tile for {n} among {candidates}")


def kernel(x_prompt, x_sample, cache_dsa_k, cache_dsa_v, cache_idx_k, cache_fox_k, cache_fox_v, cache_fox_logf, state_ffn_conv, page_table, attn_norm, w_in, b_forget, w_branch_a, w_branch_b, w_out, ffn_norm, w_gate, w_up, w_down, conv_w, conv_b, final_norm):
    b, s, d = x_prompt.shape
    bd, t_new, _ = x_sample.shape
    depth = attn_norm.shape[0]
    assert depth == 1 and t_new * GROUP == SUBLANES and s % Q_BLOCK == 0
    n_pages = page_table.shape[1]
    past = n_pages * PAGE
    n_pool = cache_dsa_k.shape[1]
    ff = w_gate.shape[2]
    lay = _Layout(d)

    w_perm = _prep_w_in_t(jnp.swapaxes(w_in[0], 0, 1), lay)
    bf_row = jnp.zeros((1, LANES), F32).at[0, LOGF_LANE:LOGF_LANE + N_HEADS].set(b_forget[0])
    wa, wb, wo = (w[0].astype(BF16) for w in (w_branch_a, w_branch_b, w_out))
    wg, wu, wd = (w[0].astype(BF16) for w in (w_gate, w_up, w_down))
    g_attn, g_ffn, g_fin = attn_norm[0][None, :], ffn_norm[0][None, :], final_norm[None, :]
    cw, cb = conv_w[0], conv_b[0][None, :]

    tm_p = _largest_divisor(s, (1024, 512, 256, 128))
    tab_p = _rope_tables(jnp.arange(s, dtype=I32))
    tab_s = jnp.tile(_rope_tables(past + jnp.arange(t_new, dtype=I32)), (bd, 1))
    xp2 = x_prompt.reshape(b * s, d)
    xs2 = x_sample.reshape(bd * t_new, d)
    proj_p, small_p, *kv_p = _norm_proj(xp2, g_attn, w_perm, tab_p, bf_row, lay, tm_p)
    proj_s, small_s, *kv_s = _norm_proj(xs2, g_attn, w_perm, tab_s, bf_row, lay, bd * t_new)

    oa_p = _dsa_prompt(proj_p, small_p, kv_p[0], kv_p[1], b, s, lay)
    ob_p = _fox_prompt(proj_p, small_p, kv_p[2], kv_p[3], b, s, lay)

    def cols(name, width):
        o = getattr(lay, name)
        return proj_s[:, o:o + width].astype(F32).reshape(bd, t_new, width)

    def heads_major(x):
        x = x.reshape(bd, t_new, N_KV, GROUP, HEAD_DIM).transpose(0, 2, 3, 1, 4)
        return x.reshape(bd, Q_ROWS, HEAD_DIM)

    def new_kv(x):
        x = x.reshape(bd, t_new, N_KV, HEAD_DIM).transpose(0, 2, 1, 3)
        return jnp.pad(x, ((0, 0), (0, 0), (0, LANES - t_new), (0, 0)))

    def heads_back(o):
        o = o.reshape(bd, N_KV, GROUP, t_new, HEAD_DIM).transpose(0, 3, 1, 2, 4)
        return o.reshape(bd * t_new, N_HEADS * HEAD_DIM)

    ka_s, va_s, kb_s, vb_s = (x.reshape(bd, t_new, N_KV * HEAD_DIM) for x in kv_s)
    ik_s = small_s[:, lay.small_ik:lay.small_ik + IDX_DIM].reshape(bd, t_new, IDX_DIM)
    iwf_s = small_s[:, lay.small_iwf:lay.small_iwf + LANES].reshape(bd, t_new, LANES)
    logf_s = iwf_s[..., LOGF_LANE:LOGF_LANE + N_HEADS]
    iq_s = cols("iq", 1024).reshape(bd, t_new, IDX_HEADS, IDX_DIM).transpose(0, 2, 1, 3)
    iq2 = jnp.broadcast_to(iq_s[:, :, None], (bd, IDX_HEADS, GROUP, t_new, IDX_DIM)).reshape(bd, IDX_HEADS * SUBLANES, IDX_DIM)
    iw_s = iwf_s[..., IW_LANE:IW_LANE + IDX_HEADS].transpose(0, 2, 1)
    iw2 = jnp.broadcast_to(iw_s[:, :, None], (bd, IDX_HEADS, GROUP, t_new)).reshape(bd, IDX_HEADS * SUBLANES, 1)
    ikt_new = jnp.pad(ik_s.transpose(0, 2, 1), ((0, 0), (0, 0), (0, LANES - t_new)))
    lft_new = jnp.pad(logf_s.transpose(0, 2, 1), ((0, 0), (0, 0), (0, LANES - t_new)))

    ik_cache_t = jnp.swapaxes(cache_idx_k[0], 1, 2)
    lf_cache_t = jnp.swapaxes(cache_fox_logf[0], 1, 2)
    kv_rows = lambda c: c.reshape(n_pool, PAGE * N_KV, HEAD_DIM)

    scores = _dsa_sample_keys(page_table, iq2, iw2, ikt_new, ik_cache_t, t_new)
    scores, thr = _dsa_sample_select(scores, min(TOPK_MAX, (past + t_new) // 4), _largest_divisor(bd, (8, 4, 2, 1)))
    oa_s = heads_back(_dsa_sample_attend(page_table, heads_major(cols("qa", 1024)), scores, thr,
                                         new_kv(ka_s), new_kv(va_s), kv_rows(cache_dsa_k), kv_rows(cache_dsa_v)))
    ob_s = heads_back(_fox_sample(page_table, heads_major(cols("qb", 1024)), lft_new, new_kv(kb_s), new_kv(vb_s),
                                  lf_cache_t, kv_rows(cache_fox_k), kv_rows(cache_fox_v), t_new))

    tm_m = _largest_divisor(s, (256, 128))
    x1_p, h2_p = _merge(xp2, oa_p, ob_p, proj_p, wa, wb, wo, g_ffn, lay, tm_m)
    x1_s, h2_s = _merge(xs2, oa_s, ob_s, proj_s, wa, wb, wo, g_ffn, lay, bd * t_new)

    tf = _largest_divisor(ff, (512, 256, 128))
    tm_f = _largest_divisor(s, (512, 256, 128))
    y_p, tails = _ffn_prompt(h2_p, x1_p, wg, wu, wd, cw, cb, g_fin, s, tm_f, tf)
    conv_p = tails.reshape(b, s // tm_f, SUBLANES, ff)[:, -1, SUBLANES - (CONV_W - 1):, :]

    t_major = lambda x: x.reshape(bd, t_new, -1).transpose(1, 0, 2).reshape(t_new * bd, -1)
    state_t = state_ffn_conv[0].transpose(1, 0, 2).reshape((CONV_W - 1) * bd, ff)
    y_s_t, st_t = _ffn_sample(t_major(h2_s), t_major(x1_s), state_t, wg, wu, wd, cw, cb, g_fin, bd, tf)
    y_s = y_s_t.reshape(t_new, bd, d).transpose(1, 0, 2)
    conv_s = st_t.reshape(CONV_W - 1, bd, ff).transpose(1, 0, 2)

    p_kv = [x.reshape(1, b, s, N_KV, HEAD_DIM) for x in kv_p]
    logf_at = lay.small_iwf + LOGF_LANE
    p_out = (p_kv[0], p_kv[1], small_p[:, lay.small_ik:lay.small_ik + IDX_DIM].reshape(1, b, s, IDX_DIM), p_kv[2], p_kv[3],
             small_p[:, logf_at:logf_at + N_HEADS].reshape(1, b, s, N_HEADS),
             conv_p[None])
    s_out = (ka_s.reshape(1, bd, t_new, N_KV, HEAD_DIM), va_s.reshape(1, bd, t_new, N_KV, HEAD_DIM), ik_s[None],
             kb_s.reshape(1, bd, t_new, N_KV, HEAD_DIM), vb_s.reshape(1, bd, t_new, N_KV, HEAD_DIM), logf_s[None],
             conv_s[None])
    return (y_p.reshape(b, s, d), y_s) + p_out + s_out
```
